```python
import math
import jax, jax.numpy as jnp
from jax import lax
import numpy as np

D_MODEL = 2048
BATCH = 4
SEQ = 2048
DEPTH = 4
DEC_BATCH = 8
DEC_SEQ = 8
PAST_LEN = 16384
PAGE_SIZE = 128

N_A_LAYERS = (DEPTH + 1) // 2
N_B_LAYERS = DEPTH // 2

A_GROUPS = ((128, 1), (512, 4), (2048, 16))
N_GROUPS = 3
A_HEADS = 8
A_HEAD_DIM = 128
A_STEPS = 128
N_KEYS = A_STEPS + 1
A_BLOCK = 128
A_IN_WIDTH = N_GROUPS * 3 * A_HEADS * A_HEAD_DIM
A_OUT_WIDTH = A_HEADS * A_HEAD_DIM

NUM_BUCKETS = 32
MAX_DISTANCE = 2048

B_HEADS = 8
B_QK_DIM = D_MODEL // 2 // B_HEADS
B_V_DIM = D_MODEL // B_HEADS
B_CHUNK = 64
B_IN_WIDTH = 2 * B_HEADS * B_QK_DIM + 2 * B_HEADS * B_V_DIM + 2 * B_HEADS

D_FF = -(-8 * D_MODEL // (3 * 256)) * 256

EPS = 1e-6
NEG = -1e30

kernel_name = "dilated_swa_mlstm_hybrid_step"


def rmsnorm(x, g):
    xf = x.astype(jnp.float32)
    y = xf * lax.rsqrt(jnp.mean(xf * xf, axis=-1, keepdims=True) + EPS)
    return (y * g.astype(jnp.float32)).astype(x.dtype)


def t5_bucket(dist):
    max_exact = NUM_BUCKETS // 2
    d = jnp.maximum(dist.astype(jnp.float32), 1.0)
    large = max_exact + jnp.log(d / max_exact) / math.log(MAX_DISTANCE / max_exact) * (NUM_BUCKETS - max_exact)
    large = jnp.minimum(large.astype(jnp.int32), NUM_BUCKETS - 1)
    return jnp.where(dist < max_exact, dist, large)


def group_bias(rel_bias, g):
    dil = A_GROUPS[g][1]
    buckets = t5_bucket(jnp.arange(N_KEYS, dtype=jnp.int32) * dil)
    return rel_bias[buckets, g * A_HEADS:(g + 1) * A_HEADS].T.astype(jnp.float32)


def dilated_band_attention(q, k, v, bias, dil):
    B, S, H, E = q.shape
    L = S // dil
    nb = -(-L // A_BLOCK)
    pad = nb * A_BLOCK - L

    def split(t):
        t = t.reshape(B, L, dil, H, E).transpose(0, 2, 1, 3, 4)
        t = jnp.pad(t, ((0, 0), (0, 0), (0, pad), (0, 0), (0, 0)))
        return t.reshape(B, dil, nb, A_BLOCK, H, E)

    def with_prev(t):
        prev = jnp.pad(t[:, :, :-1], ((0, 0), (0, 0), (1, 0), (0, 0), (0, 0), (0, 0)))
        return jnp.concatenate([prev, t], axis=3)

    qb = split(q)
    kk = with_prev(split(k))
    vv = with_prev(split(v))
    qi = jnp.arange(A_BLOCK)[:, None] + A_BLOCK
    ki = jnp.arange(2 * A_BLOCK)[None, :]
    step = qi - ki
    blk = jnp.arange(nb)[:, None, None]
    valid = (step >= 0) & (step <= A_STEPS) & (blk * A_BLOCK + ki - A_BLOCK >= 0)
    bias_band = bias[:, jnp.clip(step, 0, A_STEPS)]
    logits = jnp.einsum('brnqhe,brnkhe->brnhqk', qb, kk,
                        preferred_element_type=jnp.float32) * (A_HEAD_DIM ** -0.5)
    logits = logits + bias_band[None, None, None]
    logits = jnp.where(valid[None, None, :, None], logits, NEG)
    lse = jax.nn.logsumexp(logits, axis=-1)
    p = jnp.exp(logits - lse[..., None]).astype(v.dtype)
    o = jnp.einsum('brnhqk,brnkhe->brnqhe', p, vv)
    o = o.reshape(B, dil, nb * A_BLOCK, H, E)[:, :, :L]
    o = o.transpose(0, 2, 1, 3, 4).reshape(B, S, H, E)
    lse = lse.transpose(0, 1, 2, 4, 3).reshape(B, dil, nb * A_BLOCK, H)[:, :, :L]
    lse = lse.transpose(0, 2, 1, 3).reshape(B, S, H)
    return o, lse


def dilated_gather_attention(q, k_all, v_all, bias, dil, n_past):
    T = q.shape[1]
    idx = n_past + jnp.arange(T)[:, None] - jnp.arange(N_KEYS)[None, :] * dil
    valid = idx >= 0
    idx = jnp.clip(idx, 0)
    kg = k_all[:, idx]
    vg = v_all[:, idx]
    logits = jnp.einsum('bthe,btjhe->bthj', q, kg,
                        preferred_element_type=jnp.float32) * (A_HEAD_DIM ** -0.5)
    logits = logits + bias[None, None]
    logits = jnp.where(valid[None, :, None, :], logits, NEG)
    lse = jax.nn.logsumexp(logits, axis=-1)
    p = jnp.exp(logits - lse[..., None]).astype(v_all.dtype)
    o = jnp.einsum('bthj,btjhe->bthe', p, vg)
    return o, lse


def merge_groups(outs, lses):
    w = jax.nn.softmax(jnp.stack(lses, axis=0), axis=0)
    y = jnp.einsum('gbsh,gbshe->bshe', w, jnp.stack(outs, axis=0).astype(jnp.float32))
    return y.astype(outs[0].dtype)


def mixer_a_prompt(h, w_in, w_out, biases):
    B, S, _ = h.shape
    qkv = (h @ w_in).reshape(B, S, N_GROUPS, 3, A_HEADS, A_HEAD_DIM)
    outs, lses, rows = [], [], []
    for g, (window, dil) in enumerate(A_GROUPS):
        o, lse = dilated_band_attention(qkv[:, :, g, 0], qkv[:, :, g, 1], qkv[:, :, g, 2], biases[g], dil)
        outs.append(o)
        lses.append(lse)
        keep = min(window, S)
        rows.append(qkv[:, S - keep:, g, 1:])
    y = merge_groups(outs, lses).reshape(B, S, A_OUT_WIDTH) @ w_out
    return y, rows


def mixer_a_sample(h, caches, w_in, w_out, biases):
    B, T, _ = h.shape
    qkv = (h @ w_in).reshape(B, T, N_GROUPS, 3, A_HEADS, A_HEAD_DIM)
    outs, lses, rows = [], [], []
    for g, (window, dil) in enumerate(A_GROUPS):
        kv_new = qkv[:, :, g, 1:]
        kv_all = jnp.concatenate([caches[g].astype(kv_new.dtype), kv_new], axis=1)
        o, lse = dilated_gather_attention(qkv[:, :, g, 0], kv_all[:, :, 0], kv_all[:, :, 1],
                                          biases[g], dil, caches[g].shape[1])
        outs.append(o)
        lses.append(lse)
        rows.append(kv_new)
    y = merge_groups(outs, lses).reshape(B, T, A_OUT_WIDTH) @ w_out
    return y, rows


def mlstm_chunk(state, q, k, v, i_pre, logf):
    C, n, m = state
    L = q.shape[1]
    qf, kf, vf = q.astype(jnp.float32), k.astype(jnp.float32), v.astype(jnp.float32)
    b = jnp.cumsum(logf, axis=1)
    a = b + m[:, None]
    D = b[:, :, None] - b[:, None, :] + i_pre[:, None, :]
    causal = jnp.tril(jnp.ones((L, L), dtype=bool))
    D = jnp.where(causal[None, :, :, None], D, NEG)
    m_t = jnp.maximum(a, jnp.max(D, axis=2))
    w_inter = jnp.exp(a - m_t)
    s = jnp.einsum('bqhd,bshd->bqsh', qf, kf) * jnp.exp(D - m_t[:, :, None])
    num = w_inter[..., None] * jnp.einsum('bqhd,bhde->bqhe', qf, C) + jnp.einsum('bqsh,bshe->bqhe', s, vf)
    den = w_inter * jnp.einsum('bqhd,bhd->bqh', qf, n) + jnp.sum(s, axis=2)
    h = num / jnp.maximum(jnp.abs(den), jnp.exp(-m_t))[..., None]
    m_new = m_t[:, -1]
    w_state = jnp.exp(b[:, -1:] - b + i_pre - m_new[:, None])
    decay = jnp.exp(b[:, -1] + m - m_new)
    C_new = decay[..., None, None] * C + jnp.einsum('bsh,bshd,bshe->bhde', w_state, kf, vf)
    n_new = decay[..., None] * n + jnp.einsum('bsh,bshd->bhd', w_state, kf)
    return (C_new, n_new, m_new), h


def mixer_b(h, state, w_in, b_gate, g_norm, w_out, chunk_len):
    B, S, _ = h.shape
    qk, vd = B_HEADS * B_QK_DIM, B_HEADS * B_V_DIM
    z = h @ w_in
    q = z[..., :qk].reshape(B, S, B_HEADS, B_QK_DIM)
    k = z[..., qk:2 * qk].reshape(B, S, B_HEADS, B_QK_DIM) * (B_QK_DIM ** -0.5)
    v = z[..., 2 * qk:2 * qk + vd].reshape(B, S, B_HEADS, B_V_DIM)
    o = jax.nn.sigmoid(z[..., 2 * qk + vd:2 * qk + 2 * vd].astype(jnp.float32))
    gates = z[..., 2 * qk + 2 * vd:].astype(jnp.float32) + b_gate.astype(jnp.float32)
    i_pre = gates[..., :B_HEADS]
    logf = jax.nn.log_sigmoid(gates[..., B_HEADS:])
    nc = S // chunk_len

    def chunks(t):
        return jnp.swapaxes(t.reshape(B, nc, chunk_len, *t.shape[2:]), 0, 1)

    state, hc = lax.scan(lambda c, xs: mlstm_chunk(c, *xs), state,
                         tuple(chunks(t) for t in (q, k, v, i_pre, logf)))
    h_t = jnp.swapaxes(hc, 0, 1).reshape(B, S, vd) * o
    hh = h_t.reshape(B, S, B_HEADS, B_V_DIM)
    hh = hh * lax.rsqrt(jnp.mean(hh * hh, axis=-1, keepdims=True) + EPS)
    hn = hh.reshape(B, S, vd) * g_norm.astype(jnp.float32)
    return hn.astype(h.dtype) @ w_out, state


def swiglu(h, w_gate, w_up, w_down):
    return (jax.nn.silu(h @ w_gate) * (h @ w_up)) @ w_down


def setup_inputs(seed: int = 0) -> dict:
    key = jax.random.key(seed)
    ks = iter(jax.random.split(key, 32))

    def nrm(shape, scale):
        return jax.random.normal(next(ks), shape, jnp.float32) * scale

    na, nbl = N_A_LAYERS, N_B_LAYERS
    inp = {}
    inp["x_prompt"] = nrm((BATCH, SEQ, D_MODEL), 1.0)
    inp["x_sample"] = nrm((DEC_BATCH, DEC_SEQ, D_MODEL), 1.0)
    inp["cache_kv_w128"] = nrm((na, DEC_BATCH, min(128, PAST_LEN), 2, A_HEADS, A_HEAD_DIM), 1.0)
    inp["cache_kv_w512"] = nrm((na, DEC_BATCH, min(512, PAST_LEN), 2, A_HEADS, A_HEAD_DIM), 1.0)
    inp["cache_kv_w2048"] = nrm((na, DEC_BATCH, min(2048, PAST_LEN), 2, A_HEADS, A_HEAD_DIM), 1.0)
    inp["state_C"] = nrm((nbl, DEC_BATCH, B_HEADS, B_QK_DIM, B_V_DIM), 0.3)
    inp["state_n"] = nrm((nbl, DEC_BATCH, B_HEADS, B_QK_DIM), 0.4)
    inp["state_m"] = nrm((nbl, DEC_BATCH, B_HEADS), 1.0)
    inp["norm_mix"] = 1.0 + nrm((DEPTH, D_MODEL), 0.05)
    inp["norm_ffn"] = 1.0 + nrm((DEPTH, D_MODEL), 0.05)
    inp["norm_final"] = 1.0 + nrm((D_MODEL,), 0.05)
    inp["rel_bias"] = nrm((NUM_BUCKETS, N_GROUPS * A_HEADS), 0.2)
    inp["w_a_in"] = nrm((na, D_MODEL, A_IN_WIDTH), D_MODEL ** -0.5)
    inp["w_a_out"] = nrm((na, A_OUT_WIDTH, D_MODEL), A_OUT_WIDTH ** -0.5)
    inp["w_b_in"] = nrm((nbl, D_MODEL, B_IN_WIDTH), D_MODEL ** -0.5)
    f_bias = jnp.linspace(3.0, 6.0, B_HEADS, dtype=jnp.float32)[None] + nrm((nbl, B_HEADS), 0.1)
    inp["b_b_gate"] = jnp.concatenate([nrm((nbl, B_HEADS), 0.1), f_bias], axis=-1)
    inp["b_b_norm"] = 1.0 + nrm((nbl, B_HEADS * B_V_DIM), 0.05)
    inp["w_b_out"] = nrm((nbl, B_HEADS * B_V_DIM, D_MODEL), (B_HEADS * B_V_DIM) ** -0.5)
    inp["w_ffn_gate"] = nrm((DEPTH, D_MODEL, D_FF), D_MODEL ** -0.5)
    inp["w_ffn_up"] = nrm((DEPTH, D_MODEL, D_FF), D_MODEL ** -0.5)
    inp["w_ffn_down"] = nrm((DEPTH, D_FF, D_MODEL), D_FF ** -0.5)
    return inp


def reference(x_prompt, x_sample, cache_kv_w128, cache_kv_w512, cache_kv_w2048,
              state_C, state_n, state_m, norm_mix, norm_ffn, norm_final, rel_bias,
              w_a_in, w_a_out, w_b_in, b_b_gate, b_b_norm, w_b_out,
              w_ffn_gate, w_ffn_up, w_ffn_down):
    caches = (cache_kv_w128, cache_kv_w512, cache_kv_w2048)
    biases = [group_bias(rel_bias, g) for g in range(N_GROUPS)]
    xp, xs = x_prompt, x_sample
    Bp, S, _ = x_prompt.shape
    kv_p = [[] for _ in range(N_GROUPS)]
    kv_s = [[] for _ in range(N_GROUPS)]
    Cp, n_p, mp, Cs, n_s, ms = [], [], [], [], [], []
    for layer in range(DEPTH):
        j = layer // 2
        hp = rmsnorm(xp, norm_mix[layer])
        hs = rmsnorm(xs, norm_mix[layer])
        if layer % 2 == 0:
            op, rows_p = mixer_a_prompt(hp, w_a_in[j], w_a_out[j], biases)
            os_, rows_s = mixer_a_sample(hs, [c[j] for c in caches], w_a_in[j], w_a_out[j], biases)
            for g in range(N_GROUPS):
                kv_p[g].append(rows_p[g])
                kv_s[g].append(rows_s[g])
        else:
            init = (jnp.zeros((Bp, B_HEADS, B_QK_DIM, B_V_DIM), jnp.float32),
                    jnp.zeros((Bp, B_HEADS, B_QK_DIM), jnp.float32),
                    jnp.zeros((Bp, B_HEADS), jnp.float32))
            past = (state_C[j].astype(jnp.float32), state_n[j].astype(jnp.float32),
                    state_m[j].astype(jnp.float32))
            op, sp = mixer_b(hp, init, w_b_in[j], b_b_gate[j], b_b_norm[j], w_b_out[j], min(B_CHUNK, S))
            os_, ss = mixer_b(hs, past, w_b_in[j], b_b_gate[j], b_b_norm[j], w_b_out[j], xs.shape[1])
            Cp.append(sp[0]); n_p.append(sp[1]); mp.append(sp[2])
            Cs.append(ss[0]); n_s.append(ss[1]); ms.append(ss[2])
        xp = xp + op
        xs = xs + os_
        xp = xp + swiglu(rmsnorm(xp, norm_ffn[layer]), w_ffn_gate[layer], w_ffn_up[layer], w_ffn_down[layer])
        xs = xs + swiglu(rmsnorm(xs, norm_ffn[layer]), w_ffn_gate[layer], w_ffn_up[layer], w_ffn_down[layer])
    y_prompt = rmsnorm(xp, norm_final)
    y_sample = rmsnorm(xs, norm_final)
    kvd = cache_kv_w128.dtype
    sd = state_C.dtype
    kv_w128_prompt = jnp.stack(kv_p[0]).astype(kvd)
    kv_w512_prompt = jnp.stack(kv_p[1]).astype(kvd)
    kv_w2048_prompt = jnp.stack(kv_p[2]).astype(kvd)
    kv_w128_sample = jnp.stack(kv_s[0]).astype(kvd)
    kv_w512_sample = jnp.stack(kv_s[1]).astype(kvd)
    kv_w2048_sample = jnp.stack(kv_s[2]).astype(kvd)
    state_C_prompt = jnp.stack(Cp).astype(sd)
    state_n_prompt = jnp.stack(n_p).astype(sd)
    state_m_prompt = jnp.stack(mp).astype(sd)
    state_C_sample = jnp.stack(Cs).astype(sd)
    state_n_sample = jnp.stack(n_s).astype(sd)
    state_m_sample = jnp.stack(ms).astype(sd)
    return (y_prompt, y_sample, kv_w128_prompt, kv_w512_prompt, kv_w2048_prompt,
            kv_w128_sample, kv_w512_sample, kv_w2048_sample,
            state_C_prompt, state_n_prompt, state_m_prompt,
            state_C_sample, state_n_sample, state_m_sample)
```

```python
import functools
import math

import jax
import jax.numpy as jnp
from jax import lax
from jax.experimental import pallas as pl
from jax.experimental.pallas import tpu as pltpu

F32 = jnp.float32
BF16 = jnp.bfloat16

EPS = 1e-6
NEG = -1e30

A_GROUPS = ((128, 1), (512, 4), (2048, 16))
N_GROUPS = len(A_GROUPS)
A_HEADS = 8
A_HEAD_DIM = 128
A_STEPS = 128
A_BLOCK = 128
NUM_BUCKETS = 32
MAX_DISTANCE = 2048
B_HEADS = 8
LANES = 128
V7X_VMEM_BYTES = 64 * 1024 * 1024
VMEM_LIMIT = 56 * 1024 * 1024
PROMPT_CHUNK = 256


def _cparams(sem):
    return pltpu.CompilerParams(dimension_semantics=sem, vmem_limit_bytes=VMEM_LIMIT)


def _pick(n, cands):
    for c in cands:
        if n % c == 0:
            return c
    return n


def _rmsnorm_kernel(x_ref, g_ref, o_ref):
    x = x_ref[...]
    y = x * lax.rsqrt(jnp.mean(x * x, axis=-1, keepdims=True) + EPS)
    o_ref[...] = (y * g_ref[...]).astype(o_ref.dtype)


def rmsnorm(x, g, out_dtype):
    m, d = x.shape
    tm = _pick(m, (512, 256, 128, 64))
    return pl.pallas_call(
        _rmsnorm_kernel,
        grid=(m // tm,),
        in_specs=[pl.BlockSpec((tm, d), lambda i: (i, 0)),
                  pl.BlockSpec((1, d), lambda i: (0, 0))],
        out_specs=pl.BlockSpec((tm, d), lambda i: (i, 0)),
        out_shape=jax.ShapeDtypeStruct((m, d), out_dtype),
        compiler_params=_cparams(("parallel",)),
        name="rmsnorm",
    )(x, g.reshape(1, d))


def _mm_kernel(x_ref, w_ref, o_ref):
    acc = jnp.dot(x_ref[...].astype(BF16), w_ref[...], preferred_element_type=F32)
    o_ref[...] = acc.astype(o_ref.dtype)


def _mm_res_kernel(x_ref, w_ref, r_ref, o_ref):
    acc = jnp.dot(x_ref[...].astype(BF16), w_ref[...], preferred_element_type=F32)
    o_ref[...] = (r_ref[...] + acc).astype(o_ref.dtype)


def matmul(x, w, res=None, out_dtype=F32):
    m, k = x.shape
    n = w.shape[1]
    tm = _pick(m, (512, 256, 128, 64))
    tn = _pick(n, (512, 256, 128))
    in_specs = [pl.BlockSpec((tm, k), lambda j, i: (i, 0)),
                pl.BlockSpec((k, tn), lambda j, i: (0, j))]
    args = [x, w]
    kern = _mm_kernel
    if res is not None:
        in_specs.append(pl.BlockSpec((tm, tn), lambda j, i: (i, j)))
        args.append(res)
        kern = _mm_res_kernel
    return pl.pallas_call(
        kern,
        grid=(n // tn, m // tm),
        in_specs=in_specs,
        out_specs=pl.BlockSpec((tm, tn), lambda j, i: (i, j)),
        out_shape=jax.ShapeDtypeStruct((m, n), out_dtype),
        compiler_params=_cparams(("parallel", "parallel")),
        name="matmul",
    )(*args)


def _ffn_up_kernel(x_ref, wg_ref, wu_ref, o_ref):
    x = x_ref[...]
    g = jnp.dot(x, wg_ref[...], preferred_element_type=F32)
    u = jnp.dot(x, wu_ref[...], preferred_element_type=F32)
    o_ref[...] = (g * jax.nn.sigmoid(g) * u).astype(o_ref.dtype)


def ffn_up(x, wg, wu):
    m, k = x.shape
    n = wg.shape[1]
    tm = _pick(m, (512, 256, 128, 64))
    tn = _pick(n, (512, 256, 128))
    return pl.pallas_call(
        _ffn_up_kernel,
        grid=(n // tn, m // tm),
        in_specs=[pl.BlockSpec((tm, k), lambda j, i: (i, 0)),
                  pl.BlockSpec((k, tn), lambda j, i: (0, j)),
                  pl.BlockSpec((k, tn), lambda j, i: (0, j))],
        out_specs=pl.BlockSpec((tm, tn), lambda j, i: (i, j)),
        out_shape=jax.ShapeDtypeStruct((m, n), BF16),
        compiler_params=_cparams(("parallel", "parallel")),
        name="ffn_up",
    )(x, wg, wu)


def _t5_bucket(dist):
    max_exact = NUM_BUCKETS // 2
    d = jnp.maximum(dist.astype(F32), 1.0)
    large = max_exact + jnp.log(d / max_exact) / math.log(MAX_DISTANCE / max_exact) * (NUM_BUCKETS - max_exact)
    large = jnp.minimum(large.astype(jnp.int32), NUM_BUCKETS - 1)
    return jnp.where(dist < max_exact, dist, large)


def _group_bias(rel_bias, g):
    dil = A_GROUPS[g][1]
    buckets = _t5_bucket(jnp.arange(A_STEPS + 1, dtype=jnp.int32) * dil)
    return rel_bias[buckets, g * A_HEADS:(g + 1) * A_HEADS].T.astype(F32)


def _step_table(bias, step):
    valid = (step >= 0) & (step <= A_STEPS)
    tab = bias[:, jnp.clip(step, 0, A_STEPS)]
    return jnp.where(valid[None], tab, NEG)


def _band_tables(biases):
    qi = jnp.arange(A_BLOCK)[:, None] + A_BLOCK
    ki = jnp.arange(2 * A_BLOCK)[None, :]
    return jnp.stack([_step_table(b, qi - ki) for b in biases], axis=1)


def _sample_tables(biases, n_pasts, t_new):
    out = []
    t = jnp.arange(t_new)[:, None]
    for g, (_, dil) in enumerate(A_GROUPS):
        n_past = n_pasts[g]
        rows = jnp.arange(n_past + t_new)[None, :]
        dist = n_past + t - rows
        step = jnp.where(dist % dil == 0, dist // dil, -1)
        tab = _step_table(biases[g], step)
        out.append((tab[:, :, :n_past], tab[:, :, n_past:]))
    return out


def _rows(start, size, stride):
    return pl.ds(start, size) if stride == 1 else pl.ds(start, size, stride=stride)


def _masked_logits(q, k, tab, scale):
    s = lax.dot_general(q.astype(BF16), k.astype(BF16), (((1,), (1,)), ((), ())),
                        preferred_element_type=F32) * scale
    return jnp.where(tab > 0.5 * NEG, s + tab, NEG)


def _attn_prompt_kernel(*refs, seq):
    qkv = refs[:3 * N_GROUPS]
    band_ref, y_ref, o_scr, m_scr, l_scr = refs[3 * N_GROUPS:]
    scale = A_HEAD_DIM ** -0.5
    q_blk = A_BLOCK

    def block(g, dil, q_start, k_start, n_keys, tab):
        q_ref, k_ref, v_ref = qkv[3 * g:3 * g + 3]
        q = q_ref[_rows(q_start, q_blk, dil), :]
        k = k_ref[_rows(k_start, n_keys, dil), :]
        v = v_ref[_rows(k_start, n_keys, dil), :]
        s = _masked_logits(q, k, tab, scale)
        m = jnp.max(s, axis=-1, keepdims=True)
        p = jnp.exp(s - m)
        l = jnp.sum(p, axis=-1, keepdims=True)
        o = jnp.dot(p.astype(BF16), v.astype(BF16), preferred_element_type=F32)
        rows = _rows(q_start, q_blk, dil)
        o_scr[g, rows, :] = o
        m_scr[g, rows, :] = jnp.broadcast_to(m, (q_blk, LANES))
        l_scr[g, rows, :] = jnp.broadcast_to(l, (q_blk, LANES))

    for g, (_, dil) in enumerate(A_GROUPS):
        sub_len = seq // dil
        nb = sub_len // q_blk

        def residue(r, carry, g=g, dil=dil, nb=nb):
            block(g, dil, r, r, q_blk, band_ref[g, :, q_blk:])

            def later(n, c):
                start = r + dil * q_blk * (n - 1)
                block(g, dil, start + dil * q_blk, start, 2 * q_blk, band_ref[g])
                return c

            if nb > 1:
                lax.fori_loop(1, nb, later, 0)
            return carry

        if dil == 1:
            residue(0, 0)
        else:
            lax.fori_loop(0, dil, residue, 0)

    chunk = 256

    def merge(c, carry):
        rows = pl.ds(pl.multiple_of(c * chunk, chunk), chunk)
        ms = [m_scr[g, rows, :] for g in range(N_GROUPS)]
        m_all = functools.reduce(jnp.maximum, ms)
        num = jnp.zeros((chunk, LANES), F32)
        den = jnp.zeros((chunk, LANES), F32)
        for g in range(N_GROUPS):
            e = jnp.exp(ms[g] - m_all)
            num = num + e * o_scr[g, rows, :]
            den = den + e * l_scr[g, rows, :]
        y_ref[rows, :] = (num / den).astype(y_ref.dtype)
        return carry

    lax.fori_loop(0, seq // chunk, merge, 0)


def attention_prompt(qkv, band, batch, seq):
    e = A_HEAD_DIM
    in_specs = []
    for g in range(N_GROUPS):
        for t in range(3):
            col = (g * 3 + t) * A_HEADS
            in_specs.append(pl.BlockSpec((seq, e), lambda b, h, col=col: (b, col + h)))
    in_specs.append(pl.BlockSpec((None, N_GROUPS, A_BLOCK, 2 * A_BLOCK), lambda b, h: (h, 0, 0, 0)))
    return pl.pallas_call(
        functools.partial(_attn_prompt_kernel, seq=seq),
        grid=(batch, A_HEADS),
        in_specs=in_specs,
        out_specs=pl.BlockSpec((seq, e), lambda b, h: (b, h)),
        out_shape=jax.ShapeDtypeStruct((batch * seq, A_HEADS * e), BF16),
        scratch_shapes=[pltpu.VMEM((N_GROUPS, seq, e), F32),
                        pltpu.VMEM((N_GROUPS, seq, LANES), F32),
                        pltpu.VMEM((N_GROUPS, seq, LANES), F32)],
        compiler_params=_cparams(("parallel", "parallel")),
        name="attention_prompt",
    )(*([qkv] * (3 * N_GROUPS)), band)


def _attn_sample_kernel(*refs):
    qkv = refs[:3 * N_GROUPS]
    caches = refs[3 * N_GROUPS:5 * N_GROUPS]
    tables = refs[5 * N_GROUPS:7 * N_GROUPS]
    y_ref = refs[7 * N_GROUPS]
    scale = A_HEAD_DIM ** -0.5
    ms, ls, os_ = [], [], []
    for g in range(N_GROUPS):
        q = qkv[3 * g][...]
        k_new, v_new = qkv[3 * g + 1][...], qkv[3 * g + 2][...]
        k_old, v_old = caches[2 * g][...], caches[2 * g + 1][...]
        s_old = _masked_logits(q, k_old, tables[2 * g][...], scale)
        s_new = _masked_logits(q, k_new, tables[2 * g + 1][...], scale)
        m = jnp.maximum(jnp.max(s_old, axis=-1, keepdims=True), jnp.max(s_new, axis=-1, keepdims=True))
        p_old = jnp.exp(s_old - m)
        p_new = jnp.exp(s_new - m)
        ls.append(jnp.sum(p_old, axis=-1, keepdims=True) + jnp.sum(p_new, axis=-1, keepdims=True))
        os_.append(jnp.dot(p_old.astype(BF16), v_old.astype(BF16), preferred_element_type=F32)
                   + jnp.dot(p_new.astype(BF16), v_new.astype(BF16), preferred_element_type=F32))
        ms.append(m)
    m_all = functools.reduce(jnp.maximum, ms)
    num = 0.0
    den = 0.0
    for g in range(N_GROUPS):
        e = jnp.exp(ms[g] - m_all)
        num = num + e * os_[g]
        den = den + e * ls[g]
    y_ref[...] = (num / den).astype(y_ref.dtype)


def attention_sample(qkv, caches, tables, batch, t_new):
    e = A_HEAD_DIM
    in_specs, args = [], []
    for g in range(N_GROUPS):
        for t in range(3):
            col = (g * 3 + t) * A_HEADS
            in_specs.append(pl.BlockSpec((t_new, e), lambda b, h, col=col: (b, col + h)))
            args.append(qkv)
    for g in range(N_GROUPS):
        n_past = caches[g].shape[1]
        for t in range(2):
            in_specs.append(pl.BlockSpec((None, n_past, e), lambda b, h, t=t: (b, 0, t * A_HEADS + h)))
            args.append(caches[g])
    for g in range(N_GROUPS):
        for tab in tables[g]:
            in_specs.append(pl.BlockSpec((None,) + tab.shape[1:], lambda b, h: (h, 0, 0)))
            args.append(tab)
    return pl.pallas_call(
        _attn_sample_kernel,
        grid=(batch, A_HEADS),
        in_specs=in_specs,
        out_specs=pl.BlockSpec((t_new, e), lambda b, h: (b, h)),
        out_shape=jax.ShapeDtypeStruct((batch * t_new, A_HEADS * e), F32),
        compiler_params=_cparams(("parallel", "parallel")),
        name="attention_sample",
    )(*args)


def _log_sigmoid(x):
    return jnp.minimum(x, 0.0) - jnp.log1p(jnp.exp(-jnp.abs(x)))


def _mlstm_kernel(q_ref, k_ref, v_ref, og_ref, gc_ref, gr_ref, bc_ref, br_ref, gn_ref,
                  c0_ref, n0_ref, m0_ref,
                  h_ref, c_out, n_out, m_out,
                  c_scr, n_scr, m_scr, *, chunk, dk):
    hd = pl.program_id(1)
    c = pl.program_id(2)

    @pl.when(c == 0)
    def _():
        c_scr[...] = c0_ref[...].astype(F32)
        n_scr[...] = n0_ref[...].astype(F32)
        m_scr[...] = m0_ref[...].astype(F32)

    nh = B_HEADS
    gc = gc_ref[...] + bc_ref[...]
    gr = gr_ref[...] + br_ref[...]
    lane = lax.broadcasted_iota(jnp.int32, gc.shape, 1)
    sub = lax.broadcasted_iota(jnp.int32, gr.shape, 0)
    i_col = jnp.sum(jnp.where(lane == hd, gc, 0.0), axis=1, keepdims=True)
    f_col = jnp.sum(jnp.where(lane == hd + nh, gc, 0.0), axis=1, keepdims=True)
    i_row = jnp.sum(jnp.where(sub == hd, gr, 0.0), axis=0, keepdims=True)
    f_row = jnp.sum(jnp.where(sub == hd + nh, gr, 0.0), axis=0, keepdims=True)
    logf_col = _log_sigmoid(f_col)
    logf_row = _log_sigmoid(f_row)

    qi = lax.broadcasted_iota(jnp.int32, (chunk, chunk), 0)
    si = lax.broadcasted_iota(jnp.int32, (chunk, chunk), 1)
    causal = si <= qi
    b_col = jnp.sum(jnp.where(causal, logf_row, 0.0), axis=1, keepdims=True)
    b_row = jnp.sum(jnp.where(qi <= si, logf_col, 0.0), axis=0, keepdims=True)
    b_last = jnp.sum(logf_row, axis=1, keepdims=True)

    m_prev = m_scr[...]
    a = b_col + m_prev
    dmat = jnp.where(causal, b_col - b_row + i_row, NEG)
    m_t = jnp.maximum(a, jnp.max(dmat, axis=1, keepdims=True))
    w_inter = jnp.exp(a - m_t)

    q = q_ref[...].astype(BF16)
    kf = k_ref[...].astype(F32) * (dk ** -0.5)
    k = kf.astype(BF16)
    v = v_ref[...].astype(BF16)
    c_prev = c_scr[...]
    n_prev = n_scr[...]
    s = lax.dot_general(q, k, (((1,), (1,)), ((), ())), preferred_element_type=F32) * jnp.exp(dmat - m_t)
    num = (w_inter * jnp.dot(q, c_prev.astype(BF16), preferred_element_type=F32)
           + jnp.dot(s.astype(BF16), v, preferred_element_type=F32))
    qn = jnp.sum(q_ref[...].astype(F32) * n_prev, axis=1, keepdims=True)
    den = w_inter * qn + jnp.sum(s, axis=1, keepdims=True)
    h = num / jnp.maximum(jnp.abs(den), jnp.exp(-m_t))

    m_new = jnp.maximum(b_last + m_prev, jnp.max(b_last - b_row + i_row, axis=1, keepdims=True))
    w_state = jnp.exp(b_last - b_col + i_col - m_new)
    decay = jnp.exp(b_last + m_prev - m_new)
    kw = (w_state * kf).astype(BF16)
    c_scr[...] = decay * c_prev + lax.dot_general(kw, v, (((0,), (0,)), ((), ())),
                                                  preferred_element_type=F32)
    n_scr[...] = decay * n_prev + jnp.sum(w_state * kf, axis=0, keepdims=True)
    m_scr[...] = m_new

    ht = h * jax.nn.sigmoid(og_ref[...].astype(F32))
    hn = ht * lax.rsqrt(jnp.mean(ht * ht, axis=-1, keepdims=True) + EPS)
    h_ref[...] = (hn * gn_ref[...]).astype(h_ref.dtype)

    @pl.when(c == pl.num_programs(2) - 1)
    def _():
        c_out[...] = c_scr[...].astype(c_out.dtype)
        n_out[...] = n_scr[...].astype(n_out.dtype)
        m_out[...] = m_scr[...].astype(m_out.dtype)


def mlstm(z, gates, b_gate, g_norm, state, batch, seq, chunk, out_dtype):
    c0, n0, m0 = state
    nh = B_HEADS
    dk, dv = c0.shape[-2], c0.shape[-1]
    nc = seq // chunk
    r = dv // dk
    gcol = gates.reshape(batch * nc, chunk, 2 * nh)
    grow = jnp.swapaxes(gcol, 1, 2)
    row = lambda b, h, c: b * nc + c
    in_specs = [
        pl.BlockSpec((chunk, dk), lambda b, h, c: (row(b, h, c), h)),
        pl.BlockSpec((chunk, dk), lambda b, h, c: (row(b, h, c), nh + h)),
        pl.BlockSpec((chunk, dv), lambda b, h, c: (row(b, h, c), (2 * nh) // r + h)),
        pl.BlockSpec((chunk, dv), lambda b, h, c: (row(b, h, c), (2 * nh) // r + nh + h)),
        pl.BlockSpec((None, chunk, 2 * nh), lambda b, h, c: (row(b, h, c), 0, 0)),
        pl.BlockSpec((None, 2 * nh, chunk), lambda b, h, c: (row(b, h, c), 0, 0)),
        pl.BlockSpec((1, 2 * nh), lambda b, h, c: (0, 0)),
        pl.BlockSpec((2 * nh, 1), lambda b, h, c: (0, 0)),
        pl.BlockSpec((1, dv), lambda b, h, c: (0, h)),
        pl.BlockSpec((None, None, dk, dv), lambda b, h, c: (b, h, 0, 0)),
        pl.BlockSpec((None, None, 1, dk), lambda b, h, c: (b, h, 0, 0)),
        pl.BlockSpec((None, None, 1, 1), lambda b, h, c: (b, h, 0, 0)),
    ]
    out_specs = [
        pl.BlockSpec((chunk, dv), lambda b, h, c: (row(b, h, c), h)),
        pl.BlockSpec((None, None, dk, dv), lambda b, h, c: (b, h, 0, 0)),
        pl.BlockSpec((None, None, 1, dk), lambda b, h, c: (b, h, 0, 0)),
        pl.BlockSpec((None, None, 1, 1), lambda b, h, c: (b, h, 0, 0)),
    ]
    out_shape = [
        jax.ShapeDtypeStruct((batch * seq, nh * dv), out_dtype),
        jax.ShapeDtypeStruct((batch, nh, dk, dv), F32),
        jax.ShapeDtypeStruct((batch, nh, 1, dk), F32),
        jax.ShapeDtypeStruct((batch, nh, 1, 1), F32),
    ]
    bg = b_gate.astype(F32)
    hn, c_new, n_new, m_new = pl.pallas_call(
        functools.partial(_mlstm_kernel, chunk=chunk, dk=dk),
        grid=(batch, nh, nc),
        in_specs=in_specs,
        out_specs=out_specs,
        out_shape=out_shape,
        scratch_shapes=[pltpu.VMEM((dk, dv), F32), pltpu.VMEM((1, dk), F32), pltpu.VMEM((1, 1), F32)],
        compiler_params=_cparams(("parallel", "parallel", "arbitrary")),
        name="mlstm",
    )(z, z, z, z, gcol, grow, bg.reshape(1, 2 * nh), bg.reshape(2 * nh, 1),
      g_norm.astype(F32).reshape(1, nh * dv),
      c0.reshape(batch, nh, dk, dv), n0.reshape(batch, nh, 1, dk), m0.reshape(batch, nh, 1, 1))
    return hn, (c_new, n_new.reshape(batch, nh, dk), m_new.reshape(batch, nh))


def kernel(x_prompt, x_sample, cache_kv_w128, cache_kv_w512, cache_kv_w2048, state_C, state_n, state_m,
           norm_mix, norm_ffn, norm_final, rel_bias, w_a_in, w_a_out, w_b_in, b_b_gate, b_b_norm, w_b_out,
           w_ffn_gate, w_ffn_up, w_ffn_down):
    bp, seq, d_model = x_prompt.shape
    bs, t_new, _ = x_sample.shape
    depth = norm_mix.shape[0]
    caches = (cache_kv_w128, cache_kv_w512, cache_kv_w2048)
    nh, dk, dv = state_C.shape[2], state_C.shape[3], state_C.shape[4]
    qk_w, v_w = nh * dk, nh * dv
    main_w = 2 * qk_w + 2 * v_w
    he = A_HEADS * A_HEAD_DIM

    biases = [_group_bias(rel_bias, g) for g in range(N_GROUPS)]
    band = _band_tables(biases)
    n_pasts = [c.shape[2] for c in caches]
    tables = _sample_tables(biases, n_pasts, t_new)

    xp = x_prompt.reshape(bp * seq, d_model)
    xs = x_sample.reshape(bs * t_new, d_model)
    kv_p = [[] for _ in range(N_GROUPS)]
    kv_s = [[] for _ in range(N_GROUPS)]
    st_p, st_s = [], []
    for layer in range(depth):
        j = layer // 2
        hp = rmsnorm(xp, norm_mix[layer], BF16)
        hs = rmsnorm(xs, norm_mix[layer], BF16)
        if layer % 2 == 0:
            w_in = w_a_in[j].astype(BF16)
            w_out = w_a_out[j].astype(BF16)
            qkv_p = matmul(hp, w_in)
            qkv_s = matmul(hs, w_in)
            yp = attention_prompt(qkv_p, band, bp, seq)
            ys = attention_sample(qkv_s, [c[j].reshape(bs, c.shape[2], 2 * he) for c in caches],
                                  tables, bs, t_new)
            xp = matmul(yp, w_out, res=xp)
            xs = matmul(ys, w_out, res=xs)
            q6p = qkv_p.reshape(bp, seq, N_GROUPS, 3, A_HEADS, A_HEAD_DIM)
            q6s = qkv_s.reshape(bs, t_new, N_GROUPS, 3, A_HEADS, A_HEAD_DIM)
            for g, (window, _) in enumerate(A_GROUPS):
                kv_p[g].append(q6p[:, seq - min(window, seq):, g, 1:])
                kv_s[g].append(q6s[:, :, g, 1:])
        else:
            w_in = w_b_in[j][:, :main_w].astype(BF16)
            w_gate = jnp.pad(w_b_in[j][:, main_w:], ((0, 0), (0, LANES - 2 * nh))).astype(BF16)
            w_out = w_b_out[j].astype(BF16)
            zero = (jnp.zeros((bp, nh, dk, dv), F32), jnp.zeros((bp, nh, dk), F32), jnp.zeros((bp, nh), F32))
            past = (state_C[j], state_n[j], state_m[j])
            zp = matmul(hp, w_in)
            zs = matmul(hs, w_in)
            gp = matmul(hp, w_gate)[:, :2 * nh]
            gs = matmul(hs, w_gate)[:, :2 * nh]
            hnp, sp = mlstm(zp, gp, b_b_gate[j], b_b_norm[j], zero, bp, seq, min(PROMPT_CHUNK, seq), BF16)
            hns, ss = mlstm(zs, gs, b_b_gate[j], b_b_norm[j], past, bs, t_new, t_new, F32)
            xp = matmul(hnp, w_out, res=xp)
            xs = matmul(hns, w_out, res=xs)
            st_p.append(sp)
            st_s.append(ss)
        wg = w_ffn_gate[layer].astype(BF16)
        wu = w_ffn_up[layer].astype(BF16)
        wd = w_ffn_down[layer].astype(BF16)
        xp = matmul(ffn_up(rmsnorm(xp, norm_ffn[layer], BF16), wg, wu), wd, res=xp)
        xs = matmul(ffn_up(rmsnorm(xs, norm_ffn[layer], BF16), wg, wu), wd, res=xs)

    y_prompt = rmsnorm(xp, norm_final, F32).reshape(bp, seq, d_model)
    y_sample = rmsnorm(xs, norm_final, F32).reshape(bs, t_new, d_model)
    kvd = cache_kv_w128.dtype
    sd = state_C.dtype
    kv_out_p = [jnp.stack(kv_p[g]).astype(kvd) for g in range(N_GROUPS)]
    kv_out_s = [jnp.stack(kv_s[g]).astype(kvd) for g in range(N_GROUPS)]
    states_p = [jnp.stack([s[i] for s in st_p]).astype(sd) for i in range(3)]
    states_s = [jnp.stack([s[i] for s in st_s]).astype(sd) for i in range(3)]
    return (y_prompt, y_sample, *kv_out_p, *kv_out_s, *states_p, *states_s)
```

```python
import functools
import math

import jax
import jax.numpy as jnp
from jax import lax
from jax.experimental import pallas as pl
from jax.experimental.pallas import tpu as pltpu

F32 = jnp.float32
BF16 = jnp.bfloat16

EPS = 1e-6
NEG = -1e30

A_GROUPS = ((128, 1), (512, 4), (2048, 16))
N_GROUPS = len(A_GROUPS)
A_HEADS = 8
A_HEAD_DIM = 128
A_STEPS = 128
A_BLOCK = 128
NUM_BUCKETS = 32
MAX_DISTANCE = 2048
B_HEADS = 8
LANES = 128
VMEM_LIMIT = 56 * 1024 * 1024
PROMPT_CHUNK = 256


def _cparams(sem):
    return pltpu.CompilerParams(dimension_semantics=sem, vmem_limit_bytes=VMEM_LIMIT)


def _pick(n, cands):
    for c in cands:
        if n % c == 0:
            return c
    return n


def _rmsnorm_kernel(x_ref, g_ref, o_ref):
    x = x_ref[...]
    y = x * lax.rsqrt(jnp.mean(x * x, axis=-1, keepdims=True) + EPS)
    o_ref[...] = (y * g_ref[...]).astype(o_ref.dtype)


def rmsnorm(x, g, out_dtype):
    m, d = x.shape
    tm = _pick(m, (512, 256, 128, 64))
    return pl.pallas_call(
        _rmsnorm_kernel,
        grid=(m // tm,),
        in_specs=[pl.BlockSpec((tm, d), lambda i: (i, 0)),
                  pl.BlockSpec((1, d), lambda i: (0, 0))],
        out_specs=pl.BlockSpec((tm, d), lambda i: (i, 0)),
        out_shape=jax.ShapeDtypeStruct((m, d), out_dtype),
        compiler_params=_cparams(("parallel",)),
        name="rmsnorm",
    )(x, g.reshape(1, d))


def _linear_kernel(*refs, n_w, has_res, swiglu, m_axis):
    x_ref = refs[0]
    w_refs = refs[1:1 + n_w]
    r_ref = refs[1 + n_w] if has_res else None
    o_ref = refs[1 + n_w + has_res]
    wb_refs = refs[2 + n_w + has_res:]

    @pl.when(pl.program_id(m_axis) == 0)
    def _():
        for w_ref, wb_ref in zip(w_refs, wb_refs):
            wb_ref[...] = w_ref[...].astype(BF16)

    x = x_ref[...].astype(BF16)
    accs = [jnp.dot(x, wb_ref[...], preferred_element_type=F32) for wb_ref in wb_refs]
    if swiglu:
        out = accs[0] * jax.nn.sigmoid(accs[0]) * accs[1]
    else:
        out = accs[0]
    if has_res:
        out = r_ref[...] + out
    o_ref[...] = out.astype(o_ref.dtype)


def linear(x, ws, w_index, outer, out_shape, out_block, out_index, tn, *, res=None, swiglu=False,
           out_dtype=F32, name="linear"):
    m, k = x.shape
    tm = _pick(m, (512, 256, 128, 64))
    nd = len(outer)
    in_specs = [pl.BlockSpec((tm, k), lambda *g: (g[nd], 0))]
    in_specs += [pl.BlockSpec((None, k, tn), lambda *g: w_index(*g[:nd])) for _ in ws]
    args = [x, *ws]
    if res is not None:
        in_specs.append(pl.BlockSpec(out_block(tm), lambda *g: out_index(*g)))
        args.append(res)
    return pl.pallas_call(
        functools.partial(_linear_kernel, n_w=len(ws), has_res=res is not None, swiglu=swiglu, m_axis=nd),
        grid=(*outer, m // tm),
        in_specs=in_specs,
        out_specs=pl.BlockSpec(out_block(tm), lambda *g: out_index(*g)),
        out_shape=jax.ShapeDtypeStruct(out_shape, out_dtype),
        scratch_shapes=[pltpu.VMEM((k, tn), BF16) for _ in ws],
        compiler_params=_cparams(("parallel",) * nd + ("arbitrary",)),
        name=name,
    )(*args)


def dense(x, w, layer, *, res=None, out_dtype=F32, n_cols=None, name="dense"):
    n = n_cols or w.shape[2]
    tn = _pick(n, (1024, 512, 256, 128)) if w.shape[1] <= 2048 else _pick(n, (512, 256, 128))
    return linear(x, [w], lambda j: (layer, 0, j), (n // tn,), (x.shape[0], n),
                  lambda tm: (tm, tn), lambda j, i: (i, j), tn, res=res, out_dtype=out_dtype, name=name)


def ffn_up(x, wg, wu, layer):
    n = wg.shape[2]
    tn = _pick(n, (512, 256, 128))
    return linear(x, [wg, wu], lambda j: (layer, 0, j), (n // tn,), (x.shape[0], n),
                  lambda tm: (tm, tn), lambda j, i: (i, j), tn, swiglu=True, out_dtype=BF16, name="ffn_up")


def attn_in_proj(x, w, layer):
    m = x.shape[0]
    he = A_HEADS * A_HEAD_DIM
    q = linear(x, [w], lambda g: (layer, 0, 3 * g), (N_GROUPS,), (m, N_GROUPS * he),
               lambda tm: (tm, he), lambda g, i: (i, g), he, name="attn_q_proj")
    kv = linear(x, [w], lambda g, t: (layer, 0, 3 * g + 1 + t), (N_GROUPS, 2), (N_GROUPS, m, 2 * he),
                lambda tm: (None, tm, he), lambda g, t, i: (g, i, t), he, name="attn_kv_proj")
    return q, kv


def _gate_kernel(x_ref, w_ref, o_ref):
    o_ref[...] = jnp.dot(x_ref[...], w_ref[...].astype(BF16), preferred_element_type=F32)


def gate_proj(x, w):
    m, k = x.shape
    n = w.shape[1]
    tm = _pick(m, (512, 256, 128, 64))
    return pl.pallas_call(
        _gate_kernel,
        grid=(m // tm,),
        in_specs=[pl.BlockSpec((tm, k), lambda i: (i, 0)), pl.BlockSpec((k, n), lambda i: (0, 0))],
        out_specs=pl.BlockSpec((tm, n), lambda i: (i, 0)),
        out_shape=jax.ShapeDtypeStruct((m, n), F32),
        compiler_params=_cparams(("parallel",)),
        name="gate_proj",
    )(x, w)


def _t5_bucket(dist):
    max_exact = NUM_BUCKETS // 2
    d = jnp.maximum(dist.astype(F32), 1.0)
    large = max_exact + jnp.log(d / max_exact) / math.log(MAX_DISTANCE / max_exact) * (NUM_BUCKETS - max_exact)
    large = jnp.minimum(large.astype(jnp.int32), NUM_BUCKETS - 1)
    return jnp.where(dist < max_exact, dist, large)


def _group_bias(rel_bias, g):
    dil = A_GROUPS[g][1]
    buckets = _t5_bucket(jnp.arange(A_STEPS + 1, dtype=jnp.int32) * dil)
    onehot = (buckets[:, None] == jnp.arange(NUM_BUCKETS)[None, :]).astype(F32)
    cols = rel_bias[:, g * A_HEADS:(g + 1) * A_HEADS].astype(F32)
    return jnp.einsum("kb,bh->hk", onehot, cols, precision=lax.Precision.HIGHEST)


def _toeplitz(f, lo, n_rows, n_cols, offset):
    first, last = offset - n_cols + 1, offset + n_rows - 1
    pad_lo, pad_hi = max(lo - first, 0), max(last - (lo + f.shape[-1] - 1), 0)
    fe = jnp.pad(f, [(0, 0)] * (f.ndim - 1) + [(pad_lo, pad_hi)], constant_values=NEG)
    base = lo - pad_lo
    ext = lambda a, b: fe[..., a - base:b - base]
    period = n_rows + n_cols - 1
    w = jnp.concatenate([ext(first, offset + 1)[..., ::-1], ext(offset + 1, last + 1)[..., ::-1]], axis=-1)
    flat = jnp.tile(w, n_rows)[..., :n_rows * (period - 1)]
    return flat.reshape(*f.shape[:-1], n_rows, period - 1)[..., :n_cols]


def _band_tables(biases):
    return jnp.stack([_toeplitz(b, 0, A_BLOCK, 2 * A_BLOCK, A_BLOCK) for b in biases], axis=1)


def _sample_tables(biases, n_pasts, t_new):
    out = []
    for g, (_, dil) in enumerate(A_GROUPS):
        n_past = n_pasts[g]
        b = biases[g]
        by_dist = jnp.concatenate([b[..., None], jnp.full(b.shape + (dil - 1,), NEG, F32)], axis=-1)
        by_dist = by_dist.reshape(b.shape[0], -1)[:, :A_STEPS * dil + 1]
        tab = _toeplitz(by_dist, 0, t_new, n_past + t_new, n_past)
        out.append((tab[:, :, :n_past], tab[:, :, n_past:]))
    return out


def _rows(start, size, stride):
    return pl.ds(start, size) if stride == 1 else pl.ds(start, size, stride=stride)


def _masked_logits(q, k, tab, scale):
    s = lax.dot_general(q.astype(BF16), k.astype(BF16), (((1,), (1,)), ((), ())),
                        preferred_element_type=F32) * scale
    return jnp.where(tab > 0.5 * NEG, s + tab, NEG)


def _attn_prompt_kernel(*refs, seq):
    qkv = refs[:3 * N_GROUPS]
    band_ref, y_ref, o_scr, m_scr, l_scr = refs[3 * N_GROUPS:]
    scale = A_HEAD_DIM ** -0.5
    q_blk = A_BLOCK

    def block(g, dil, q_start, k_start, n_keys, tab):
        q_ref, k_ref, v_ref = qkv[3 * g:3 * g + 3]
        q = q_ref[_rows(q_start, q_blk, dil), :]
        k = k_ref[_rows(k_start, n_keys, dil), :]
        v = v_ref[_rows(k_start, n_keys, dil), :]
        s = _masked_logits(q, k, tab, scale)
        m = jnp.max(s, axis=-1, keepdims=True)
        p = jnp.exp(s - m)
        l = jnp.sum(p, axis=-1, keepdims=True)
        o = jnp.dot(p.astype(BF16), v.astype(BF16), preferred_element_type=F32)
        rows = _rows(q_start, q_blk, dil)
        o_scr[g, rows, :] = o
        m_scr[g, rows, :] = jnp.broadcast_to(m, (q_blk, LANES))
        l_scr[g, rows, :] = jnp.broadcast_to(l, (q_blk, LANES))

    for g, (_, dil) in enumerate(A_GROUPS):
        sub_len = seq // dil
        nb = sub_len // q_blk

        def residue(r, carry, g=g, dil=dil, nb=nb):
            block(g, dil, r, r, q_blk, band_ref[g, :, q_blk:])

            def later(n, c):
                start = r + dil * q_blk * (n - 1)
                block(g, dil, start + dil * q_blk, start, 2 * q_blk, band_ref[g])
                return c

            if nb > 1:
                lax.fori_loop(1, nb, later, 0)
            return carry

        if dil == 1:
            residue(0, 0)
        else:
            lax.fori_loop(0, dil, residue, 0)

    chunk = 256

    def merge(c, carry):
        rows = pl.ds(pl.multiple_of(c * chunk, chunk), chunk)
        ms = [m_scr[g, rows, :] for g in range(N_GROUPS)]
        m_all = functools.reduce(jnp.maximum, ms)
        num = jnp.zeros((chunk, LANES), F32)
        den = jnp.zeros((chunk, LANES), F32)
        for g in range(N_GROUPS):
            e = jnp.exp(ms[g] - m_all)
            num = num + e * o_scr[g, rows, :]
            den = den + e * l_scr[g, rows, :]
        y_ref[rows, :] = (num / den).astype(y_ref.dtype)
        return carry

    lax.fori_loop(0, seq // chunk, merge, 0)


def attention_prompt(q, kv, band, batch, seq):
    e = A_HEAD_DIM
    assert all(seq % (dil * A_BLOCK) == 0 for _, dil in A_GROUPS)
    in_specs, args = [], []
    for g in range(N_GROUPS):
        in_specs.append(pl.BlockSpec((seq, e), lambda b, h, g=g: (b, g * A_HEADS + h)))
        in_specs.append(pl.BlockSpec((None, seq, e), lambda b, h, g=g: (g, b, h)))
        in_specs.append(pl.BlockSpec((None, seq, e), lambda b, h, g=g: (g, b, A_HEADS + h)))
        args += [q, kv, kv]
    in_specs.append(pl.BlockSpec((None, N_GROUPS, A_BLOCK, 2 * A_BLOCK), lambda b, h: (h, 0, 0, 0)))
    return pl.pallas_call(
        functools.partial(_attn_prompt_kernel, seq=seq),
        grid=(batch, A_HEADS),
        in_specs=in_specs,
        out_specs=pl.BlockSpec((seq, e), lambda b, h: (b, h)),
        out_shape=jax.ShapeDtypeStruct((batch * seq, A_HEADS * e), BF16),
        scratch_shapes=[pltpu.VMEM((N_GROUPS, seq, e), F32),
                        pltpu.VMEM((N_GROUPS, seq, LANES), F32),
                        pltpu.VMEM((N_GROUPS, seq, LANES), F32)],
        compiler_params=_cparams(("parallel", "parallel")),
        name="attention_prompt",
    )(*args, band)


def _attn_sample_kernel(*refs):
    qkv = refs[:3 * N_GROUPS]
    caches = refs[3 * N_GROUPS:5 * N_GROUPS]
    tables = refs[5 * N_GROUPS:7 * N_GROUPS]
    y_ref = refs[7 * N_GROUPS]
    scale = A_HEAD_DIM ** -0.5
    ms, ls, os_ = [], [], []
    for g in range(N_GROUPS):
        q = qkv[3 * g][...]
        k_new, v_new = qkv[3 * g + 1][...], qkv[3 * g + 2][...]
        k_old, v_old = caches[2 * g][...], caches[2 * g + 1][...]
        s_old = _masked_logits(q, k_old, tables[2 * g][...], scale)
        s_new = _masked_logits(q, k_new, tables[2 * g + 1][...], scale)
        m = jnp.maximum(jnp.max(s_old, axis=-1, keepdims=True), jnp.max(s_new, axis=-1, keepdims=True))
        p_old = jnp.exp(s_old - m)
        p_new = jnp.exp(s_new - m)
        ls.append(jnp.sum(p_old, axis=-1, keepdims=True) + jnp.sum(p_new, axis=-1, keepdims=True))
        os_.append(jnp.dot(p_old.astype(BF16), v_old.astype(BF16), preferred_element_type=F32)
                   + jnp.dot(p_new.astype(BF16), v_new.astype(BF16), preferred_element_type=F32))
        ms.append(m)
    m_all = functools.reduce(jnp.maximum, ms)
    num = 0.0
    den = 0.0
    for g in range(N_GROUPS):
        e = jnp.exp(ms[g] - m_all)
        num = num + e * os_[g]
        den = den + e * ls[g]
    y_ref[...] = (num / den).astype(y_ref.dtype)


def attention_sample(q, kv, caches, layer, tables, batch, t_new):
    e = A_HEAD_DIM
    in_specs, args = [], []
    for g in range(N_GROUPS):
        in_specs.append(pl.BlockSpec((t_new, e), lambda b, h, g=g: (b, g * A_HEADS + h)))
        in_specs.append(pl.BlockSpec((None, t_new, e), lambda b, h, g=g: (g, b, h)))
        in_specs.append(pl.BlockSpec((None, t_new, e), lambda b, h, g=g: (g, b, A_HEADS + h)))
        args += [q, kv, kv]
    for g in range(N_GROUPS):
        n_past = caches[g].shape[2]
        for t in range(2):
            in_specs.append(pl.BlockSpec((None, None, n_past, e),
                                         lambda b, h, t=t: (layer, b, 0, t * A_HEADS + h)))
            args.append(caches[g])
    for g in range(N_GROUPS):
        for tab in tables[g]:
            in_specs.append(pl.BlockSpec((None,) + tab.shape[1:], lambda b, h: (h, 0, 0)))
            args.append(tab)
    return pl.pallas_call(
        _attn_sample_kernel,
        grid=(batch, A_HEADS),
        in_specs=in_specs,
        out_specs=pl.BlockSpec((t_new, e), lambda b, h: (b, h)),
        out_shape=jax.ShapeDtypeStruct((batch * t_new, A_HEADS * e), F32),
        compiler_params=_cparams(("parallel", "parallel")),
        name="attention_sample",
    )(*args)


def _log_sigmoid(x):
    return jnp.minimum(x, 0.0) - jnp.log1p(jnp.exp(-jnp.abs(x)))


def _mlstm_kernel(q_ref, k_ref, v_ref, og_ref, gc_ref, gr_ref, bc_ref, br_ref, gn_ref,
                  c0_ref, n0_ref, m0_ref,
                  h_ref, c_out, n_out, m_out,
                  c_scr, n_scr, m_scr, *, chunk, dk):
    hd = pl.program_id(1)
    c = pl.program_id(2)

    @pl.when(c == 0)
    def _():
        c_scr[...] = c0_ref[...].astype(F32)
        n_scr[...] = n0_ref[...].astype(F32)
        m_scr[...] = m0_ref[...].astype(F32)

    nh = B_HEADS
    gc = gc_ref[...] + bc_ref[...]
    gr = gr_ref[...] + br_ref[...]
    lane = lax.broadcasted_iota(jnp.int32, gc.shape, 1)
    sub = lax.broadcasted_iota(jnp.int32, gr.shape, 0)
    i_col = jnp.sum(jnp.where(lane == hd, gc, 0.0), axis=1, keepdims=True)
    f_col = jnp.sum(jnp.where(lane == hd + nh, gc, 0.0), axis=1, keepdims=True)
    i_row = jnp.sum(jnp.where(sub == hd, gr, 0.0), axis=0, keepdims=True)
    f_row = jnp.sum(jnp.where(sub == hd + nh, gr, 0.0), axis=0, keepdims=True)
    logf_col = _log_sigmoid(f_col)
    logf_row = _log_sigmoid(f_row)

    qi = lax.broadcasted_iota(jnp.int32, (chunk, chunk), 0)
    si = lax.broadcasted_iota(jnp.int32, (chunk, chunk), 1)
    causal = si <= qi
    b_col = jnp.sum(jnp.where(causal, logf_row, 0.0), axis=1, keepdims=True)
    b_row = jnp.sum(jnp.where(qi <= si, logf_col, 0.0), axis=0, keepdims=True)
    b_last = jnp.sum(logf_row, axis=1, keepdims=True)

    m_prev = m_scr[...]
    a = b_col + m_prev
    dmat = jnp.where(causal, b_col - b_row + i_row, NEG)
    m_t = jnp.maximum(a, jnp.max(dmat, axis=1, keepdims=True))
    w_inter = jnp.exp(a - m_t)

    qf = q_ref[...].astype(F32)
    q = qf.astype(BF16)
    kf = k_ref[...].astype(F32) * (dk ** -0.5)
    k = kf.astype(BF16)
    v = v_ref[...].astype(BF16)
    c_prev = c_scr[...]
    n_prev = n_scr[...]
    s = lax.dot_general(q, k, (((1,), (1,)), ((), ())), preferred_element_type=F32) * jnp.exp(dmat - m_t)
    num = (w_inter * jnp.dot(q, c_prev.astype(BF16), preferred_element_type=F32)
           + jnp.dot(s.astype(BF16), v, preferred_element_type=F32))
    qn = jnp.sum(qf * n_prev, axis=1, keepdims=True)
    den = w_inter * qn + jnp.sum(s, axis=1, keepdims=True)
    h = num / jnp.maximum(jnp.abs(den), jnp.exp(-m_t))

    m_new = jnp.maximum(b_last + m_prev, jnp.max(b_last - b_row + i_row, axis=1, keepdims=True))
    w_state = jnp.exp(b_last - b_col + i_col - m_new)
    decay = jnp.exp(b_last + m_prev - m_new)
    kw = (w_state * kf).astype(BF16)
    c_scr[...] = decay * c_prev + lax.dot_general(kw, v, (((0,), (0,)), ((), ())),
                                                  preferred_element_type=F32)
    n_scr[...] = decay * n_prev + jnp.sum(w_state * kf, axis=0, keepdims=True)
    m_scr[...] = m_new

    ht = h * jax.nn.sigmoid(og_ref[...].astype(F32))
    hn = ht * lax.rsqrt(jnp.mean(ht * ht, axis=-1, keepdims=True) + EPS)
    h_ref[...] = (hn * gn_ref[...]).astype(h_ref.dtype)

    @pl.when(c == pl.num_programs(2) - 1)
    def _():
        c_out[...] = c_scr[...].astype(c_out.dtype)
        n_out[...] = n_scr[...].astype(n_out.dtype)
        m_out[...] = m_scr[...].astype(m_out.dtype)


def mlstm(z, gates, b_gate, g_norm, state, batch, seq, chunk, out_dtype):
    c0, n0, m0 = state
    nh = B_HEADS
    dk, dv = c0.shape[-2], c0.shape[-1]
    nc = seq // chunk
    r = dv // dk
    gcol = gates.reshape(batch * nc, chunk, 2 * nh)
    grow = jnp.swapaxes(gcol, 1, 2)
    row = lambda b, h, c: b * nc + c
    in_specs = [
        pl.BlockSpec((chunk, dk), lambda b, h, c: (row(b, h, c), h)),
        pl.BlockSpec((chunk, dk), lambda b, h, c: (row(b, h, c), nh + h)),
        pl.BlockSpec((chunk, dv), lambda b, h, c: (row(b, h, c), (2 * nh) // r + h)),
        pl.BlockSpec((chunk, dv), lambda b, h, c: (row(b, h, c), (2 * nh) // r + nh + h)),
        pl.BlockSpec((None, chunk, 2 * nh), lambda b, h, c: (row(b, h, c), 0, 0)),
        pl.BlockSpec((None, 2 * nh, chunk), lambda b, h, c: (row(b, h, c), 0, 0)),
        pl.BlockSpec((1, 2 * nh), lambda b, h, c: (0, 0)),
        pl.BlockSpec((2 * nh, 1), lambda b, h, c: (0, 0)),
        pl.BlockSpec((1, dv), lambda b, h, c: (0, h)),
        pl.BlockSpec((None, None, dk, dv), lambda b, h, c: (b, h, 0, 0)),
        pl.BlockSpec((None, None, 1, dk), lambda b, h, c: (b, h, 0, 0)),
        pl.BlockSpec((None, None, 1, 1), lambda b, h, c: (b, h, 0, 0)),
    ]
    out_specs = [
        pl.BlockSpec((chunk, dv), lambda b, h, c: (row(b, h, c), h)),
        pl.BlockSpec((None, None, dk, dv), lambda b, h, c: (b, h, 0, 0)),
        pl.BlockSpec((None, None, 1, dk), lambda b, h, c: (b, h, 0, 0)),
        pl.BlockSpec((None, None, 1, 1), lambda b, h, c: (b, h, 0, 0)),
    ]
    out_shape = [
        jax.ShapeDtypeStruct((batch * seq, nh * dv), out_dtype),
        jax.ShapeDtypeStruct((batch, nh, dk, dv), F32),
        jax.ShapeDtypeStruct((batch, nh, 1, dk), F32),
        jax.ShapeDtypeStruct((batch, nh, 1, 1), F32),
    ]
    bg = b_gate.astype(F32)
    hn, c_new, n_new, m_new = pl.pallas_call(
        functools.partial(_mlstm_kernel, chunk=chunk, dk=dk),
        grid=(batch, nh, nc),
        in_specs=in_specs,
        out_specs=out_specs,
        out_shape=out_shape,
        scratch_shapes=[pltpu.VMEM((dk, dv), F32), pltpu.VMEM((1, dk), F32), pltpu.VMEM((1, 1), F32)],
        compiler_params=_cparams(("parallel", "parallel", "arbitrary")),
        name="mlstm",
    )(z, z, z, z, gcol, grow, bg.reshape(1, 2 * nh), bg.reshape(2 * nh, 1),
      g_norm.astype(F32).reshape(1, nh * dv),
      c0.reshape(batch, nh, dk, dv), n0.reshape(batch, nh, 1, dk), m0.reshape(batch, nh, 1, 1))
    return hn, (c_new, n_new.reshape(batch, nh, dk), m_new.reshape(batch, nh))


def kernel(x_prompt, x_sample, cache_kv_w128, cache_kv_w512, cache_kv_w2048, state_C, state_n, state_m,
           norm_mix, norm_ffn, norm_final, rel_bias, w_a_in, w_a_out, w_b_in, b_b_gate, b_b_norm, w_b_out,
           w_ffn_gate, w_ffn_up, w_ffn_down):
    bp, seq, d_model = x_prompt.shape
    bs, t_new, _ = x_sample.shape
    depth = norm_mix.shape[0]
    nh, dk, dv = state_C.shape[2], state_C.shape[3], state_C.shape[4]
    main_w = 2 * nh * dk + 2 * nh * dv
    he = A_HEADS * A_HEAD_DIM
    caches = [c.reshape(c.shape[0], bs, c.shape[2], 2 * he) for c in (cache_kv_w128, cache_kv_w512, cache_kv_w2048)]

    biases = [_group_bias(rel_bias, g) for g in range(N_GROUPS)]
    band = _band_tables(biases)
    tables = _sample_tables(biases, [c.shape[2] for c in caches], t_new)

    xp = x_prompt.reshape(bp * seq, d_model)
    xs = x_sample.reshape(bs * t_new, d_model)
    kv_p = [[] for _ in range(N_GROUPS)]
    kv_s = [[] for _ in range(N_GROUPS)]
    st_p, st_s = [], []
    for layer in range(depth):
        j = layer // 2
        hp = rmsnorm(xp, norm_mix[layer], BF16)
        hs = rmsnorm(xs, norm_mix[layer], BF16)
        if layer % 2 == 0:
            qp, kvp = attn_in_proj(hp, w_a_in, j)
            qs, kvs = attn_in_proj(hs, w_a_in, j)
            yp = attention_prompt(qp, kvp, band, bp, seq)
            ys = attention_sample(qs, kvs, caches, j, tables, bs, t_new)
            xp = dense(yp, w_a_out, j, res=xp, name="attn_out_proj")
            xs = dense(ys, w_a_out, j, res=xs, name="attn_out_proj")
            for g, (window, _) in enumerate(A_GROUPS):
                keep = min(window, seq)
                kv_p[g].append(kvp[g].reshape(bp, seq, 2, A_HEADS, A_HEAD_DIM)[:, seq - keep:])
                kv_s[g].append(kvs[g].reshape(bs, t_new, 2, A_HEADS, A_HEAD_DIM))
        else:
            w_gate = w_b_in[j, :, main_w:]
            zero = (jnp.zeros((bp, nh, dk, dv), F32), jnp.zeros((bp, nh, dk), F32), jnp.zeros((bp, nh), F32))
            past = (state_C[j], state_n[j], state_m[j])
            zp = dense(hp, w_b_in, j, n_cols=main_w, name="mlstm_in_proj")
            zs = dense(hs, w_b_in, j, n_cols=main_w, name="mlstm_in_proj")
            hnp, sp = mlstm(zp, gate_proj(hp, w_gate), b_b_gate[j], b_b_norm[j], zero, bp, seq,
                            min(PROMPT_CHUNK, seq), BF16)
            hns, ss = mlstm(zs, gate_proj(hs, w_gate), b_b_gate[j], b_b_norm[j], past, bs, t_new, t_new, F32)
            xp = dense(hnp, w_b_out, j, res=xp, name="mlstm_out_proj")
            xs = dense(hns, w_b_out, j, res=xs, name="mlstm_out_proj")
            st_p.append(sp)
            st_s.append(ss)
        ap = ffn_up(rmsnorm(xp, norm_ffn[layer], BF16), w_ffn_gate, w_ffn_up, layer)
        as_ = ffn_up(rmsnorm(xs, norm_ffn[layer], BF16), w_ffn_gate, w_ffn_up, layer)
        xp = dense(ap, w_ffn_down, layer, res=xp, name="ffn_down")
        xs = dense(as_, w_ffn_down, layer, res=xs, name="ffn_down")

    y_prompt = rmsnorm(xp, norm_final, F32).reshape(bp, seq, d_model)
    y_sample = rmsnorm(xs, norm_final, F32).reshape(bs, t_new, d_model)
    kvd = cache_kv_w128.dtype
    sd = state_C.dtype
    kv_out_p = [jnp.stack(kv_p[g]).astype(kvd) for g in range(N_GROUPS)]
    kv_out_s = [jnp.stack(kv_s[g]).astype(kvd) for g in range(N_GROUPS)]
    states_p = [jnp.stack([s[i] for s in st_p]).astype(sd) for i in range(3)]
    states_s = [jnp.stack([s[i] for s in st_s]).astype(sd) for i in range(3)]
    return (y_prompt, y_sample, *kv_out_p, *kv_out_s, *states_p, *states_s)
```

```python
import functools
import math

import jax
import jax.numpy as jnp
from jax import lax
from jax.experimental import pallas as pl
from jax.experimental.pallas import tpu as pltpu

F32 = jnp.float32
BF16 = jnp.bfloat16

EPS = 1e-6
NEG = -1e30

A_GROUPS = ((128, 1), (512, 4), (2048, 16))
N_GROUPS = len(A_GROUPS)
A_HEADS = 8
A_HEAD_DIM = 128
A_STEPS = 128
A_BLOCK = 128
NUM_BUCKETS = 32
MAX_DISTANCE = 2048
B_HEADS = 8
LANES = 128
VMEM_LIMIT = 56 * 1024 * 1024
PROMPT_CHUNK = 256


def _cparams(sem):
    return pltpu.CompilerParams(dimension_semantics=sem, vmem_limit_bytes=VMEM_LIMIT)


def _pick(n, cands):
    for c in cands:
        if n % c == 0:
            return c
    return n


def _rmsnorm_kernel(x_ref, xs_ref, g_ref, o_ref, os_ref):
    def norm(x):
        return x * lax.rsqrt(jnp.mean(x * x, axis=-1, keepdims=True) + EPS) * g_ref[...]

    @pl.when(pl.program_id(0) == 0)
    def _():
        os_ref[...] = norm(xs_ref[...]).astype(os_ref.dtype)

    o_ref[...] = norm(x_ref[...]).astype(o_ref.dtype)


def rmsnorm(x, xs, g, out_dtype):
    m, d = x.shape
    ms = xs.shape[0]
    tm = _pick(m, (512, 256, 128, 64))
    return pl.pallas_call(
        _rmsnorm_kernel,
        grid=(m // tm,),
        in_specs=[pl.BlockSpec((tm, d), lambda i: (i, 0)),
                  pl.BlockSpec((ms, d), lambda i: (0, 0)),
                  pl.BlockSpec((1, d), lambda i: (0, 0))],
        out_specs=[pl.BlockSpec((tm, d), lambda i: (i, 0)),
                   pl.BlockSpec((ms, d), lambda i: (0, 0))],
        out_shape=[jax.ShapeDtypeStruct((m, d), out_dtype), jax.ShapeDtypeStruct((ms, d), out_dtype)],
        compiler_params=_cparams(("arbitrary",)),
        name="rmsnorm",
    )(x, xs, g.reshape(1, d))


def _linear_kernel(*refs, n_w, has_res, swiglu, m_axis):
    x_ref, xs_ref = refs[:2]
    w_refs = refs[2:2 + n_w]
    p = 2 + n_w
    r_ref, rs_ref = (refs[p], refs[p + 1]) if has_res else (None, None)
    p += 2 * has_res
    o_ref, os_ref = refs[p], refs[p + 1]
    wb_refs = refs[p + 2:]

    def apply(x_ref, r_ref, o_ref):
        x = x_ref[...].astype(BF16)
        accs = [jnp.dot(x, wb_ref[...], preferred_element_type=F32) for wb_ref in wb_refs]
        out = accs[0] * jax.nn.sigmoid(accs[0]) * accs[1] if swiglu else accs[0]
        if has_res:
            out = r_ref[...] + out
        o_ref[...] = out.astype(o_ref.dtype)

    @pl.when(pl.program_id(m_axis) == 0)
    def _():
        for w_ref, wb_ref in zip(w_refs, wb_refs):
            wb_ref[...] = w_ref[...].astype(BF16)
        apply(xs_ref, rs_ref, os_ref)

    apply(x_ref, r_ref, o_ref)


def linear(x, xs, ws, w_index, outer, out_shape, out_block, out_index, tn, *, res=None, res_s=None,
           swiglu=False, out_dtype=F32, name="linear"):
    m, k = x.shape
    ms = xs.shape[0]
    tm = _pick(m, (512, 256, 128, 64))
    nd = len(outer)
    has_res = res is not None
    main_map = lambda *g: out_index(*g)
    sample_map = lambda *g: out_index(*g[:nd], 0)
    in_specs = [pl.BlockSpec((tm, k), lambda *g: (g[nd], 0)),
                pl.BlockSpec((ms, k), lambda *g: (0, 0))]
    in_specs += [pl.BlockSpec((None, k, tn), lambda *g: w_index(*g[:nd])) for _ in ws]
    args = [x, xs, *ws]
    if has_res:
        in_specs += [pl.BlockSpec(out_block(tm), main_map), pl.BlockSpec(out_block(ms), sample_map)]
        args += [res, res_s]
    return pl.pallas_call(
        functools.partial(_linear_kernel, n_w=len(ws), has_res=has_res, swiglu=swiglu, m_axis=nd),
        grid=(*outer, m // tm),
        in_specs=in_specs,
        out_specs=[pl.BlockSpec(out_block(tm), main_map), pl.BlockSpec(out_block(ms), sample_map)],
        out_shape=[jax.ShapeDtypeStruct(out_shape(m), out_dtype), jax.ShapeDtypeStruct(out_shape(ms), out_dtype)],
        scratch_shapes=[pltpu.VMEM((k, tn), BF16) for _ in ws],
        compiler_params=_cparams(("parallel",) * nd + ("arbitrary",)),
        name=name,
    )(*args)


def dense(x, xs, w, layer, *, res=None, res_s=None, out_dtype=F32, n_cols=None, name="dense"):
    n = n_cols or w.shape[2]
    tn = _pick(n, (1024, 512, 256, 128)) if w.shape[1] <= 2048 else _pick(n, (512, 256, 128))
    return linear(x, xs, [w], lambda j: (layer, 0, j), (n // tn,), lambda m: (m, n),
                  lambda tm: (tm, tn), lambda j, i: (i, j), tn, res=res, res_s=res_s,
                  out_dtype=out_dtype, name=name)


def ffn_up(x, xs, wg, wu, layer):
    n = wg.shape[2]
    tn = _pick(n, (512, 256, 128))
    return linear(x, xs, [wg, wu], lambda j: (layer, 0, j), (n // tn,), lambda m: (m, n),
                  lambda tm: (tm, tn), lambda j, i: (i, j), tn, swiglu=True, out_dtype=BF16, name="ffn_up")


def attn_q_proj(x, xs, w, layer):
    he = A_HEADS * A_HEAD_DIM
    return linear(x, xs, [w], lambda g: (layer, 0, 3 * g), (N_GROUPS,), lambda m: (m, N_GROUPS * he),
                  lambda tm: (tm, he), lambda g, i: (i, g), he, name="attn_q_proj")


def _kv_proj_kernel(x_ref, xs_ref, w_ref, o_ref, os_ref, f_ref, fs_ref, wb_ref, *,
                    tiles_per_batch, first_kept, keep_blk):
    i = pl.program_id(1)
    nh, e = f_ref.shape[-2:]

    @pl.when(i == 0)
    def _():
        wb_ref[...] = w_ref[...].astype(BF16)
        acc = jnp.dot(xs_ref[...].astype(BF16), wb_ref[...], preferred_element_type=F32)
        os_ref[...] = acc
        for h in range(nh):
            fs_ref[:, h, :] = acc[:, h * e:(h + 1) * e]

    acc = jnp.dot(x_ref[...].astype(BF16), wb_ref[...], preferred_element_type=F32)
    o_ref[...] = acc

    @pl.when(i % tiles_per_batch >= first_kept)
    def _():
        tm = acc.shape[0]
        for h in range(nh):
            f_ref[:, h, :] = acc[tm - keep_blk:, h * e:(h + 1) * e]


def attn_kv_proj(x, xs, w, layer, group, batch, seq, keep):
    m, k = x.shape
    ms = xs.shape[0]
    he = A_HEADS * A_HEAD_DIM
    tm = _pick(seq, (512, 256, 128))
    keep_blk = min(keep, tm)
    assert keep % keep_blk == 0 and tm % keep_blk == 0
    tpb = seq // tm
    first_kept = tpb - keep // keep_blk if keep >= tm else tpb - 1

    def kept_map(t, i):
        return (i // tpb, jnp.maximum(i % tpb - first_kept, 0), t, 0, 0)

    return pl.pallas_call(
        functools.partial(_kv_proj_kernel, tiles_per_batch=tpb, first_kept=first_kept, keep_blk=keep_blk),
        grid=(2, m // tm),
        in_specs=[pl.BlockSpec((tm, k), lambda t, i: (i, 0)),
                  pl.BlockSpec((ms, k), lambda t, i: (0, 0)),
                  pl.BlockSpec((None, k, he), lambda t, i: (layer, 0, 3 * group + 1 + t))],
        out_specs=[pl.BlockSpec((tm, he), lambda t, i: (i, t)),
                   pl.BlockSpec((ms, he), lambda t, i: (0, t)),
                   pl.BlockSpec((None, keep_blk, None, A_HEADS, A_HEAD_DIM), kept_map),
                   pl.BlockSpec((ms, None, A_HEADS, A_HEAD_DIM), lambda t, i: (0, t, 0, 0))],
        out_shape=[jax.ShapeDtypeStruct((m, 2 * he), F32),
                   jax.ShapeDtypeStruct((ms, 2 * he), F32),
                   jax.ShapeDtypeStruct((batch, keep, 2, A_HEADS, A_HEAD_DIM), F32),
                   jax.ShapeDtypeStruct((ms, 2, A_HEADS, A_HEAD_DIM), F32)],
        scratch_shapes=[pltpu.VMEM((k, he), BF16)],
        compiler_params=_cparams(("parallel", "arbitrary")),
        name="attn_kv_proj",
    )(x, xs, w)


def _gate_kernel(x_ref, xs_ref, w_ref, o_ref, os_ref):
    w = w_ref[:, :o_ref.shape[1]].astype(BF16)

    @pl.when(pl.program_id(0) == 0)
    def _():
        os_ref[...] = jnp.dot(xs_ref[...], w, preferred_element_type=F32)

    o_ref[...] = jnp.dot(x_ref[...], w, preferred_element_type=F32)


def gate_proj(x, xs, w, layer, col0, n):
    m, k = x.shape
    ms = xs.shape[0]
    assert col0 % LANES == 0 and n <= LANES and col0 + n <= w.shape[2]
    tm = _pick(m, (512, 256, 128, 64))
    return pl.pallas_call(
        _gate_kernel,
        grid=(m // tm,),
        in_specs=[pl.BlockSpec((tm, k), lambda i: (i, 0)),
                  pl.BlockSpec((ms, k), lambda i: (0, 0)),
                  pl.BlockSpec((None, k, LANES), lambda i: (layer, 0, col0 // LANES))],
        out_specs=[pl.BlockSpec((tm, n), lambda i: (i, 0)), pl.BlockSpec((ms, n), lambda i: (0, 0))],
        out_shape=[jax.ShapeDtypeStruct((m, n), F32), jax.ShapeDtypeStruct((ms, n), F32)],
        compiler_params=_cparams(("arbitrary",)),
        name="gate_proj",
    )(x, xs, w)


def _t5_bucket(dist):
    max_exact = NUM_BUCKETS // 2
    d = jnp.maximum(dist.astype(F32), 1.0)
    large = max_exact + jnp.log(d / max_exact) / math.log(MAX_DISTANCE / max_exact) * (NUM_BUCKETS - max_exact)
    large = jnp.minimum(large.astype(jnp.int32), NUM_BUCKETS - 1)
    return jnp.where(dist < max_exact, dist, large)


def _group_bias(rel_bias, g):
    dil = A_GROUPS[g][1]
    buckets = _t5_bucket(jnp.arange(A_STEPS + 1, dtype=jnp.int32) * dil)
    onehot = (buckets[:, None] == jnp.arange(NUM_BUCKETS)[None, :]).astype(F32)
    cols = rel_bias[:, g * A_HEADS:(g + 1) * A_HEADS].astype(F32)
    return jnp.einsum("kb,bh->hk", onehot, cols, precision=lax.Precision.HIGHEST)


def _toeplitz(f, lo, n_rows, n_cols, offset):
    first, last = offset - n_cols + 1, offset + n_rows - 1
    pad_lo, pad_hi = max(lo - first, 0), max(last - (lo + f.shape[-1] - 1), 0)
    fe = jnp.pad(f, [(0, 0)] * (f.ndim - 1) + [(pad_lo, pad_hi)], constant_values=NEG)
    base = lo - pad_lo
    ext = lambda a, b: fe[..., a - base:b - base]
    period = n_rows + n_cols - 1
    w = jnp.concatenate([ext(first, offset + 1)[..., ::-1], ext(offset + 1, last + 1)[..., ::-1]], axis=-1)
    flat = jnp.tile(w, n_rows)[..., :n_rows * (period - 1)]
    return flat.reshape(*f.shape[:-1], n_rows, period - 1)[..., :n_cols]


def _band_tables(biases):
    return jnp.stack([_toeplitz(b, 0, A_BLOCK, 2 * A_BLOCK, A_BLOCK) for b in biases], axis=1)


def _rows(start, size, stride):
    return pl.ds(start, size) if stride == 1 else pl.ds(start, size, stride=stride)


def _masked_logits(q, k, tab, scale):
    s = lax.dot_general(q.astype(BF16), k.astype(BF16), (((1,), (1,)), ((), ())),
                        preferred_element_type=F32) * scale
    return jnp.where(tab > 0.5 * NEG, s + tab, NEG)


def _attn_prompt_kernel(*refs, seq):
    qkv = refs[:3 * N_GROUPS]
    band_ref, y_ref, o_scr, lse_scr = refs[3 * N_GROUPS:]
    scale = A_HEAD_DIM ** -0.5
    q_blk = A_BLOCK

    def block(g, dil, q_start, k_start, n_keys, tab):
        q_ref, k_ref, v_ref = qkv[3 * g:3 * g + 3]
        q = q_ref[_rows(q_start, q_blk, dil), :]
        k = k_ref[_rows(k_start, n_keys, dil), :]
        v = v_ref[_rows(k_start, n_keys, dil), :]
        s = _masked_logits(q, k, tab, scale)
        m = jnp.max(s, axis=-1, keepdims=True)
        p = jnp.exp(s - m)
        l = jnp.sum(p, axis=-1, keepdims=True)
        o = jnp.dot(p.astype(BF16), v.astype(BF16), preferred_element_type=F32)
        rows = _rows(q_start, q_blk, dil)
        o_scr[g, rows, :] = o / l
        lse_scr[g, rows, :] = jnp.broadcast_to(m + jnp.log(l), (q_blk, LANES))

    for g, (_, dil) in enumerate(A_GROUPS):
        sub_len = seq // dil
        nb = sub_len // q_blk

        def residue(r, carry, g=g, dil=dil, nb=nb):
            block(g, dil, r, r, q_blk, band_ref[g, :, q_blk:])

            def later(n, c):
                start = r + dil * q_blk * (n - 1)
                block(g, dil, start + dil * q_blk, start, 2 * q_blk, band_ref[g])
                return c

            if nb > 1:
                lax.fori_loop(1, nb, later, 0, unroll=min(nb - 1, 15))
            return carry

        for r in range(dil):
            residue(r, 0)

    chunk = 256

    def merge(c, carry):
        rows = pl.ds(pl.multiple_of(c * chunk, chunk), chunk)
        lses = [lse_scr[g, rows, :] for g in range(N_GROUPS)]
        top = functools.reduce(jnp.maximum, lses)
        ws = [jnp.exp(lse - top) for lse in lses]
        num = functools.reduce(jnp.add, [w * o_scr[g, rows, :] for g, w in enumerate(ws)])
        y_ref[rows, :] = (num / functools.reduce(jnp.add, ws)).astype(y_ref.dtype)
        return carry

    lax.fori_loop(0, seq // chunk, merge, 0)


def attention_prompt(q, kv, band, batch, seq):
    e = A_HEAD_DIM
    assert all(seq % (dil * A_BLOCK) == 0 for _, dil in A_GROUPS)
    in_specs, args = [], []
    for g in range(N_GROUPS):
        in_specs.append(pl.BlockSpec((seq, e), lambda b, h, g=g: (b, g * A_HEADS + h)))
        in_specs.append(pl.BlockSpec((seq, e), lambda b, h: (b, h)))
        in_specs.append(pl.BlockSpec((seq, e), lambda b, h: (b, A_HEADS + h)))
        args += [q, kv[g], kv[g]]
    in_specs.append(pl.BlockSpec((None, N_GROUPS, A_BLOCK, 2 * A_BLOCK), lambda b, h: (h, 0, 0, 0)))
    return pl.pallas_call(
        functools.partial(_attn_prompt_kernel, seq=seq),
        grid=(batch, A_HEADS),
        in_specs=in_specs,
        out_specs=pl.BlockSpec((seq, e), lambda b, h: (b, h)),
        out_shape=jax.ShapeDtypeStruct((batch * seq, A_HEADS * e), BF16),
        scratch_shapes=[pltpu.VMEM((N_GROUPS, seq, e), F32),
                        pltpu.VMEM((N_GROUPS, seq, LANES), F32)],
        compiler_params=_cparams(("parallel", "parallel")),
        name="attention_prompt",
    )(*args, band)


def _attn_sample_kernel(*refs):
    q_ref = refs[0]
    kv_new = refs[1:1 + N_GROUPS]
    caches = refs[1 + N_GROUPS:1 + 3 * N_GROUPS]
    tables = refs[1 + 3 * N_GROUPS:1 + 5 * N_GROUPS]
    y_ref = refs[1 + 5 * N_GROUPS]
    scale = A_HEAD_DIM ** -0.5
    e = A_HEAD_DIM

    def head_rows(ref, h):
        if len(ref.shape) == 3:
            return ref[:, h, :]
        x = ref[:, :, h, :]
        return x.reshape(x.shape[0] * x.shape[1], e)

    for h in range(A_HEADS):
        lses, outs = [], []
        for g in range(N_GROUPS):
            q = q_ref[:, (g * A_HEADS + h) * e:(g * A_HEADS + h + 1) * e]
            k_new = kv_new[g][:, h * e:(h + 1) * e]
            v_new = kv_new[g][:, (A_HEADS + h) * e:(A_HEADS + h + 1) * e]
            k_old = head_rows(caches[2 * g], h)
            v_old = head_rows(caches[2 * g + 1], h)
            s_old = _masked_logits(q, k_old, tables[2 * g][h], scale)
            s_new = _masked_logits(q, k_new, tables[2 * g + 1][h], scale)
            m = jnp.maximum(jnp.max(s_old, axis=-1, keepdims=True), jnp.max(s_new, axis=-1, keepdims=True))
            p_old = jnp.exp(s_old - m)
            p_new = jnp.exp(s_new - m)
            l = jnp.sum(p_old, axis=-1, keepdims=True) + jnp.sum(p_new, axis=-1, keepdims=True)
            o = (jnp.dot(p_old.astype(BF16), v_old.astype(BF16), preferred_element_type=F32)
                 + jnp.dot(p_new.astype(BF16), v_new.astype(BF16), preferred_element_type=F32))
            outs.append(o / l)
            lses.append(m + jnp.log(l))
        top = functools.reduce(jnp.maximum, lses)
        ws = [jnp.exp(lse - top) for lse in lses]
        num = functools.reduce(jnp.add, [w * o for w, o in zip(ws, outs)])
        y_ref[:, h * e:(h + 1) * e] = (num / functools.reduce(jnp.add, ws)).astype(y_ref.dtype)


def attention_sample(q, kv_new, caches, layer, biases, batch, t_new):
    he = A_HEADS * A_HEAD_DIM
    in_specs = [pl.BlockSpec((t_new, N_GROUPS * he), lambda b: (b, 0))]
    args = [q]
    for g in range(N_GROUPS):
        in_specs.append(pl.BlockSpec((t_new, 2 * he), lambda b: (b, 0)))
        args.append(kv_new[g])
    tables = []
    for g, (_, dil) in enumerate(A_GROUPS):
        c = caches[g]
        n_past = c.shape[2]
        bg = biases[g]
        by_dist = jnp.concatenate([bg[..., None], jnp.full(bg.shape + (dil - 1,), NEG, F32)], axis=-1)
        by_dist = by_dist.reshape(bg.shape[0], -1)[:, :A_STEPS * dil + 1]
        tab = _toeplitz(by_dist, 0, t_new, n_past + t_new, n_past)
        tab_old, tab_new = tab[:, :, :n_past], tab[:, :, n_past:]
        if dil % t_new == 0 and n_past % dil == 0:
            c = c.reshape(c.shape[0], batch, n_past // dil, dil, 2, A_HEADS, A_HEAD_DIM)
            for t in range(2):
                in_specs.append(pl.BlockSpec((None, None, n_past // dil, t_new, None, A_HEADS, A_HEAD_DIM),
                                             lambda b, t=t: (layer, b, 0, 0, t, 0, 0)))
                args.append(c)
            tab_old = tab_old.reshape(A_HEADS, t_new, n_past // dil, dil)[..., :t_new]
            tab_old = tab_old.reshape(A_HEADS, t_new, (n_past // dil) * t_new)
        else:
            for t in range(2):
                in_specs.append(pl.BlockSpec((None, None, n_past, None, A_HEADS, A_HEAD_DIM),
                                             lambda b, t=t: (layer, b, 0, t, 0, 0)))
                args.append(c)
        tables += [tab_old, tab_new]
    for tab in tables:
        in_specs.append(pl.BlockSpec(tab.shape, lambda b: (0, 0, 0)))
        args.append(tab)
    return pl.pallas_call(
        _attn_sample_kernel,
        grid=(batch,),
        in_specs=in_specs,
        out_specs=pl.BlockSpec((t_new, he), lambda b: (b, 0)),
        out_shape=jax.ShapeDtypeStruct((batch * t_new, he), F32),
        compiler_params=_cparams(("parallel",)),
        name="attention_sample",
    )(*args)


def _log_sigmoid(x):
    return jnp.minimum(x, 0.0) - jnp.log1p(jnp.exp(-jnp.abs(x)))


def _mlstm_kernel(q_ref, k_ref, v_ref, og_ref, gc_ref, gr_ref, bc_ref, br_ref, gn_ref,
                  c0_ref, n0_ref, m0_ref,
                  h_ref, c_out, n_out, m_out,
                  c_scr, n_scr, m_scr, *, chunk, dk):
    hd = pl.program_id(1)
    c = pl.program_id(2)

    @pl.when(c == 0)
    def _():
        c_scr[...] = c0_ref[...].astype(F32)
        n_scr[...] = n0_ref[...].astype(F32)
        m_scr[...] = m0_ref[...].astype(F32)

    nh = B_HEADS
    gc = gc_ref[...] + bc_ref[...]
    gr = gr_ref[...] + br_ref[...]
    lane = lax.broadcasted_iota(jnp.int32, gc.shape, 1)
    sub = lax.broadcasted_iota(jnp.int32, gr.shape, 0)
    i_col = jnp.sum(jnp.where(lane == hd, gc, 0.0), axis=1, keepdims=True)
    f_col = jnp.sum(jnp.where(lane == hd + nh, gc, 0.0), axis=1, keepdims=True)
    i_row = jnp.sum(jnp.where(sub == hd, gr, 0.0), axis=0, keepdims=True)
    f_row = jnp.sum(jnp.where(sub == hd + nh, gr, 0.0), axis=0, keepdims=True)
    logf_col = _log_sigmoid(f_col)
    logf_row = _log_sigmoid(f_row)

    qi = lax.broadcasted_iota(jnp.int32, (chunk, chunk), 0)
    si = lax.broadcasted_iota(jnp.int32, (chunk, chunk), 1)
    causal = si <= qi
    b_col = jnp.sum(jnp.where(causal, logf_row, 0.0), axis=1, keepdims=True)
    b_row = jnp.sum(jnp.where(qi <= si, logf_col, 0.0), axis=0, keepdims=True)
    b_last = jnp.sum(logf_row, axis=1, keepdims=True)

    m_prev = m_scr[...]
    a = b_col + m_prev
    dmat = jnp.where(causal, b_col - b_row + i_row, NEG)
    m_t = jnp.maximum(a, jnp.max(dmat, axis=1, keepdims=True))
    w_inter = jnp.exp(a - m_t)

    qf = q_ref[...].astype(F32)
    q = qf.astype(BF16)
    kf = k_ref[...].astype(F32) * (dk ** -0.5)
    k = kf.astype(BF16)
    v = v_ref[...].astype(BF16)
    c_prev = c_scr[...]
    n_prev = n_scr[...]
    s = lax.dot_general(q, k, (((1,), (1,)), ((), ())), preferred_element_type=F32) * jnp.exp(dmat - m_t)
    num = (w_inter * jnp.dot(q, c_prev.astype(BF16), preferred_element_type=F32)
           + jnp.dot(s.astype(BF16), v, preferred_element_type=F32))
    qn = jnp.sum(qf * n_prev, axis=1, keepdims=True)
    den = w_inter * qn + jnp.sum(s, axis=1, keepdims=True)
    h = num / jnp.maximum(jnp.abs(den), jnp.exp(-m_t))

    m_new = jnp.maximum(b_last + m_prev, jnp.max(b_last - b_row + i_row, axis=1, keepdims=True))
    w_state = jnp.exp(b_last - b_col + i_col - m_new)
    decay = jnp.exp(b_last + m_prev - m_new)
    kw = (w_state * kf).astype(BF16)
    c_scr[...] = decay * c_prev + lax.dot_general(kw, v, (((0,), (0,)), ((), ())),
                                                  preferred_element_type=F32)
    n_scr[...] = decay * n_prev + jnp.sum(w_state * kf, axis=0, keepdims=True)
    m_scr[...] = m_new

    ht = h * jax.nn.sigmoid(og_ref[...].astype(F32))
    hn = ht * lax.rsqrt(jnp.mean(ht * ht, axis=-1, keepdims=True) + EPS)
    h_ref[...] = (hn * gn_ref[...]).astype(h_ref.dtype)

    @pl.when(c == pl.num_programs(2) - 1)
    def _():
        c_out[...] = c_scr[...].astype(c_out.dtype)
        n_out[...] = n_scr[...].astype(n_out.dtype)
        m_out[...] = m_scr[...].astype(m_out.dtype)


def mlstm(z, gates, b_gate, g_norm, state, batch, seq, chunk, out_dtype):
    c0, n0, m0 = state
    nh = B_HEADS
    dk, dv = c0.shape[-2], c0.shape[-1]
    nc = seq // chunk
    r = dv // dk
    gcol = gates.reshape(batch * nc, chunk, 2 * nh)
    grow = jnp.swapaxes(gcol, 1, 2)
    row = lambda b, h, c: b * nc + c
    in_specs = [
        pl.BlockSpec((chunk, dk), lambda b, h, c: (row(b, h, c), h)),
        pl.BlockSpec((chunk, dk), lambda b, h, c: (row(b, h, c), nh + h)),
        pl.BlockSpec((chunk, dv), lambda b, h, c: (row(b, h, c), (2 * nh) // r + h)),
        pl.BlockSpec((chunk, dv), lambda b, h, c: (row(b, h, c), (2 * nh) // r + nh + h)),
        pl.BlockSpec((None, chunk, 2 * nh), lambda b, h, c: (row(b, h, c), 0, 0)),
        pl.BlockSpec((None, 2 * nh, chunk), lambda b, h, c: (row(b, h, c), 0, 0)),
        pl.BlockSpec((1, 2 * nh), lambda b, h, c: (0, 0)),
        pl.BlockSpec((2 * nh, 1), lambda b, h, c: (0, 0)),
        pl.BlockSpec((1, dv), lambda b, h, c: (0, h)),
        pl.BlockSpec((None, None, dk, dv), lambda b, h, c: (b, h, 0, 0)),
        pl.BlockSpec((None, None, 1, dk), lambda b, h, c: (b, h, 0, 0)),
        pl.BlockSpec((None, None, 1, 1), lambda b, h, c: (b, h, 0, 0)),
    ]
    out_specs = [
        pl.BlockSpec((chunk, dv), lambda b, h, c: (row(b, h, c), h)),
        pl.BlockSpec((None, None, dk, dv), lambda b, h, c: (b, h, 0, 0)),
        pl.BlockSpec((None, None, 1, dk), lambda b, h, c: (b, h, 0, 0)),
        pl.BlockSpec((None, None, 1, 1), lambda b, h, c: (b, h, 0, 0)),
    ]
    out_shape = [
        jax.ShapeDtypeStruct((batch * seq, nh * dv), out_dtype),
        jax.ShapeDtypeStruct((batch, nh, dk, dv), F32),
        jax.ShapeDtypeStruct((batch, nh, 1, dk), F32),
        jax.ShapeDtypeStruct((batch, nh, 1, 1), F32),
    ]
    bg = b_gate.astype(F32)
    hn, c_new, n_new, m_new = pl.pallas_call(
        functools.partial(_mlstm_kernel, chunk=chunk, dk=dk),
        grid=(batch, nh, nc),
        in_specs=in_specs,
        out_specs=out_specs,
        out_shape=out_shape,
        scratch_shapes=[pltpu.VMEM((dk, dv), F32), pltpu.VMEM((1, dk), F32), pltpu.VMEM((1, 1), F32)],
        compiler_params=_cparams(("parallel", "parallel", "arbitrary")),
        name="mlstm",
    )(z, z, z, z, gcol, grow, bg.reshape(1, 2 * nh), bg.reshape(2 * nh, 1),
      g_norm.astype(F32).reshape(1, nh * dv),
      c0.reshape(batch, nh, dk, dv), n0.reshape(batch, nh, 1, dk), m0.reshape(batch, nh, 1, 1))
    return hn, (c_new, n_new.reshape(batch, nh, dk), m_new.reshape(batch, nh))


def kernel(x_prompt, x_sample, cache_kv_w128, cache_kv_w512, cache_kv_w2048, state_C, state_n, state_m,
           norm_mix, norm_ffn, norm_final, rel_bias, w_a_in, w_a_out, w_b_in, b_b_gate, b_b_norm, w_b_out,
           w_ffn_gate, w_ffn_up, w_ffn_down):
    bp, seq, d_model = x_prompt.shape
    bs, t_new, _ = x_sample.shape
    depth = norm_mix.shape[0]
    nh, dk, dv = state_C.shape[2], state_C.shape[3], state_C.shape[4]
    main_w = 2 * nh * dk + 2 * nh * dv
    caches = (cache_kv_w128, cache_kv_w512, cache_kv_w2048)

    biases = [_group_bias(rel_bias, g) for g in range(N_GROUPS)]
    band = _band_tables(biases)

    xp = x_prompt.reshape(bp * seq, d_model)
    xs = x_sample.reshape(bs * t_new, d_model)
    kv_p = [[] for _ in range(N_GROUPS)]
    kv_s = [[] for _ in range(N_GROUPS)]
    st_p, st_s = [], []
    for layer in range(depth):
        j = layer // 2
        hp, hs = rmsnorm(xp, xs, norm_mix[layer], BF16)
        if layer % 2 == 0:
            qp, qs = attn_q_proj(hp, hs, w_a_in, j)
            kvp, kvs = [], []
            for g, (window, _) in enumerate(A_GROUPS):
                kp, ks, fp, fs = attn_kv_proj(hp, hs, w_a_in, j, g, bp, seq, min(window, seq))
                kvp.append(kp)
                kvs.append(ks)
                kv_p[g].append(fp)
                kv_s[g].append(fs.reshape(bs, t_new, 2, A_HEADS, A_HEAD_DIM))
            yp = attention_prompt(qp, kvp, band, bp, seq)
            ys = attention_sample(qs, kvs, caches, j, biases, bs, t_new)
            xp, xs = dense(yp, ys, w_a_out, j, res=xp, res_s=xs, name="attn_out_proj")
        else:
            zero = (jnp.zeros((bp, nh, dk, dv), F32), jnp.zeros((bp, nh, dk), F32), jnp.zeros((bp, nh), F32))
            past = (state_C[j], state_n[j], state_m[j])
            zp, zs = dense(hp, hs, w_b_in, j, n_cols=main_w, name="mlstm_in_proj")
            gp, gs = gate_proj(hp, hs, w_b_in, j, main_w, 2 * nh)
            hnp, sp = mlstm(zp, gp, b_b_gate[j], b_b_norm[j], zero, bp, seq, min(PROMPT_CHUNK, seq), BF16)
            hns, ss = mlstm(zs, gs, b_b_gate[j], b_b_norm[j], past, bs, t_new, t_new, F32)
            xp, xs = dense(hnp, hns, w_b_out, j, res=xp, res_s=xs, name="mlstm_out_proj")
            st_p.append(sp)
            st_s.append(ss)
        hp, hs = rmsnorm(xp, xs, norm_ffn[layer], BF16)
        ap, as_ = ffn_up(hp, hs, w_ffn_gate, w_ffn_up, layer)
        xp, xs = dense(ap, as_, w_ffn_down, layer, res=xp, res_s=xs, name="ffn_down")

    y_prompt, y_sample = rmsnorm(xp, xs, norm_final, F32)
    y_prompt = y_prompt.reshape(bp, seq, d_model)
    y_sample = y_sample.reshape(bs, t_new, d_model)
    kvd = cache_kv_w128.dtype
    sd = state_C.dtype
    kv_out_p = [jnp.stack(kv_p[g]).astype(kvd) for g in range(N_GROUPS)]
    kv_out_s = [jnp.stack(kv_s[g]).astype(kvd) for g in range(N_GROUPS)]
    states_p = [jnp.stack([s[i] for s in st_p]).astype(sd) for i in range(3)]
    states_s = [jnp.stack([s[i] for s in st_s]).astype(sd) for i in range(3)]
    return (y_prompt, y_sample, *kv_out_p, *kv_out_s, *states_p, *states_s)
```

```python
import functools
import math

import jax
import jax.numpy as jnp
from jax import lax
from jax.experimental import pallas as pl
from jax.experimental.pallas import tpu as pltpu

F32 = jnp.float32
BF16 = jnp.bfloat16

EPS = 1e-6
NEG = -1e30

A_GROUPS = ((128, 1), (512, 4), (2048, 16))
N_GROUPS = len(A_GROUPS)
A_HEADS = 8
A_HEAD_DIM = 128
A_STEPS = 128
A_BLOCK = 128
NUM_BUCKETS = 32
MAX_DISTANCE = 2048
B_HEADS = 8
LANES = 128
VMEM_LIMIT = 56 * 1024 * 1024
PROMPT_CHUNK = 256


def _cparams(sem):
    return pltpu.CompilerParams(dimension_semantics=sem, vmem_limit_bytes=VMEM_LIMIT)


def _pick(n, cands):
    for c in cands:
        if n % c == 0:
            return c
    return n


def _row_tile(m, fixed_bytes, bytes_per_row):
    for tm in (1024, 512, 256, 128, 64):
        if m % tm == 0 and fixed_bytes + 2 * tm * bytes_per_row <= VMEM_LIMIT:
            return tm
    return m


def _rmsnorm_kernel(x_ref, xs_ref, g_ref, o_ref, os_ref):
    def norm(x):
        return x * lax.rsqrt(jnp.mean(x * x, axis=-1, keepdims=True) + EPS) * g_ref[...]

    @pl.when(pl.program_id(0) == 0)
    def _():
        os_ref[...] = norm(xs_ref[...]).astype(os_ref.dtype)

    o_ref[...] = norm(x_ref[...]).astype(o_ref.dtype)


def rmsnorm(x, xs, g, out_dtype):
    m, d = x.shape
    ms = xs.shape[0]
    tm = _pick(m, (512, 256, 128, 64))
    return pl.pallas_call(
        _rmsnorm_kernel,
        grid=(m // tm,),
        in_specs=[pl.BlockSpec((tm, d), lambda i: (i, 0)),
                  pl.BlockSpec((ms, d), lambda i: (0, 0)),
                  pl.BlockSpec((1, d), lambda i: (0, 0))],
        out_specs=[pl.BlockSpec((tm, d), lambda i: (i, 0)),
                   pl.BlockSpec((ms, d), lambda i: (0, 0))],
        out_shape=[jax.ShapeDtypeStruct((m, d), out_dtype), jax.ShapeDtypeStruct((ms, d), out_dtype)],
        compiler_params=_cparams(("arbitrary",)),
        name="rmsnorm",
    )(x, xs, g.reshape(1, d))


def _linear_kernel(*refs, n_w, has_res, swiglu, m_axis, w_t):
    x_ref, xs_ref = refs[:2]
    w_refs = refs[2:2 + n_w]
    p = 2 + n_w
    r_ref, rs_ref = (refs[p], refs[p + 1]) if has_res else (None, None)
    p += 2 * has_res
    o_ref, os_ref = refs[p], refs[p + 1]
    wb_refs = refs[p + 2:]
    contract = (((1,), (1 if w_t else 0,)), ((), ()))

    def apply(x_ref, r_ref, o_ref):
        x = x_ref[...].astype(BF16)
        accs = [lax.dot_general(x, wb_ref[...], contract, preferred_element_type=F32) for wb_ref in wb_refs]
        out = accs[0] * jax.nn.sigmoid(accs[0]) * accs[1] if swiglu else accs[0]
        if has_res:
            out = r_ref[...] + out
        o_ref[...] = out.astype(o_ref.dtype)

    @pl.when(pl.program_id(m_axis) == 0)
    def _():
        for w_ref, wb_ref in zip(w_refs, wb_refs):
            wb_ref[...] = w_ref[...].astype(BF16)
        apply(xs_ref, rs_ref, os_ref)

    apply(x_ref, r_ref, o_ref)


def linear(x, xs, ws, w_index, outer, out_shape, out_block, out_index, tn, *, res=None, res_s=None,
           swiglu=False, out_dtype=F32, w_t=False, name="linear"):
    m, k = x.shape
    ms = xs.shape[0]
    nd = len(outer)
    has_res = res is not None
    x_bytes, o_bytes = x.dtype.itemsize, jnp.dtype(out_dtype).itemsize
    row_bytes = k * x_bytes + tn * (o_bytes + (res.dtype.itemsize if has_res else 0))
    w_bytes = len(ws) * k * tn * (2 * ws[0].dtype.itemsize + jnp.dtype(BF16).itemsize)
    tm = _row_tile(m, w_bytes + 2 * ms * row_bytes, row_bytes)
    w_block = (None, tn, k) if w_t else (None, k, tn)
    main_map = lambda *g: out_index(*g)
    sample_map = lambda *g: out_index(*g[:nd], 0)
    in_specs = [pl.BlockSpec((tm, k), lambda *g: (g[nd], 0)),
                pl.BlockSpec((ms, k), lambda *g: (0, 0))]
    in_specs += [pl.BlockSpec(w_block, lambda *g: w_index(*g[:nd])) for _ in ws]
    args = [x, xs, *ws]
    if has_res:
        in_specs += [pl.BlockSpec(out_block(tm), main_map), pl.BlockSpec(out_block(ms), sample_map)]
        args += [res, res_s]
    return pl.pallas_call(
        functools.partial(_linear_kernel, n_w=len(ws), has_res=has_res, swiglu=swiglu, m_axis=nd, w_t=w_t),
        grid=(*outer, m // tm),
        in_specs=in_specs,
        out_specs=[pl.BlockSpec(out_block(tm), main_map), pl.BlockSpec(out_block(ms), sample_map)],
        out_shape=[jax.ShapeDtypeStruct(out_shape(m), out_dtype), jax.ShapeDtypeStruct(out_shape(ms), out_dtype)],
        scratch_shapes=[pltpu.VMEM(w_block[1:], BF16) for _ in ws],
        compiler_params=_cparams(("parallel",) * nd + ("arbitrary",)),
        name=name,
    )(*args)


def dense(x, xs, w, layer, *, res=None, res_s=None, out_dtype=F32, n_cols=None, w_t=False, name="dense"):
    k = x.shape[1]
    n = n_cols or w.shape[1 if w_t else 2]
    tn = _pick(n, (1024, 512, 256, 128)) if k <= 2048 else _pick(n, (512, 256, 128))
    w_index = (lambda j: (layer, j, 0)) if w_t else (lambda j: (layer, 0, j))
    return linear(x, xs, [w], w_index, (n // tn,), lambda m: (m, n),
                  lambda tm: (tm, tn), lambda j, i: (i, j), tn, res=res, res_s=res_s,
                  out_dtype=out_dtype, w_t=w_t, name=name)


def ffn_up(x, xs, wg, wu, layer):
    n = wg.shape[2]
    tn = _pick(n, (512, 256, 128))
    return linear(x, xs, [wg, wu], lambda j: (layer, 0, j), (n // tn,), lambda m: (m, n),
                  lambda tm: (tm, tn), lambda j, i: (i, j), tn, swiglu=True, out_dtype=BF16, name="ffn_up")


def attn_q_proj(x, xs, w, layer):
    he = A_HEADS * A_HEAD_DIM
    return linear(x, xs, [w], lambda g: (layer, 0, 3 * g), (N_GROUPS,), lambda m: (m, N_GROUPS * he),
                  lambda tm: (tm, he), lambda g, i: (i, g), he, name="attn_q_proj")


def _kv_proj_kernel(x_ref, xs_ref, w_ref, o_ref, os_ref, f_ref, fs_ref, wb_ref, *,
                    tiles_per_batch, first_kept, keep_blk):
    i = pl.program_id(1)
    nh, e = f_ref.shape[-2:]

    @pl.when(i == 0)
    def _():
        wb_ref[...] = w_ref[...].astype(BF16)
        acc = jnp.dot(xs_ref[...].astype(BF16), wb_ref[...], preferred_element_type=F32)
        os_ref[...] = acc
        for h in range(nh):
            fs_ref[:, h, :] = acc[:, h * e:(h + 1) * e]

    acc = jnp.dot(x_ref[...].astype(BF16), wb_ref[...], preferred_element_type=F32)
    o_ref[...] = acc

    @pl.when(i % tiles_per_batch >= first_kept)
    def _():
        tm = acc.shape[0]
        for h in range(nh):
            f_ref[:, h, :] = acc[tm - keep_blk:, h * e:(h + 1) * e]


def attn_kv_proj(x, xs, w, layer, group, batch, seq, keep):
    m, k = x.shape
    ms = xs.shape[0]
    he = A_HEADS * A_HEAD_DIM
    tm = _pick(seq, (1024, 512, 256, 128))
    keep_blk = min(keep, tm)
    assert keep % keep_blk == 0 and tm % keep_blk == 0
    tpb = seq // tm
    first_kept = tpb - keep // keep_blk if keep >= tm else tpb - 1

    def kept_map(t, i):
        return (i // tpb, jnp.maximum(i % tpb - first_kept, 0), t, 0, 0)

    return pl.pallas_call(
        functools.partial(_kv_proj_kernel, tiles_per_batch=tpb, first_kept=first_kept, keep_blk=keep_blk),
        grid=(2, m // tm),
        in_specs=[pl.BlockSpec((tm, k), lambda t, i: (i, 0)),
                  pl.BlockSpec((ms, k), lambda t, i: (0, 0)),
                  pl.BlockSpec((None, k, he), lambda t, i: (layer, 0, 3 * group + 1 + t))],
        out_specs=[pl.BlockSpec((tm, he), lambda t, i: (i, t)),
                   pl.BlockSpec((ms, he), lambda t, i: (0, t)),
                   pl.BlockSpec((None, keep_blk, None, A_HEADS, A_HEAD_DIM), kept_map),
                   pl.BlockSpec((ms, None, A_HEADS, A_HEAD_DIM), lambda t, i: (0, t, 0, 0))],
        out_shape=[jax.ShapeDtypeStruct((m, 2 * he), F32),
                   jax.ShapeDtypeStruct((ms, 2 * he), F32),
                   jax.ShapeDtypeStruct((batch, keep, 2, A_HEADS, A_HEAD_DIM), F32),
                   jax.ShapeDtypeStruct((ms, 2, A_HEADS, A_HEAD_DIM), F32)],
        scratch_shapes=[pltpu.VMEM((k, he), BF16)],
        compiler_params=_cparams(("parallel", "arbitrary")),
        name="attn_kv_proj",
    )(x, xs, w)


def _gate_kernel(x_ref, xs_ref, w_ref, o_ref, os_ref):
    w = w_ref[...].astype(BF16)
    contract = (((1,), (1,)), ((), ()))

    @pl.when(pl.program_id(0) == 0)
    def _():
        os_ref[...] = lax.dot_general(xs_ref[...], w, contract, preferred_element_type=F32)

    o_ref[...] = lax.dot_general(x_ref[...], w, contract, preferred_element_type=F32)


def gate_proj(x, xs, w_t, layer, row0, n):
    m, k = x.shape
    ms = xs.shape[0]
    assert row0 % n == 0
    tm = _pick(m, (512, 256, 128, 64))
    return pl.pallas_call(
        _gate_kernel,
        grid=(m // tm,),
        in_specs=[pl.BlockSpec((tm, k), lambda i: (i, 0)),
                  pl.BlockSpec((ms, k), lambda i: (0, 0)),
                  pl.BlockSpec((None, n, k), lambda i: (layer, row0 // n, 0))],
        out_specs=[pl.BlockSpec((tm, n), lambda i: (i, 0)), pl.BlockSpec((ms, n), lambda i: (0, 0))],
        out_shape=[jax.ShapeDtypeStruct((m, n), F32), jax.ShapeDtypeStruct((ms, n), F32)],
        compiler_params=_cparams(("arbitrary",)),
        name="gate_proj",
    )(x, xs, w_t)


def _t5_bucket(dist):
    max_exact = NUM_BUCKETS // 2
    d = jnp.maximum(dist.astype(F32), 1.0)
    large = max_exact + jnp.log(d / max_exact) / math.log(MAX_DISTANCE / max_exact) * (NUM_BUCKETS - max_exact)
    large = jnp.minimum(large.astype(jnp.int32), NUM_BUCKETS - 1)
    return jnp.where(dist < max_exact, dist, large)


def _group_bias(rel_bias, g):
    dil = A_GROUPS[g][1]
    buckets = _t5_bucket(jnp.arange(A_STEPS + 1, dtype=jnp.int32) * dil)
    onehot = (buckets[:, None] == jnp.arange(NUM_BUCKETS)[None, :]).astype(F32)
    cols = rel_bias[:, g * A_HEADS:(g + 1) * A_HEADS].astype(F32)
    return jnp.einsum("kb,bh->hk", onehot, cols, precision=lax.Precision.HIGHEST)


def _toeplitz(f, lo, n_rows, n_cols, offset):
    first, last = offset - n_cols + 1, offset + n_rows - 1
    pad_lo, pad_hi = max(lo - first, 0), max(last - (lo + f.shape[-1] - 1), 0)
    fe = jnp.pad(f, [(0, 0)] * (f.ndim - 1) + [(pad_lo, pad_hi)], constant_values=NEG)
    base = lo - pad_lo
    ext = lambda a, b: fe[..., a - base:b - base]
    period = n_rows + n_cols - 1
    w = jnp.concatenate([ext(first, offset + 1)[..., ::-1], ext(offset + 1, last + 1)[..., ::-1]], axis=-1)
    flat = jnp.tile(w, n_rows)[..., :n_rows * (period - 1)]
    return flat.reshape(*f.shape[:-1], n_rows, period - 1)[..., :n_cols]


def _band_tables(biases):
    return jnp.stack([_toeplitz(b, 0, A_BLOCK, 2 * A_BLOCK, A_BLOCK) for b in biases], axis=1)


def _rows(start, size, stride):
    return pl.ds(start, size) if stride == 1 else pl.ds(start, size, stride=stride)


def _masked_logits(q, k, tab, scale):
    s = lax.dot_general(q.astype(BF16), k.astype(BF16), (((1,), (1,)), ((), ())),
                        preferred_element_type=F32) * scale
    return jnp.where(tab > 0.5 * NEG, s + tab, NEG)


def _attn_prompt_kernel(*refs, seq):
    qkv = refs[:3 * N_GROUPS]
    band_ref, y_ref, o_scr, lse_scr = refs[3 * N_GROUPS:]
    scale = A_HEAD_DIM ** -0.5
    q_blk = A_BLOCK

    def block(g, dil, q_start, k_start, n_keys, tab):
        q_ref, k_ref, v_ref = qkv[3 * g:3 * g + 3]
        q = q_ref[_rows(q_start, q_blk, dil), :]
        k = k_ref[_rows(k_start, n_keys, dil), :]
        v = v_ref[_rows(k_start, n_keys, dil), :]
        s = _masked_logits(q, k, tab, scale)
        m = jnp.max(s, axis=-1, keepdims=True)
        p = jnp.exp(s - m)
        l = jnp.sum(p, axis=-1, keepdims=True)
        o = jnp.dot(p.astype(BF16), v.astype(BF16), preferred_element_type=F32)
        rows = _rows(q_start, q_blk, dil)
        o_scr[g, rows, :] = o / l
        lse_scr[g, rows, :] = jnp.broadcast_to(m + jnp.log(l), (q_blk, LANES))

    for g, (_, dil) in enumerate(A_GROUPS):
        sub_len = seq // dil
        nb = sub_len // q_blk

        def residue(r, carry, g=g, dil=dil, nb=nb):
            block(g, dil, r, r, q_blk, band_ref[g, :, q_blk:])

            def later(n, c):
                start = r + dil * q_blk * (n - 1)
                block(g, dil, start + dil * q_blk, start, 2 * q_blk, band_ref[g])
                return c

            if nb > 1:
                lax.fori_loop(1, nb, later, 0, unroll=min(nb - 1, 15))
            return carry

        for r in range(dil):
            residue(r, 0)

    chunk = 256

    def merge(c, carry):
        rows = pl.ds(pl.multiple_of(c * chunk, chunk), chunk)
        lses = [lse_scr[g, rows, :] for g in range(N_GROUPS)]
        top = functools.reduce(jnp.maximum, lses)
        ws = [jnp.exp(lse - top) for lse in lses]
        num = functools.reduce(jnp.add, [w * o_scr[g, rows, :] for g, w in enumerate(ws)])
        y_ref[rows, :] = (num / functools.reduce(jnp.add, ws)).astype(y_ref.dtype)
        return carry

    lax.fori_loop(0, seq // chunk, merge, 0)


def attention_prompt(q, kv, band, batch, seq):
    e = A_HEAD_DIM
    assert all(seq % (dil * A_BLOCK) == 0 for _, dil in A_GROUPS)
    in_specs, args = [], []
    for g in range(N_GROUPS):
        in_specs.append(pl.BlockSpec((seq, e), lambda b, h, g=g: (b, g * A_HEADS + h)))
        in_specs.append(pl.BlockSpec((seq, e), lambda b, h: (b, h)))
        in_specs.append(pl.BlockSpec((seq, e), lambda b, h: (b, A_HEADS + h)))
        args += [q, kv[g], kv[g]]
    in_specs.append(pl.BlockSpec((None, N_GROUPS, A_BLOCK, 2 * A_BLOCK), lambda b, h: (h, 0, 0, 0)))
    return pl.pallas_call(
        functools.partial(_attn_prompt_kernel, seq=seq),
        grid=(batch, A_HEADS),
        in_specs=in_specs,
        out_specs=pl.BlockSpec((seq, e), lambda b, h: (b, h)),
        out_shape=jax.ShapeDtypeStruct((batch * seq, A_HEADS * e), BF16),
        scratch_shapes=[pltpu.VMEM((N_GROUPS, seq, e), F32),
                        pltpu.VMEM((N_GROUPS, seq, LANES), F32)],
        compiler_params=_cparams(("parallel", "parallel")),
        name="attention_prompt",
    )(*args, band)


def _attn_sample_kernel(*refs):
    q_ref = refs[0]
    kv_new = refs[1:1 + N_GROUPS]
    caches = refs[1 + N_GROUPS:1 + 3 * N_GROUPS]
    tables = refs[1 + 3 * N_GROUPS:1 + 5 * N_GROUPS]
    y_ref = refs[1 + 5 * N_GROUPS]
    scale = A_HEAD_DIM ** -0.5
    e = A_HEAD_DIM

    def head_rows(ref, h):
        rows = math.prod(ref.shape[:-2])
        return ref.reshape(rows * A_HEADS, e)[pl.ds(h, rows, stride=A_HEADS), :]

    for h in range(A_HEADS):
        lses, outs = [], []
        for g in range(N_GROUPS):
            q = q_ref[:, (g * A_HEADS + h) * e:(g * A_HEADS + h + 1) * e]
            k_new = kv_new[g][:, h * e:(h + 1) * e]
            v_new = kv_new[g][:, (A_HEADS + h) * e:(A_HEADS + h + 1) * e]
            k_old = head_rows(caches[2 * g], h)
            v_old = head_rows(caches[2 * g + 1], h)
            s_old = _masked_logits(q, k_old, tables[2 * g][h], scale)
            s_new = _masked_logits(q, k_new, tables[2 * g + 1][h], scale)
            m = jnp.maximum(jnp.max(s_old, axis=-1, keepdims=True), jnp.max(s_new, axis=-1, keepdims=True))
            p_old = jnp.exp(s_old - m)
            p_new = jnp.exp(s_new - m)
            l = jnp.sum(p_old, axis=-1, keepdims=True) + jnp.sum(p_new, axis=-1, keepdims=True)
            o = (jnp.dot(p_old.astype(BF16), v_old.astype(BF16), preferred_element_type=F32)
                 + jnp.dot(p_new.astype(BF16), v_new.astype(BF16), preferred_element_type=F32))
            outs.append(o / l)
            lses.append(m + jnp.log(l))
        top = functools.reduce(jnp.maximum, lses)
        ws = [jnp.exp(lse - top) for lse in lses]
        num = functools.reduce(jnp.add, [w * o for w, o in zip(ws, outs)])
        y_ref[:, h * e:(h + 1) * e] = (num / functools.reduce(jnp.add, ws)).astype(y_ref.dtype)


def attention_sample(q, kv_new, caches, layer, biases, batch, t_new):
    he = A_HEADS * A_HEAD_DIM
    in_specs = [pl.BlockSpec((t_new, N_GROUPS * he), lambda b: (b, 0))]
    args = [q]
    for g in range(N_GROUPS):
        in_specs.append(pl.BlockSpec((t_new, 2 * he), lambda b: (b, 0)))
        args.append(kv_new[g])
    tables = []
    for g, (_, dil) in enumerate(A_GROUPS):
        c = caches[g]
        n_past = c.shape[2]
        bg = biases[g]
        by_dist = jnp.concatenate([bg[..., None], jnp.full(bg.shape + (dil - 1,), NEG, F32)], axis=-1)
        by_dist = by_dist.reshape(bg.shape[0], -1)[:, :A_STEPS * dil + 1]
        tab = _toeplitz(by_dist, 0, t_new, n_past + t_new, n_past)
        tab_old, tab_new = tab[:, :, :n_past], tab[:, :, n_past:]
        if dil % t_new == 0 and n_past % dil == 0:
            c = c.reshape(c.shape[0], batch, n_past // dil, dil, 2, A_HEADS, A_HEAD_DIM)
            for t in range(2):
                in_specs.append(pl.BlockSpec((None, None, n_past // dil, t_new, None, A_HEADS, A_HEAD_DIM),
                                             lambda b, t=t: (layer, b, 0, 0, t, 0, 0)))
                args.append(c)
            tab_old = tab_old.reshape(A_HEADS, t_new, n_past // dil, dil)[..., :t_new]
            tab_old = tab_old.reshape(A_HEADS, t_new, (n_past // dil) * t_new)
        else:
            for t in range(2):
                in_specs.append(pl.BlockSpec((None, None, n_past, None, A_HEADS, A_HEAD_DIM),
                                             lambda b, t=t: (layer, b, 0, t, 0, 0)))
                args.append(c)
        tables += [tab_old, tab_new]
    for tab in tables:
        in_specs.append(pl.BlockSpec(tab.shape, lambda b: (0, 0, 0)))
        args.append(tab)
    return pl.pallas_call(
        _attn_sample_kernel,
        grid=(batch,),
        in_specs=in_specs,
        out_specs=pl.BlockSpec((t_new, he), lambda b: (b, 0)),
        out_shape=jax.ShapeDtypeStruct((batch * t_new, he), F32),
        compiler_params=_cparams(("parallel",)),
        name="attention_sample",
    )(*args)


def _log_sigmoid(x):
    return jnp.minimum(x, 0.0) - jnp.log1p(jnp.exp(-jnp.abs(x)))


def _mlstm_kernel(q_ref, k_ref, v_ref, og_ref, gc_ref, gr_ref, bc_ref, br_ref, gn_ref,
                  c0_ref, n0_ref, m0_ref,
                  h_ref, c_out, n_out, m_out,
                  c_scr, n_scr, m_scr, *, chunk):
    c = pl.program_id(1)
    nh, dk, dv = c_scr.shape

    @pl.when(c == 0)
    def _():
        c_scr[...] = c0_ref[...].astype(F32)
        n_scr[...] = n0_ref[...].astype(F32)
        m_scr[...] = m0_ref[...].astype(F32)

    gc = gc_ref[...] + bc_ref[...]
    gr = gr_ref[...] + br_ref[...]
    logf_c = _log_sigmoid(gc[:, nh:])
    logf_r = _log_sigmoid(gr[nh:, :])
    qi = lax.broadcasted_iota(jnp.int32, (chunk, chunk), 0)
    si = lax.broadcasted_iota(jnp.int32, (chunk, chunk), 1)
    causal = si <= qi
    upper = qi <= si
    m_all, n_all = m_scr[...], n_scr[...]
    m_news, n_news = [], []

    for h in range(nh):
        i_col, i_row = gc[:, h:h + 1], gr[h:h + 1, :]
        logf_col, logf_row = logf_c[:, h:h + 1], logf_r[h:h + 1, :]
        b_col = jnp.sum(jnp.where(causal, logf_row, 0.0), axis=1, keepdims=True)
        b_row = jnp.sum(jnp.where(upper, logf_col, 0.0), axis=0, keepdims=True)
        b_last = jnp.sum(logf_row, axis=1, keepdims=True)

        m_prev = m_all[h:h + 1, :]
        a = b_col + m_prev
        dmat = jnp.where(causal, b_col - b_row + i_row, NEG)
        m_t = jnp.maximum(a, jnp.max(dmat, axis=1, keepdims=True))
        w_inter = jnp.exp(a - m_t)

        qf = q_ref[:, h * dk:(h + 1) * dk].astype(F32)
        q = qf.astype(BF16)
        kf = k_ref[:, h * dk:(h + 1) * dk].astype(F32) * (dk ** -0.5)
        k = kf.astype(BF16)
        v = v_ref[:, h * dv:(h + 1) * dv].astype(BF16)
        c_prev = c_scr[h]
        n_prev = n_all[h:h + 1, :]
        s = lax.dot_general(q, k, (((1,), (1,)), ((), ())), preferred_element_type=F32) * jnp.exp(dmat - m_t)
        num = (w_inter * jnp.dot(q, c_prev.astype(BF16), preferred_element_type=F32)
               + jnp.dot(s.astype(BF16), v, preferred_element_type=F32))
        qn = jnp.sum(qf * n_prev, axis=1, keepdims=True)
        den = w_inter * qn + jnp.sum(s, axis=1, keepdims=True)
        hh = num / jnp.maximum(jnp.abs(den), jnp.exp(-m_t))

        m_new = jnp.maximum(b_last + m_prev, jnp.max(b_last - b_row + i_row, axis=1, keepdims=True))
        w_state = jnp.exp(b_last - b_col + i_col - m_new)
        decay = jnp.exp(b_last + m_prev - m_new)
        kw = (w_state * kf).astype(BF16)
        c_scr[h] = decay * c_prev + lax.dot_general(kw, v, (((0,), (0,)), ((), ())),
                                                    preferred_element_type=F32)
        n_news.append(decay * n_prev + jnp.sum(w_state * kf, axis=0, keepdims=True))
        m_news.append(m_new)

        ht = hh * jax.nn.sigmoid(og_ref[:, h * dv:(h + 1) * dv].astype(F32))
        hn = ht * lax.rsqrt(jnp.mean(ht * ht, axis=-1, keepdims=True) + EPS)
        h_ref[:, h * dv:(h + 1) * dv] = (hn * gn_ref[:, h * dv:(h + 1) * dv]).astype(h_ref.dtype)

    n_scr[...] = jnp.concatenate(n_news, axis=0)
    m_scr[...] = jnp.concatenate(m_news, axis=0)

    @pl.when(c == pl.num_programs(1) - 1)
    def _():
        c_out[...] = c_scr[...].astype(c_out.dtype)
        n_out[...] = n_scr[...].astype(n_out.dtype)
        m_out[...] = m_scr[...].astype(m_out.dtype)


def mlstm(z, gates, b_gate, g_norm, state, batch, seq, chunk, out_dtype):
    c0, n0, m0 = state
    nh = B_HEADS
    dk, dv = c0.shape[-2], c0.shape[-1]
    nc = seq // chunk
    qk_w, v_w = nh * dk, nh * dv
    assert v_w % qk_w == 0
    gcol = gates.reshape(batch * nc, chunk, 2 * nh)
    grow = jnp.swapaxes(gcol, 1, 2)
    row = lambda b, c: b * nc + c
    state_specs = [
        pl.BlockSpec((None, nh, dk, dv), lambda b, c: (b, 0, 0, 0)),
        pl.BlockSpec((None, nh, dk), lambda b, c: (b, 0, 0)),
        pl.BlockSpec((None, nh, 1), lambda b, c: (b, 0, 0)),
    ]
    in_specs = [
        pl.BlockSpec((chunk, qk_w), lambda b, c: (row(b, c), 0)),
        pl.BlockSpec((chunk, qk_w), lambda b, c: (row(b, c), 1)),
        pl.BlockSpec((chunk, v_w), lambda b, c: (row(b, c), 2 * qk_w // v_w)),
        pl.BlockSpec((chunk, v_w), lambda b, c: (row(b, c), 2 * qk_w // v_w + 1)),
        pl.BlockSpec((None, chunk, 2 * nh), lambda b, c: (row(b, c), 0, 0)),
        pl.BlockSpec((None, 2 * nh, chunk), lambda b, c: (row(b, c), 0, 0)),
        pl.BlockSpec((1, 2 * nh), lambda b, c: (0, 0)),
        pl.BlockSpec((2 * nh, 1), lambda b, c: (0, 0)),
        pl.BlockSpec((1, v_w), lambda b, c: (0, 0)),
    ] + state_specs
    out_shape = [
        jax.ShapeDtypeStruct((batch * seq, v_w), out_dtype),
        jax.ShapeDtypeStruct((batch, nh, dk, dv), F32),
        jax.ShapeDtypeStruct((batch, nh, dk), F32),
        jax.ShapeDtypeStruct((batch, nh, 1), F32),
    ]
    bg = b_gate.astype(F32)
    hn, c_new, n_new, m_new = pl.pallas_call(
        functools.partial(_mlstm_kernel, chunk=chunk),
        grid=(batch, nc),
        in_specs=in_specs,
        out_specs=[pl.BlockSpec((chunk, v_w), lambda b, c: (row(b, c), 0))] + state_specs,
        out_shape=out_shape,
        scratch_shapes=[pltpu.VMEM((nh, dk, dv), F32), pltpu.VMEM((nh, dk), F32), pltpu.VMEM((nh, 1), F32)],
        compiler_params=_cparams(("parallel", "arbitrary")),
        name="mlstm",
    )(z, z, z, z, gcol, grow, bg.reshape(1, 2 * nh), bg.reshape(2 * nh, 1),
      g_norm.astype(F32).reshape(1, v_w),
      c0.reshape(batch, nh, dk, dv), n0.reshape(batch, nh, dk), m0.reshape(batch, nh, 1))
    return hn, (c_new, n_new, m_new.reshape(batch, nh))


def kernel(x_prompt, x_sample, cache_kv_w128, cache_kv_w512, cache_kv_w2048, state_C, state_n, state_m,
           norm_mix, norm_ffn, norm_final, rel_bias, w_a_in, w_a_out, w_b_in, b_b_gate, b_b_norm, w_b_out,
           w_ffn_gate, w_ffn_up, w_ffn_down):
    bp, seq, d_model = x_prompt.shape
    bs, t_new, _ = x_sample.shape
    depth = norm_mix.shape[0]
    nh, dk, dv = state_C.shape[2], state_C.shape[3], state_C.shape[4]
    main_w = 2 * nh * dk + 2 * nh * dv
    caches = (cache_kv_w128, cache_kv_w512, cache_kv_w2048)
    w_b_in_t = jnp.swapaxes(w_b_in, 1, 2)

    biases =[_group_bias(rel_bias, g) for g in range(N_GROUPS)]
    band = _band_tables(biases)

    xp = x_prompt.reshape(bp * seq, d_model)
    xs = x_sample.reshape(bs * t_new, d_model)
    kv_p = [[] for _ in range(N_GROUPS)]
    kv_s = [[] for _ in range(N_GROUPS)]
    st_p, st_s = [], []
    for layer in range(depth):
        j = layer // 2
        hp, hs = rmsnorm(xp, xs, norm_mix[layer], BF16)
        if layer % 2 == 0:
            qp, qs = attn_q_proj(hp, hs, w_a_in, j)
            kvp, kvs = [], []
            for g, (window, _) in enumerate(A_GROUPS):
                kp, ks, fp, fs = attn_kv_proj(hp, hs, w_a_in, j, g, bp, seq, min(window, seq))
                kvp.append(kp)
                kvs.append(ks)
                kv_p[g].append(fp)
                kv_s[g].append(fs.reshape(bs, t_new, 2, A_HEADS, A_HEAD_DIM))
            yp = attention_prompt(qp, kvp, band, bp, seq)
            ys = attention_sample(qs, kvs, caches, j, biases, bs, t_new)
            xp, xs = dense(yp, ys, w_a_out, j, res=xp, res_s=xs, name="attn_out_proj")
        else:
            zero = (jnp.zeros((bp, nh, dk, dv), F32), jnp.zeros((bp, nh, dk), F32), jnp.zeros((bp, nh), F32))
            past = (state_C[j], state_n[j], state_m[j])
            zp, zs = dense(hp, hs, w_b_in_t, j, n_cols=main_w, w_t=True, name="mlstm_in_proj")
            gp, gs = gate_proj(hp, hs, w_b_in_t, j, main_w, 2 * nh)
            hnp, sp = mlstm(zp, gp, b_b_gate[j], b_b_norm[j], zero, bp, seq, min(PROMPT_CHUNK, seq), BF16)
            hns, ss = mlstm(zs, gs, b_b_gate[j], b_b_norm[j], past, bs, t_new, t_new, F32)
            xp, xs = dense(hnp, hns, w_b_out, j, res=xp, res_s=xs, name="mlstm_out_proj")
            st_p.append(sp)
            st_s.append(ss)
        hp, hs = rmsnorm(xp, xs, norm_ffn[layer], BF16)
        ap, as_ = ffn_up(hp, hs, w_ffn_gate, w_ffn_up, layer)
        xp, xs = dense(ap, as_, w_ffn_down, layer, res=xp, res_s=xs, name="ffn_down")

    y_prompt, y_sample = rmsnorm(xp, xs, norm_final, F32)
    y_prompt = y_prompt.reshape(bp, seq, d_model)
    y_sample = y_sample.reshape(bs, t_new, d_model)
    kvd = cache_kv_w128.dtype
    sd = state_C.dtype
    kv_out_p = [jnp.stack(kv_p[g]).astype(kvd) for g in range(N_GROUPS)]
    kv_out_s = [jnp.stack(kv_s[g]).astype(kvd) for g in range(N_GROUPS)]
    states_p = [jnp.stack([s[i] for s in st_p]).astype(sd) for i in range(3)]
    states_s = [jnp.stack([s[i] for s in st_s]).astype(sd) for i in range(3)]
    return (y_prompt, y_sample, *kv_out_p, *kv_out_s, *states_p, *states_s)
```

```python
import functools
import math

import jax
import jax.numpy as jnp
from jax import lax
from jax.experimental import pallas as pl
from jax.experimental.pallas import tpu as pltpu

F32 = jnp.float32
BF16 = jnp.bfloat16

EPS = 1e-6
NEG = -1e30

A_GROUPS = ((128, 1), (512, 4), (2048, 16))
N_GROUPS = len(A_GROUPS)
A_HEADS = 8
A_HEAD_DIM = 128
A_STEPS = 128
A_BLOCK = 128
NUM_BUCKETS = 32
MAX_DISTANCE = 2048
B_HEADS = 8
LANES = 128
VMEM_LIMIT = 56 * 1024 * 1024
PROMPT_CHUNK = 256
HEADS_AHEAD = 1
LOGITS_AHEAD = 3


def _cparams(sem):
    return pltpu.CompilerParams(dimension_semantics=sem, vmem_limit_bytes=VMEM_LIMIT)


def _pick(n, cands):
    for c in cands:
        if n % c == 0:
            return c
    return n


def _row_tile(m, fixed_bytes, block_bytes_per_row, temp_bytes_per_row):
    for tm in (2048, 1024, 512, 256, 128, 64):
        if m % tm == 0 and fixed_bytes + tm * (2 * block_bytes_per_row + temp_bytes_per_row) <= VMEM_LIMIT:
            return tm
    return m


def _rmsnorm_kernel(x_ref, xs_ref, g_ref, o_ref, os_ref):
    def norm(x):
        return x * lax.rsqrt(jnp.mean(x * x, axis=-1, keepdims=True) + EPS) * g_ref[...]

    @pl.when(pl.program_id(0) == 0)
    def _():
        os_ref[...] = norm(xs_ref[...]).astype(os_ref.dtype)

    o_ref[...] = norm(x_ref[...]).astype(o_ref.dtype)


def rmsnorm(x, xs, g, out_dtype):
    m, d = x.shape
    ms = xs.shape[0]
    tm = _pick(m, (512, 256, 128, 64))
    return pl.pallas_call(
        _rmsnorm_kernel,
        grid=(m // tm,),
        in_specs=[pl.BlockSpec((tm, d), lambda i: (i, 0)),
                  pl.BlockSpec((ms, d), lambda i: (0, 0)),
                  pl.BlockSpec((1, d), lambda i: (0, 0))],
        out_specs=[pl.BlockSpec((tm, d), lambda i: (i, 0)),
                   pl.BlockSpec((ms, d), lambda i: (0, 0))],
        out_shape=[jax.ShapeDtypeStruct((m, d), out_dtype), jax.ShapeDtypeStruct((ms, d), out_dtype)],
        compiler_params=_cparams(("arbitrary",)),
        name="rmsnorm",
    )(x, xs, g.reshape(1, d))


def _linear_kernel(*refs, n_w, has_res, swiglu, m_axis, w_t):
    x_ref, xs_ref = refs[:2]
    w_refs = refs[2:2 + n_w]
    p = 2 + n_w
    r_ref, rs_ref = (refs[p], refs[p + 1]) if has_res else (None, None)
    p += 2 * has_res
    o_ref, os_ref = refs[p], refs[p + 1]
    wb_refs = refs[p + 2:]
    contract = (((1,), (1 if w_t else 0,)), ((), ()))

    def apply(x_ref, r_ref, o_ref):
        x = x_ref[...].astype(BF16)
        accs = [lax.dot_general(x, wb_ref[...], contract, preferred_element_type=F32) for wb_ref in wb_refs]
        out = accs[0] * jax.nn.sigmoid(accs[0]) * accs[1] if swiglu else accs[0]
        if has_res:
            out = r_ref[...] + out
        o_ref[...] = out.astype(o_ref.dtype)

    @pl.when(pl.program_id(m_axis) == 0)
    def _():
        for w_ref, wb_ref in zip(w_refs, wb_refs):
            wb_ref[...] = w_ref[...].astype(BF16)
        apply(xs_ref, rs_ref, os_ref)

    apply(x_ref, r_ref, o_ref)


def linear(x, xs, ws, w_index, outer, out_shape, out_block, out_index, tn, *, res=None, res_s=None,
           swiglu=False, out_dtype=F32, w_t=False, name="linear"):
    m, k = x.shape
    ms = xs.shape[0]
    nd = len(outer)
    has_res = res is not None
    x_bytes, o_bytes = x.dtype.itemsize, jnp.dtype(out_dtype).itemsize
    row_bytes = k * x_bytes + tn * (o_bytes + (res.dtype.itemsize if has_res else 0))
    w_bytes = len(ws) * k * tn * (2 * ws[0].dtype.itemsize + jnp.dtype(BF16).itemsize)
    acc_bytes = len(ws) * tn * jnp.dtype(F32).itemsize
    tm = _row_tile(m, w_bytes + 2 * ms * row_bytes, row_bytes, acc_bytes)
    w_block = (None, tn, k) if w_t else (None, k, tn)
    main_map = lambda *g: out_index(*g)
    sample_map = lambda *g: out_index(*g[:nd], 0)
    in_specs = [pl.BlockSpec((tm, k), lambda *g: (g[nd], 0)),
                pl.BlockSpec((ms, k), lambda *g: (0, 0))]
    in_specs += [pl.BlockSpec(w_block, lambda *g: w_index(*g[:nd])) for _ in ws]
    args = [x, xs, *ws]
    if has_res:
        in_specs += [pl.BlockSpec(out_block(tm), main_map), pl.BlockSpec(out_block(ms), sample_map)]
        args += [res, res_s]
    return pl.pallas_call(
        functools.partial(_linear_kernel, n_w=len(ws), has_res=has_res, swiglu=swiglu, m_axis=nd, w_t=w_t),
        grid=(*outer, m // tm),
        in_specs=in_specs,
        out_specs=[pl.BlockSpec(out_block(tm), main_map), pl.BlockSpec(out_block(ms), sample_map)],
        out_shape=[jax.ShapeDtypeStruct(out_shape(m), out_dtype), jax.ShapeDtypeStruct(out_shape(ms), out_dtype)],
        scratch_shapes=[pltpu.VMEM(w_block[1:], BF16) for _ in ws],
        compiler_params=_cparams(("parallel",) * nd + ("arbitrary",)),
        name=name,
    )(*args)


def dense(x, xs, w, layer, *, res=None, res_s=None, out_dtype=F32, n_cols=None, w_t=False, name="dense"):
    k = x.shape[1]
    n = n_cols or w.shape[1 if w_t else 2]
    tn = _pick(n, (1024, 512, 256, 128)) if k <= 2048 else _pick(n, (512, 256, 128))
    w_index = (lambda j: (layer, j, 0)) if w_t else (lambda j: (layer, 0, j))
    return linear(x, xs, [w], w_index, (n // tn,), lambda m: (m, n),
                  lambda tm: (tm, tn), lambda j, i: (i, j), tn, res=res, res_s=res_s,
                  out_dtype=out_dtype, w_t=w_t, name=name)


def ffn_up(x, xs, wg, wu, layer):
    n = wg.shape[2]
    tn = _pick(n, (512, 256, 128))
    return linear(x, xs, [wg, wu], lambda j: (layer, 0, j), (n // tn,), lambda m: (m, n),
                  lambda tm: (tm, tn), lambda j, i: (i, j), tn, swiglu=True, out_dtype=BF16, name="ffn_up")


def attn_q_proj(x, xs, w, layer):
    he = A_HEADS * A_HEAD_DIM
    return linear(x, xs, [w], lambda g: (layer, 0, 3 * g), (N_GROUPS,), lambda m: (m, N_GROUPS * he),
                  lambda tm: (tm, he), lambda g, i: (i, g), he, name="attn_q_proj")


def _kv_proj_kernel(x_ref, xs_ref, w_ref, o_ref, os_ref, f_ref, fs_ref, wb_ref, *,
                    tiles_per_batch, first_kept, keep_blk):
    i = pl.program_id(1)
    nh, e = f_ref.shape[-2:]

    def split_heads(dst_ref, rows):
        flat = dst_ref.reshape(rows.shape[0] * nh, e)
        for h in range(nh):
            flat[pl.ds(h, rows.shape[0], stride=nh), :] = rows[:, h * e:(h + 1) * e]

    @pl.when(i == 0)
    def _():
        wb_ref[...] = w_ref[...].astype(BF16)
        acc = jnp.dot(xs_ref[...].astype(BF16), wb_ref[...], preferred_element_type=F32)
        os_ref[...] = acc
        split_heads(fs_ref, acc)

    acc = jnp.dot(x_ref[...].astype(BF16), wb_ref[...], preferred_element_type=F32)
    o_ref[...] = acc

    @pl.when(i % tiles_per_batch >= first_kept)
    def _():
        split_heads(f_ref, acc[acc.shape[0] - keep_blk:])


def attn_kv_proj(x, xs, w, layer, group, batch, seq, keep):
    m, k = x.shape
    ms = xs.shape[0]
    he = A_HEADS * A_HEAD_DIM
    tm = _pick(seq, (1024, 512, 256, 128))
    keep_blk = min(keep, tm)
    assert keep % keep_blk == 0 and tm % keep_blk == 0
    tpb = seq // tm
    first_kept = tpb - keep // keep_blk if keep >= tm else tpb - 1

    def kept_map(t, i):
        return (i // tpb, jnp.maximum(i % tpb - first_kept, 0), t, 0, 0)

    return pl.pallas_call(
        functools.partial(_kv_proj_kernel, tiles_per_batch=tpb, first_kept=first_kept, keep_blk=keep_blk),
        grid=(2, m // tm),
        in_specs=[pl.BlockSpec((tm, k), lambda t, i: (i, 0)),
                  pl.BlockSpec((ms, k), lambda t, i: (0, 0)),
                  pl.BlockSpec((None, k, he), lambda t, i: (layer, 0, 3 * group + 1 + t))],
        out_specs=[pl.BlockSpec((tm, he), lambda t, i: (i, t)),
                   pl.BlockSpec((ms, he), lambda t, i: (0, t)),
                   pl.BlockSpec((None, keep_blk, None, A_HEADS, A_HEAD_DIM), kept_map),
                   pl.BlockSpec((ms, None, A_HEADS, A_HEAD_DIM), lambda t, i: (0, t, 0, 0))],
        out_shape=[jax.ShapeDtypeStruct((m, 2 * he), F32),
                   jax.ShapeDtypeStruct((ms, 2 * he), F32),
                   jax.ShapeDtypeStruct((batch, keep, 2, A_HEADS, A_HEAD_DIM), F32),
                   jax.ShapeDtypeStruct((ms, 2, A_HEADS, A_HEAD_DIM), F32)],
        scratch_shapes=[pltpu.VMEM((k, he), BF16)],
        compiler_params=_cparams(("parallel", "arbitrary")),
        name="attn_kv_proj",
    )(x, xs, w)


def _gate_kernel(x_ref, xs_ref, w_ref, o_ref, os_ref):
    w = w_ref[...].astype(BF16)
    contract = (((1,), (1,)), ((), ()))

    @pl.when(pl.program_id(0) == 0)
    def _():
        os_ref[...] = lax.dot_general(xs_ref[...], w, contract, preferred_element_type=F32)

    o_ref[...] = lax.dot_general(x_ref[...], w, contract, preferred_element_type=F32)


def gate_proj(x, xs, w_t, layer, row0, n):
    m, k = x.shape
    ms = xs.shape[0]
    assert row0 % n == 0
    tm = _pick(m, (512, 256, 128, 64))
    return pl.pallas_call(
        _gate_kernel,
        grid=(m // tm,),
        in_specs=[pl.BlockSpec((tm, k), lambda i: (i, 0)),
                  pl.BlockSpec((ms, k), lambda i: (0, 0)),
                  pl.BlockSpec((None, n, k), lambda i: (layer, row0 // n, 0))],
        out_specs=[pl.BlockSpec((tm, n), lambda i: (i, 0)), pl.BlockSpec((ms, n), lambda i: (0, 0))],
        out_shape=[jax.ShapeDtypeStruct((m, n), F32), jax.ShapeDtypeStruct((ms, n), F32)],
        compiler_params=_cparams(("arbitrary",)),
        name="gate_proj",
    )(x, xs, w_t)


def _t5_bucket(dist):
    max_exact = NUM_BUCKETS // 2
    d = jnp.maximum(dist.astype(F32), 1.0)
    large = max_exact + jnp.log(d / max_exact) / math.log(MAX_DISTANCE / max_exact) * (NUM_BUCKETS - max_exact)
    large = jnp.minimum(large.astype(jnp.int32), NUM_BUCKETS - 1)
    return jnp.where(dist < max_exact, dist, large)


def _group_bias(rel_bias, g):
    dil = A_GROUPS[g][1]
    buckets = _t5_bucket(jnp.arange(A_STEPS + 1, dtype=jnp.int32) * dil)
    onehot = (buckets[:, None] == jnp.arange(NUM_BUCKETS)[None, :]).astype(F32)
    cols = rel_bias[:, g * A_HEADS:(g + 1) * A_HEADS].astype(F32)
    return jnp.einsum("kb,bh->hk", onehot, cols, precision=lax.Precision.HIGHEST)


def _toeplitz(f, lo, n_rows, n_cols, offset):
    first, last = offset - n_cols + 1, offset + n_rows - 1
    pad_lo, pad_hi = max(lo - first, 0), max(last - (lo + f.shape[-1] - 1), 0)
    fe = jnp.pad(f, [(0, 0)] * (f.ndim - 1) + [(pad_lo, pad_hi)], constant_values=NEG)
    base = lo - pad_lo
    ext = lambda a, b: fe[..., a - base:b - base]
    period = n_rows + n_cols - 1
    w = jnp.concatenate([ext(first, offset + 1)[..., ::-1], ext(offset + 1, last + 1)[..., ::-1]], axis=-1)
    flat = jnp.tile(w, n_rows)[..., :n_rows * (period - 1)]
    return flat.reshape(*f.shape[:-1], n_rows, period - 1)[..., :n_cols]


def _band_tables(biases):
    return jnp.stack([_toeplitz(b, 0, A_BLOCK, 2 * A_BLOCK, A_BLOCK) for b in biases], axis=1)


def _rows(start, size, stride):
    return pl.ds(start, size) if stride == 1 else pl.ds(start, size, stride=stride)


def _masked_logits(q, k, tab, scale):
    s = lax.dot_general(q.astype(BF16), k.astype(BF16), (((1,), (1,)), ((), ())),
                        preferred_element_type=F32) * scale
    return jnp.where(tab > 0.5 * NEG, s + tab, NEG)


def _attn_prompt_kernel(*refs, seq):
    qkv = refs[:3 * N_GROUPS]
    band_ref, y_ref, o_scr, lse_scr = refs[3 * N_GROUPS:]
    scale = A_HEAD_DIM ** -0.5
    q_blk = A_BLOCK

    def logits(g, dil, q_start, k_start, n_keys):
        q_ref, k_ref, v_ref = qkv[3 * g:3 * g + 3]
        q = q_ref[_rows(q_start, q_blk, dil), :]
        k = k_ref[_rows(k_start, n_keys, dil), :]
        v = v_ref[_rows(k_start, n_keys, dil), :]
        tab = band_ref[g] if n_keys == 2 * q_blk else band_ref[g, :, q_blk:]
        return _masked_logits(q, k, tab, scale), v.astype(BF16)

    def finish(g, dil, q_start, s, v):
        m = jnp.max(s, axis=-1, keepdims=True)
        p = jnp.exp(s - m)
        l = jnp.sum(p, axis=-1, keepdims=True)
        o = jnp.dot(p.astype(BF16), v, preferred_element_type=F32)
        rows = _rows(q_start, q_blk, dil)
        o_scr[g, rows, :] = o / l
        lse_scr[g, rows, :] = jnp.broadcast_to(m + jnp.log(l), (q_blk, LANES))

    blocks = []
    for g, (_, dil) in enumerate(A_GROUPS):
        for r in range(dil):
            blocks.append((g, dil, r, r, q_blk))
            for n in range(1, seq // dil // q_blk):
                start = r + dil * q_blk * (n - 1)
                blocks.append((g, dil, start + dil * q_blk, start, 2 * q_blk))
    pending = []
    for g, dil, q_start, k_start, n_keys in blocks:
        s, v = logits(g, dil, q_start, k_start, n_keys)
        pending.append((g, dil, q_start, s, v))
        if len(pending) > LOGITS_AHEAD:
            finish(*pending.pop(0))
    for item in pending:
        finish(*item)

    chunk = 256

    def merge(c, carry):
        rows = pl.ds(pl.multiple_of(c * chunk, chunk), chunk)
        lses = [lse_scr[g, rows, :] for g in range(N_GROUPS)]
        top = functools.reduce(jnp.maximum, lses)
        ws = [jnp.exp(lse - top) for lse in lses]
        num = functools.reduce(jnp.add, [w * o_scr[g, rows, :] for g, w in enumerate(ws)])
        y_ref[rows, :] = (num / functools.reduce(jnp.add, ws)).astype(y_ref.dtype)
        return carry

    lax.fori_loop(0, seq // chunk, merge, 0)


def attention_prompt(q, kv, band, batch, seq):
    e = A_HEAD_DIM
    assert all(seq % (dil * A_BLOCK) == 0 for _, dil in A_GROUPS)
    in_specs, args = [], []
    for g in range(N_GROUPS):
        in_specs.append(pl.BlockSpec((seq, e), lambda b, h, g=g: (b, g * A_HEADS + h)))
        in_specs.append(pl.BlockSpec((seq, e), lambda b, h: (b, h)))
        in_specs.append(pl.BlockSpec((seq, e), lambda b, h: (b, A_HEADS + h)))
        args += [q, kv[g], kv[g]]
    in_specs.append(pl.BlockSpec((None, N_GROUPS, A_BLOCK, 2 * A_BLOCK), lambda b, h: (h, 0, 0, 0)))
    return pl.pallas_call(
        functools.partial(_attn_prompt_kernel, seq=seq),
        grid=(batch, A_HEADS),
        in_specs=in_specs,
        out_specs=pl.BlockSpec((seq, e), lambda b, h: (b, h)),
        out_shape=jax.ShapeDtypeStruct((batch * seq, A_HEADS * e), BF16),
        scratch_shapes=[pltpu.VMEM((N_GROUPS, seq, e), F32),
                        pltpu.VMEM((N_GROUPS, seq, LANES), F32)],
        compiler_params=_cparams(("parallel", "parallel")),
        name="attention_prompt",
    )(*args, band)


def _attn_sample_kernel(*refs):
    q_ref = refs[0]
    kv_new = refs[1:1 + N_GROUPS]
    caches = refs[1 + N_GROUPS:1 + 3 * N_GROUPS]
    tables = refs[1 + 3 * N_GROUPS:1 + 5 * N_GROUPS]
    y_ref = refs[1 + 5 * N_GROUPS]
    scale = A_HEAD_DIM ** -0.5
    e = A_HEAD_DIM

    def head_rows(ref, h):
        rows = math.prod(ref.shape[:-2])
        return ref.reshape(rows * A_HEADS, e)[pl.ds(h, rows, stride=A_HEADS), :]

    for h in range(A_HEADS):
        lses, outs = [], []
        for g in range(N_GROUPS):
            q = q_ref[:, (g * A_HEADS + h) * e:(g * A_HEADS + h + 1) * e]
            k_new = kv_new[g][:, h * e:(h + 1) * e]
            v_new = kv_new[g][:, (A_HEADS + h) * e:(A_HEADS + h + 1) * e]
            k_old = head_rows(caches[2 * g], h)
            v_old = head_rows(caches[2 * g + 1], h)
            s_old = _masked_logits(q, k_old, tables[2 * g][h], scale)
            s_new = _masked_logits(q, k_new, tables[2 * g + 1][h], scale)
            m = jnp.maximum(jnp.max(s_old, axis=-1, keepdims=True), jnp.max(s_new, axis=-1, keepdims=True))
            p_old = jnp.exp(s_old - m)
            p_new = jnp.exp(s_new - m)
            l = jnp.sum(p_old, axis=-1, keepdims=True) + jnp.sum(p_new, axis=-1, keepdims=True)
            o = (jnp.dot(p_old.astype(BF16), v_old.astype(BF16), preferred_element_type=F32)
                 + jnp.dot(p_new.astype(BF16), v_new.astype(BF16), preferred_element_type=F32))
            outs.append(o / l)
            lses.append(m + jnp.log(l))
        top = functools.reduce(jnp.maximum, lses)
        ws = [jnp.exp(lse - top) for lse in lses]
        num = functools.reduce(jnp.add, [w * o for w, o in zip(ws, outs)])
        y_ref[:, h * e:(h + 1) * e] = (num / functools.reduce(jnp.add, ws)).astype(y_ref.dtype)


def attention_sample(q, kv_new, caches, layer, biases, batch, t_new):
    he = A_HEADS * A_HEAD_DIM
    in_specs = [pl.BlockSpec((t_new, N_GROUPS * he), lambda b: (b, 0))]
    args = [q]
    for g in range(N_GROUPS):
        in_specs.append(pl.BlockSpec((t_new, 2 * he), lambda b: (b, 0)))
        args.append(kv_new[g])
    tables = []
    for g, (_, dil) in enumerate(A_GROUPS):
        c = caches[g]
        n_past = c.shape[2]
        bg = biases[g]
        by_dist = jnp.concatenate([bg[..., None], jnp.full(bg.shape + (dil - 1,), NEG, F32)], axis=-1)
        by_dist = by_dist.reshape(bg.shape[0], -1)[:, :A_STEPS * dil + 1]
        tab = _toeplitz(by_dist, 0, t_new, n_past + t_new, n_past)
        tab_old, tab_new = tab[:, :, :n_past], tab[:, :, n_past:]
        if dil % t_new == 0 and n_past % dil == 0:
            c = c.reshape(c.shape[0], batch, n_past // dil, dil, 2, A_HEADS, A_HEAD_DIM)
            for t in range(2):
                in_specs.append(pl.BlockSpec((None, None, n_past // dil, t_new, None, A_HEADS, A_HEAD_DIM),
                                             lambda b, t=t: (layer, b, 0, 0, t, 0, 0)))
                args.append(c)
            tab_old = tab_old.reshape(A_HEADS, t_new, n_past // dil, dil)[..., :t_new]
            tab_old = tab_old.reshape(A_HEADS, t_new, (n_past // dil) * t_new)
        else:
            for t in range(2):
                in_specs.append(pl.BlockSpec((None, None, n_past, None, A_HEADS, A_HEAD_DIM),
                                             lambda b, t=t: (layer, b, 0, t, 0, 0)))
                args.append(c)
        tables += [tab_old, tab_new]
    for tab in tables:
        in_specs.append(pl.BlockSpec(tab.shape, lambda b: (0, 0, 0)))
        args.append(tab)
    return pl.pallas_call(
        _attn_sample_kernel,
        grid=(batch,),
        in_specs=in_specs,
        out_specs=pl.BlockSpec((t_new, he), lambda b: (b, 0)),
        out_shape=jax.ShapeDtypeStruct((batch * t_new, he), F32),
        compiler_params=_cparams(("parallel",)),
        name="attention_sample",
    )(*args)


def _log_sigmoid(x):
    return jnp.minimum(x, 0.0) - jnp.log1p(jnp.exp(-jnp.abs(x)))


def _mlstm_kernel(q_ref, k_ref, v_ref, og_ref, gc_ref, gr_ref, bc_ref, br_ref, gn_ref,
                  c0_ref, n0_ref, m0_ref,
                  h_ref, c_out, n_out, m_out,
                  c_scr, n_scr, m_scr, *, chunk):
    c = pl.program_id(1)
    nh, dk, dv = c_scr.shape

    @pl.when(c == 0)
    def _():
        c_scr[...] = c0_ref[...].astype(F32)
        n_scr[...] = n0_ref[...].astype(F32)
        m_scr[...] = m0_ref[...].astype(F32)

    gc = gc_ref[...] + bc_ref[...]
    gr = gr_ref[...] + br_ref[...]
    logf_c = _log_sigmoid(gc[:, nh:])
    logf_r = _log_sigmoid(gr[nh:, :])
    qi = lax.broadcasted_iota(jnp.int32, (chunk, chunk), 0)
    si = lax.broadcasted_iota(jnp.int32, (chunk, chunk), 1)
    causal = si <= qi
    upper = qi <= si
    m_all, n_all = m_scr[...], n_scr[...]
    m_news, n_news = [], []

    def state_stage(h):
        i_col, i_row = gc[:, h:h + 1], gr[h:h + 1, :]
        logf_col, logf_row = logf_c[:, h:h + 1], logf_r[h:h + 1, :]
        b_col = jnp.sum(jnp.where(causal, logf_row, 0.0), axis=1, keepdims=True)
        b_row = jnp.sum(jnp.where(upper, logf_col, 0.0), axis=0, keepdims=True)
        b_last = jnp.sum(logf_row, axis=1, keepdims=True)

        m_prev = m_all[h:h + 1, :]
        a = b_col + m_prev
        dmat = jnp.where(causal, b_col - b_row + i_row, NEG)
        m_t = jnp.maximum(a, jnp.max(dmat, axis=1, keepdims=True))
        w_inter = jnp.exp(a - m_t)

        qf = q_ref[:, h * dk:(h + 1) * dk].astype(F32)
        q = qf.astype(BF16)
        kf = k_ref[:, h * dk:(h + 1) * dk].astype(F32) * (dk ** -0.5)
        k = kf.astype(BF16)
        v = v_ref[:, h * dv:(h + 1) * dv].astype(BF16)
        c_prev = c_scr[h]
        n_prev = n_all[h:h + 1, :]
        qk = lax.dot_general(q, k, (((1,), (1,)), ((), ())), preferred_element_type=F32)
        qc = jnp.dot(q, c_prev.astype(BF16), preferred_element_type=F32)
        qn = jnp.sum(qf * n_prev, axis=1, keepdims=True)

        m_new = jnp.maximum(b_last + m_prev, jnp.max(b_last - b_row + i_row, axis=1, keepdims=True))
        w_state = jnp.exp(b_last - b_col + i_col - m_new)
        decay = jnp.exp(b_last + m_prev - m_new)
        kw = (w_state * kf).astype(BF16)
        c_scr[h] = decay * c_prev + lax.dot_general(kw, v, (((0,), (0,)), ((), ())),
                                                    preferred_element_type=F32)
        n_news.append(decay * n_prev + jnp.sum(w_state * kf, axis=0, keepdims=True))
        m_news.append(m_new)
        return h, qk, dmat, m_t, w_inter, qc, qn, v

    def output_stage(h, qk, dmat, m_t, w_inter, qc, qn, v):
        s = qk * jnp.exp(dmat - m_t)
        num = w_inter * qc + jnp.dot(s.astype(BF16), v, preferred_element_type=F32)
        den = w_inter * qn + jnp.sum(s, axis=1, keepdims=True)
        hh = num / jnp.maximum(jnp.abs(den), jnp.exp(-m_t))
        ht = hh * jax.nn.sigmoid(og_ref[:, h * dv:(h + 1) * dv].astype(F32))
        hn = ht * lax.rsqrt(jnp.mean(ht * ht, axis=-1, keepdims=True) + EPS)
        h_ref[:, h * dv:(h + 1) * dv] = (hn * gn_ref[:, h * dv:(h + 1) * dv]).astype(h_ref.dtype)

    pending = []
    for h in range(nh):
        pending.append(state_stage(h))
        if len(pending) > HEADS_AHEAD:
            output_stage(*pending.pop(0))
    for item in pending:
        output_stage(*item)

    n_scr[...] = jnp.concatenate(n_news, axis=0)
    m_scr[...] = jnp.concatenate(m_news, axis=0)

    @pl.when(c == pl.num_programs(1) - 1)
    def _():
        c_out[...] = c_scr[...].astype(c_out.dtype)
        n_out[...] = n_scr[...].astype(n_out.dtype)
        m_out[...] = m_scr[...].astype(m_out.dtype)


def mlstm(z, gates, b_gate, g_norm, state, batch, seq, chunk, out_dtype):
    c0, n0, m0 = state
    nh = B_HEADS
    dk, dv = c0.shape[-2], c0.shape[-1]
    nc = seq // chunk
    qk_w, v_w = nh * dk, nh * dv
    assert v_w % qk_w == 0
    gcol = gates.reshape(batch * nc, chunk, 2 * nh)
    grow = jnp.swapaxes(gcol, 1, 2)
    row = lambda b, c: b * nc + c
    state_specs = [
        pl.BlockSpec((None, nh, dk, dv), lambda b, c: (b, 0, 0, 0)),
        pl.BlockSpec((None, nh, dk), lambda b, c: (b, 0, 0)),
        pl.BlockSpec((None, nh, 1), lambda b, c: (b, 0, 0)),
    ]
    in_specs = [
        pl.BlockSpec((chunk, qk_w), lambda b, c: (row(b, c), 0)),
        pl.BlockSpec((chunk, qk_w), lambda b, c: (row(b, c), 1)),
        pl.BlockSpec((chunk, v_w), lambda b, c: (row(b, c), 2 * qk_w // v_w)),
        pl.BlockSpec((chunk, v_w), lambda b, c: (row(b, c), 2 * qk_w // v_w + 1)),
        pl.BlockSpec((None, chunk, 2 * nh), lambda b, c: (row(b, c), 0, 0)),
        pl.BlockSpec((None, 2 * nh, chunk), lambda b, c: (row(b, c), 0, 0)),
        pl.BlockSpec((1, 2 * nh), lambda b, c: (0, 0)),
        pl.BlockSpec((2 * nh, 1), lambda b, c: (0, 0)),
        pl.BlockSpec((1, v_w), lambda b, c: (0, 0)),
    ] + state_specs
    out_shape = [
        jax.ShapeDtypeStruct((batch * seq, v_w), out_dtype),
        jax.ShapeDtypeStruct((batch, nh, dk, dv), F32),
        jax.ShapeDtypeStruct((batch, nh, dk), F32),
        jax.ShapeDtypeStruct((batch, nh, 1), F32),
    ]
    bg = b_gate.astype(F32)
    hn, c_new, n_new, m_new = pl.pallas_call(
        functools.partial(_mlstm_kernel, chunk=chunk),
        grid=(batch, nc),
        in_specs=in_specs,
        out_specs=[pl.BlockSpec((chunk, v_w), lambda b, c: (row(b, c), 0))] + state_specs,
        out_shape=out_shape,
        scratch_shapes=[pltpu.VMEM((nh, dk, dv), F32), pltpu.VMEM((nh, dk), F32), pltpu.VMEM((nh, 1), F32)],
        compiler_params=_cparams(("parallel", "arbitrary")),
        name="mlstm",
    )(z, z, z, z, gcol, grow, bg.reshape(1, 2 * nh), bg.reshape(2 * nh, 1),
      g_norm.astype(F32).reshape(1, v_w),
      c0.reshape(batch, nh, dk, dv), n0.reshape(batch, nh, dk), m0.reshape(batch, nh, 1))
    return hn, (c_new, n_new, m_new.reshape(batch, nh))


def kernel(x_prompt, x_sample, cache_kv_w128, cache_kv_w512, cache_kv_w2048, state_C, state_n, state_m,
           norm_mix, norm_ffn, norm_final, rel_bias, w_a_in, w_a_out, w_b_in, b_b_gate, b_b_norm, w_b_out,
           w_ffn_gate, w_ffn_up, w_ffn_down):
    bp, seq, d_model = x_prompt.shape
    bs, t_new, _ = x_sample.shape
    depth = norm_mix.shape[0]
    nh, dk, dv = state_C.shape[2], state_C.shape[3], state_C.shape[4]
    main_w = 2 * nh * dk + 2 * nh * dv
    caches = (cache_kv_w128, cache_kv_w512, cache_kv_w2048)
    w_b_in_t = jnp.swapaxes(w_b_in, 1, 2)

    biases =[_group_bias(rel_bias, g) for g in range(N_GROUPS)]
    band = _band_tables(biases)

    xp = x_prompt.reshape(bp * seq, d_model)
    xs = x_sample.reshape(bs * t_new, d_model)
    kv_p = [[] for _ in range(N_GROUPS)]
    kv_s = [[] for _ in range(N_GROUPS)]
    st_p, st_s = [], []
    for layer in range(depth):
        j = layer // 2
        hp, hs = rmsnorm(xp, xs, norm_mix[layer], BF16)
        if layer % 2 == 0:
            qp, qs = attn_q_proj(hp, hs, w_a_in, j)
            kvp, kvs = [], []
            for g, (window, _) in enumerate(A_GROUPS):
                kp, ks, fp, fs = attn_kv_proj(hp, hs, w_a_in, j, g, bp, seq, min(window, seq))
                kvp.append(kp)
                kvs.append(ks)
                kv_p[g].append(fp)
                kv_s[g].append(fs.reshape(bs, t_new, 2, A_HEADS, A_HEAD_DIM))
            yp = attention_prompt(qp, kvp, band, bp, seq)
            ys = attention_sample(qs, kvs, caches, j, biases, bs, t_new)
            xp, xs = dense(yp, ys, w_a_out, j, res=xp, res_s=xs, name="attn_out_proj")
        else:
            zero = (jnp.zeros((bp, nh, dk, dv), F32), jnp.zeros((bp, nh, dk), F32), jnp.zeros((bp, nh), F32))
            past = (state_C[j], state_n[j], state_m[j])
            zp, zs = dense(hp, hs, w_b_in_t, j, n_cols=main_w, w_t=True, name="mlstm_in_proj")
            gp, gs = gate_proj(hp, hs, w_b_in_t, j, main_w, 2 * nh)
            hnp, sp = mlstm(zp, gp, b_b_gate[j], b_b_norm[j], zero, bp, seq, min(PROMPT_CHUNK, seq), BF16)
            hns, ss = mlstm(zs, gs, b_b_gate[j], b_b_norm[j], past, bs, t_new, t_new, F32)
            xp, xs = dense(hnp, hns, w_b_out, j, res=xp, res_s=xs, name="mlstm_out_proj")
            st_p.append(sp)
            st_s.append(ss)
        hp, hs = rmsnorm(xp, xs, norm_ffn[layer], BF16)
        ap, as_ = ffn_up(hp, hs, w_ffn_gate, w_ffn_up, layer)
        xp, xs = dense(ap, as_, w_ffn_down, layer, res=xp, res_s=xs, name="ffn_down")

    y_prompt, y_sample = rmsnorm(xp, xs, norm_final, F32)
    y_prompt = y_prompt.reshape(bp, seq, d_model)
    y_sample = y_sample.reshape(bs, t_new, d_model)
    kvd = cache_kv_w128.dtype
    sd = state_C.dtype
    kv_out_p = [jnp.stack(kv_p[g]).astype(kvd) for g in range(N_GROUPS)]
    kv_out_s = [jnp.stack(kv_s[g]).astype(kvd) for g in range(N_GROUPS)]
    states_p = [jnp.stack([s[i] for s in st_p]).astype(sd) for i in range(3)]
    states_s = [jnp.stack([s[i] for s in st_s]).astype(sd) for i in range(3)]
    return (y_prompt, y_sample, *kv_out_p, *kv_out_s, *states_p, *states_s)
```

```python
import functools
import math

import jax
import jax.numpy as jnp
from jax import lax
from jax.experimental import pallas as pl
from jax.experimental.pallas import tpu as pltpu

F32 = jnp.float32
BF16 = jnp.bfloat16

EPS = 1e-6
NEG = -1e30

A_GROUPS = ((128, 1), (512, 4), (2048, 16))
N_GROUPS = len(A_GROUPS)
A_HEADS = 8
A_HEAD_DIM = 128
A_STEPS = 128
A_BLOCK = 128
NUM_BUCKETS = 32
MAX_DISTANCE = 2048
B_HEADS = 8
LANES = 128
VMEM_LIMIT = 56 * 1024 * 1024
PROMPT_CHUNK = 256
HEADS_AHEAD = 1
LOGITS_AHEAD = 3


def _cparams(sem):
    return pltpu.CompilerParams(dimension_semantics=sem, vmem_limit_bytes=VMEM_LIMIT)


def _pick(n, cands):
    for c in cands:
        if n % c == 0:
            return c
    return n


def _row_tile(m, fixed_bytes, block_bytes_per_row, temp_bytes_per_row):
    for tm in (1024, 512, 256, 128, 64):
        if m % tm == 0 and fixed_bytes + tm * (2 * block_bytes_per_row + temp_bytes_per_row) <= VMEM_LIMIT:
            return tm
    return m


def _rmsnorm_kernel(x_ref, xs_ref, g_ref, o_ref, os_ref):
    def norm(x):
        return x * lax.rsqrt(jnp.mean(x * x, axis=-1, keepdims=True) + EPS) * g_ref[...]

    @pl.when(pl.program_id(0) == 0)
    def _():
        os_ref[...] = norm(xs_ref[...]).astype(os_ref.dtype)

    o_ref[...] = norm(x_ref[...]).astype(o_ref.dtype)


def rmsnorm(x, xs, g, out_dtype):
    m, d = x.shape
    ms = xs.shape[0]
    tm = _pick(m, (512, 256, 128, 64))
    return pl.pallas_call(
        _rmsnorm_kernel,
        grid=(m // tm,),
        in_specs=[pl.BlockSpec((tm, d), lambda i: (i, 0)),
                  pl.BlockSpec((ms, d), lambda i: (0, 0)),
                  pl.BlockSpec((1, d), lambda i: (0, 0))],
        out_specs=[pl.BlockSpec((tm, d), lambda i: (i, 0)),
                   pl.BlockSpec((ms, d), lambda i: (0, 0))],
        out_shape=[jax.ShapeDtypeStruct((m, d), out_dtype), jax.ShapeDtypeStruct((ms, d), out_dtype)],
        compiler_params=_cparams(("arbitrary",)),
        name="rmsnorm",
    )(x, xs, g.reshape(1, d))


def _store_cols(o_ref, val):
    if len(o_ref.shape) == 2:
        o_ref[...] = val
    else:
        e = o_ref.shape[-1]
        for c in range(o_ref.shape[0]):
            o_ref[c] = val[:, c * e:(c + 1) * e]


def _linear_kernel(*refs, n_w, has_res, swiglu, m_axis, w_t):
    x_ref, xs_ref = refs[:2]
    w_refs = refs[2:2 + n_w]
    p = 2 + n_w
    r_ref, rs_ref = (refs[p], refs[p + 1]) if has_res else (None, None)
    p += 2 * has_res
    o_ref, os_ref = refs[p], refs[p + 1]
    wb_refs = refs[p + 2:]
    contract = (((1,), (1 if w_t else 0,)), ((), ()))

    def apply(x_ref, r_ref, o_ref):
        if len(x_ref.shape) == 3:
            x = jnp.concatenate([x_ref[c] for c in range(x_ref.shape[0])], axis=1).astype(BF16)
        else:
            x = x_ref[...].astype(BF16)
        accs = [lax.dot_general(x, wb_ref[...], contract, preferred_element_type=F32) for wb_ref in wb_refs]
        out = accs[0] * jax.nn.sigmoid(accs[0]) * accs[1] if swiglu else accs[0]
        if has_res:
            out = r_ref[...] + out
        _store_cols(o_ref, out.astype(o_ref.dtype))

    @pl.when(pl.program_id(m_axis) == 0)
    def _():
        for w_ref, wb_ref in zip(w_refs, wb_refs):
            wb_ref[...] = w_ref[...].astype(BF16)
        apply(xs_ref, rs_ref, os_ref)

    apply(x_ref, r_ref, o_ref)


def linear(x, xs, ws, w_index, outer, out_shape, out_block, out_index, tn, *, res=None, res_s=None,
           swiglu=False, out_dtype=F32, w_t=False, name="linear"):
    if x.ndim == 3:
        m, k = x.shape[1], x.shape[0] * x.shape[2]
        x_spec = lambda tm: pl.BlockSpec((x.shape[0], tm, x.shape[2]), lambda *g: (0, g[nd], 0))
    else:
        m, k = x.shape
        x_spec = lambda tm: pl.BlockSpec((tm, k), lambda *g: (g[nd], 0))
    ms = xs.shape[0]
    nd = len(outer)
    has_res = res is not None
    x_bytes, o_bytes = x.dtype.itemsize, jnp.dtype(out_dtype).itemsize
    row_bytes = k * x_bytes + tn * (o_bytes + (res.dtype.itemsize if has_res else 0))
    w_bytes = len(ws) * k * tn * (2 * ws[0].dtype.itemsize + jnp.dtype(BF16).itemsize)
    acc_bytes = len(ws) * tn * jnp.dtype(F32).itemsize
    tm = _row_tile(m, w_bytes + 2 * ms * row_bytes, row_bytes, acc_bytes)
    w_block = (None, tn, k) if w_t else (None, k, tn)
    main_map = lambda *g: out_index(*g)
    sample_map = lambda *g: out_index(*g[:nd], 0)
    in_specs = [x_spec(tm), pl.BlockSpec((ms, k), lambda *g: (0, 0))]
    in_specs += [pl.BlockSpec(w_block, lambda *g: w_index(*g[:nd])) for _ in ws]
    args = [x, xs, *ws]
    if has_res:
        in_specs += [pl.BlockSpec(out_block(tm), main_map), pl.BlockSpec(out_block(ms), sample_map)]
        args += [res, res_s]
    return pl.pallas_call(
        functools.partial(_linear_kernel, n_w=len(ws), has_res=has_res, swiglu=swiglu, m_axis=nd, w_t=w_t),
        grid=(*outer, m // tm),
        in_specs=in_specs,
        out_specs=[pl.BlockSpec(out_block(tm), main_map), pl.BlockSpec(out_block(ms), sample_map)],
        out_shape=[jax.ShapeDtypeStruct(out_shape(m), out_dtype), jax.ShapeDtypeStruct(out_shape(ms), out_dtype)],
        scratch_shapes=[pltpu.VMEM(w_block[1:], BF16) for _ in ws],
        compiler_params=_cparams(("parallel",) * nd + ("arbitrary",)),
        name=name,
    )(*args)


def dense(x, xs, w, layer, *, res=None, res_s=None, out_dtype=F32, n_cols=None, w_t=False, name="dense"):
    k = x.shape[1] if x.ndim == 2 else x.shape[0] * x.shape[2]
    n = n_cols or w.shape[1 if w_t else 2]
    tn = _pick(n, (1024, 512, 256, 128)) if k <= 2048 else _pick(n, (512, 256, 128))
    w_index = (lambda j: (layer, j, 0)) if w_t else (lambda j: (layer, 0, j))
    return linear(x, xs, [w], w_index, (n // tn,), lambda m: (m, n),
                  lambda tm: (tm, tn), lambda j, i: (i, j), tn, res=res, res_s=res_s,
                  out_dtype=out_dtype, w_t=w_t, name=name)


def ffn_up(x, xs, wg, wu, layer):
    n = wg.shape[2]
    tn = _pick(n, (512, 256, 128))
    return linear(x, xs, [wg, wu], lambda j: (layer, 0, j), (n // tn,), lambda m: (m, n),
                  lambda tm: (tm, tn), lambda j, i: (i, j), tn, swiglu=True, out_dtype=BF16, name="ffn_up")


def attn_q_proj(x, xs, w, layer):
    he = A_HEADS * A_HEAD_DIM
    return linear(x, xs, [w], lambda g: (layer, 0, 3 * g), (N_GROUPS,),
                  lambda m: (N_GROUPS * A_HEADS, m, A_HEAD_DIM),
                  lambda tm: (A_HEADS, tm, A_HEAD_DIM), lambda g, i: (g, i, 0), he, name="attn_q_proj")


def _kv_proj_kernel(x_ref, xs_ref, w_ref, o_ref, os_ref, f_ref, fs_ref, wb_ref, *,
                    tiles_per_batch, first_kept, keep_blk):
    i = pl.program_id(1)
    nh, e = f_ref.shape[-2:]

    def split_heads(dst_ref, rows):
        flat = dst_ref.reshape(rows.shape[0] * nh, e)
        for h in range(nh):
            flat[pl.ds(h, rows.shape[0], stride=nh), :] = rows[:, h * e:(h + 1) * e]

    @pl.when(i == 0)
    def _():
        wb_ref[...] = w_ref[...].astype(BF16)
        acc = jnp.dot(xs_ref[...].astype(BF16), wb_ref[...], preferred_element_type=F32)
        _store_cols(os_ref, acc)
        split_heads(fs_ref, acc)

    acc = jnp.dot(x_ref[...].astype(BF16), wb_ref[...], preferred_element_type=F32)
    _store_cols(o_ref, acc)

    @pl.when(i % tiles_per_batch >= first_kept)
    def _():
        split_heads(f_ref, acc[acc.shape[0] - keep_blk:])


def attn_kv_proj(x, xs, w, layer, group, batch, seq, keep):
    m, k = x.shape
    ms = xs.shape[0]
    he = A_HEADS * A_HEAD_DIM
    tm = _pick(seq, (1024, 512, 256, 128))
    keep_blk = min(keep, tm)
    assert keep % keep_blk == 0 and tm % keep_blk == 0
    tpb = seq // tm
    first_kept = tpb - keep // keep_blk if keep >= tm else tpb - 1

    def kept_map(t, i):
        return (i // tpb, jnp.maximum(i % tpb - first_kept, 0), t, 0, 0)

    return pl.pallas_call(
        functools.partial(_kv_proj_kernel, tiles_per_batch=tpb, first_kept=first_kept, keep_blk=keep_blk),
        grid=(2, m // tm),
        in_specs=[pl.BlockSpec((tm, k), lambda t, i: (i, 0)),
                  pl.BlockSpec((ms, k), lambda t, i: (0, 0)),
                  pl.BlockSpec((None, k, he), lambda t, i: (layer, 0, 3 * group + 1 + t))],
        out_specs=[pl.BlockSpec((A_HEADS, tm, A_HEAD_DIM), lambda t, i: (t, i, 0)),
                   pl.BlockSpec((A_HEADS, ms, A_HEAD_DIM), lambda t, i: (t, 0, 0)),
                   pl.BlockSpec((None, keep_blk, None, A_HEADS, A_HEAD_DIM), kept_map),
                   pl.BlockSpec((ms, None, A_HEADS, A_HEAD_DIM), lambda t, i: (0, t, 0, 0))],
        out_shape=[jax.ShapeDtypeStruct((2 * A_HEADS, m, A_HEAD_DIM), F32),
                   jax.ShapeDtypeStruct((2 * A_HEADS, ms, A_HEAD_DIM), F32),
                   jax.ShapeDtypeStruct((batch, keep, 2, A_HEADS, A_HEAD_DIM), F32),
                   jax.ShapeDtypeStruct((ms, 2, A_HEADS, A_HEAD_DIM), F32)],
        scratch_shapes=[pltpu.VMEM((k, he), BF16)],
        compiler_params=_cparams(("parallel", "arbitrary")),
        name="attn_kv_proj",
    )(x, xs, w)


def _gate_kernel(x_ref, xs_ref, w_ref, o_ref, os_ref):
    w = w_ref[...].astype(BF16)
    contract = (((1,), (1,)), ((), ()))

    @pl.when(pl.program_id(0) == 0)
    def _():
        os_ref[...] = lax.dot_general(xs_ref[...], w, contract, preferred_element_type=F32)

    o_ref[...] = lax.dot_general(x_ref[...], w, contract, preferred_element_type=F32)


def gate_proj(x, xs, w_t, layer, row0, n):
    m, k = x.shape
    ms = xs.shape[0]
    assert row0 % n == 0
    tm = _pick(m, (512, 256, 128, 64))
    return pl.pallas_call(
        _gate_kernel,
        grid=(m // tm,),
        in_specs=[pl.BlockSpec((tm, k), lambda i: (i, 0)),
                  pl.BlockSpec((ms, k), lambda i: (0, 0)),
                  pl.BlockSpec((None, n, k), lambda i: (layer, row0 // n, 0))],
        out_specs=[pl.BlockSpec((tm, n), lambda i: (i, 0)), pl.BlockSpec((ms, n), lambda i: (0, 0))],
        out_shape=[jax.ShapeDtypeStruct((m, n), F32), jax.ShapeDtypeStruct((ms, n), F32)],
        compiler_params=_cparams(("arbitrary",)),
        name="gate_proj",
    )(x, xs, w_t)


def _t5_bucket(dist):
    max_exact = NUM_BUCKETS // 2
    d = jnp.maximum(dist.astype(F32), 1.0)
    large = max_exact + jnp.log(d / max_exact) / math.log(MAX_DISTANCE / max_exact) * (NUM_BUCKETS - max_exact)
    large = jnp.minimum(large.astype(jnp.int32), NUM_BUCKETS - 1)
    return jnp.where(dist < max_exact, dist, large)


def _group_bias(rel_bias, g):
    dil = A_GROUPS[g][1]
    buckets = _t5_bucket(jnp.arange(A_STEPS + 1, dtype=jnp.int32) * dil)
    onehot = (buckets[:, None] == jnp.arange(NUM_BUCKETS)[None, :]).astype(F32)
    cols = rel_bias[:, g * A_HEADS:(g + 1) * A_HEADS].astype(F32)
    return jnp.einsum("kb,bh->hk", onehot, cols, precision=lax.Precision.HIGHEST)


def _toeplitz(f, lo, n_rows, n_cols, offset):
    first, last = offset - n_cols + 1, offset + n_rows - 1
    pad_lo, pad_hi = max(lo - first, 0), max(last - (lo + f.shape[-1] - 1), 0)
    fe = jnp.pad(f, [(0, 0)] * (f.ndim - 1) + [(pad_lo, pad_hi)], constant_values=NEG)
    base = lo - pad_lo
    ext = lambda a, b: fe[..., a - base:b - base]
    period = n_rows + n_cols - 1
    w = jnp.concatenate([ext(first, offset + 1)[..., ::-1], ext(offset + 1, last + 1)[..., ::-1]], axis=-1)
    flat = jnp.tile(w, n_rows)[..., :n_rows * (period - 1)]
    return flat.reshape(*f.shape[:-1], n_rows, period - 1)[..., :n_cols]


def _band_tables(biases):
    return jnp.stack([_toeplitz(b, 0, A_BLOCK, 2 * A_BLOCK, A_BLOCK) for b in biases], axis=1)


def _rows(start, size, stride):
    return pl.ds(start, size) if stride == 1 else pl.ds(start, size, stride=stride)


def _masked_logits(q, k, tab, scale):
    s = lax.dot_general(q.astype(BF16), k.astype(BF16), (((1,), (1,)), ((), ())),
                        preferred_element_type=F32) * scale
    return jnp.where(tab > 0.5 * NEG, s + tab, NEG)


def _attn_prompt_kernel(*refs, seq):
    qkv = refs[:3 * N_GROUPS]
    band_ref, y_ref, o_scr, lse_scr = refs[3 * N_GROUPS:]
    scale = A_HEAD_DIM ** -0.5
    q_blk = A_BLOCK

    def logits(g, dil, q_start, k_start, n_keys):
        q_ref, k_ref, v_ref = qkv[3 * g:3 * g + 3]
        q = q_ref[_rows(q_start, q_blk, dil), :]
        k = k_ref[_rows(k_start, n_keys, dil), :]
        v = v_ref[_rows(k_start, n_keys, dil), :]
        tab = band_ref[g] if n_keys == 2 * q_blk else band_ref[g, :, q_blk:]
        return _masked_logits(q, k, tab, scale), v.astype(BF16)

    def finish(g, dil, q_start, s, v):
        m = jnp.max(s, axis=-1, keepdims=True)
        p = jnp.exp(s - m)
        l = jnp.sum(p, axis=-1, keepdims=True)
        o = jnp.dot(p.astype(BF16), v, preferred_element_type=F32)
        rows = _rows(q_start, q_blk, dil)
        o_scr[g, rows, :] = o / l
        lse_scr[g, rows, :] = jnp.broadcast_to(m + jnp.log(l), (q_blk, LANES))

    blocks = []
    for g, (_, dil) in enumerate(A_GROUPS):
        for r in range(dil):
            blocks.append((g, dil, r, r, q_blk))
            for n in range(1, seq // dil // q_blk):
                start = r + dil * q_blk * (n - 1)
                blocks.append((g, dil, start + dil * q_blk, start, 2 * q_blk))
    pending = []
    for g, dil, q_start, k_start, n_keys in blocks:
        s, v = logits(g, dil, q_start, k_start, n_keys)
        pending.append((g, dil, q_start, s, v))
        if len(pending) > LOGITS_AHEAD:
            finish(*pending.pop(0))
    for item in pending:
        finish(*item)

    chunk = 256

    def merge(c, carry):
        rows = pl.ds(pl.multiple_of(c * chunk, chunk), chunk)
        lses = [lse_scr[g, rows, :] for g in range(N_GROUPS)]
        top = functools.reduce(jnp.maximum, lses)
        ws = [jnp.exp(lse - top) for lse in lses]
        num = functools.reduce(jnp.add, [w * o_scr[g, rows, :] for g, w in enumerate(ws)])
        y_ref[rows, :] = (num / functools.reduce(jnp.add, ws)).astype(y_ref.dtype)
        return carry

    lax.fori_loop(0, seq // chunk, merge, 0)


def attention_prompt(q, kv, band, batch, seq):
    e = A_HEAD_DIM
    assert all(seq % (dil * A_BLOCK) == 0 for _, dil in A_GROUPS)
    in_specs, args = [], []
    for g in range(N_GROUPS):
        in_specs.append(pl.BlockSpec((None, seq, e), lambda b, h, g=g: (g * A_HEADS + h, b, 0)))
        in_specs.append(pl.BlockSpec((None, seq, e), lambda b, h: (h, b, 0)))
        in_specs.append(pl.BlockSpec((None, seq, e), lambda b, h: (A_HEADS + h, b, 0)))
        args += [q, kv[g], kv[g]]
    in_specs.append(pl.BlockSpec((None, N_GROUPS, A_BLOCK, 2 * A_BLOCK), lambda b, h: (h, 0, 0, 0)))
    return pl.pallas_call(
        functools.partial(_attn_prompt_kernel, seq=seq),
        grid=(batch, A_HEADS),
        in_specs=in_specs,
        out_specs=pl.BlockSpec((None, seq, e), lambda b, h: (h, b, 0)),
        out_shape=jax.ShapeDtypeStruct((A_HEADS, batch * seq, e), BF16),
        scratch_shapes=[pltpu.VMEM((N_GROUPS, seq, e), F32),
                        pltpu.VMEM((N_GROUPS, seq, LANES), F32)],
        compiler_params=_cparams(("parallel", "parallel")),
        name="attention_prompt",
    )(*args, band)


def _attn_sample_kernel(*refs):
    q_ref = refs[0]
    kv_new = refs[1:1 + N_GROUPS]
    caches = refs[1 + N_GROUPS:1 + 3 * N_GROUPS]
    tables = refs[1 + 3 * N_GROUPS:1 + 5 * N_GROUPS]
    y_ref = refs[1 + 5 * N_GROUPS]
    scale = A_HEAD_DIM ** -0.5
    e = A_HEAD_DIM

    def head_rows(ref, h):
        rows = math.prod(ref.shape[:-2])
        return ref.reshape(rows * A_HEADS, e)[pl.ds(h, rows, stride=A_HEADS), :]

    for h in range(A_HEADS):
        lses, outs = [], []
        for g in range(N_GROUPS):
            q = q_ref[g * A_HEADS + h]
            k_new = kv_new[g][h]
            v_new = kv_new[g][A_HEADS + h]
            k_old = head_rows(caches[2 * g], h)
            v_old = head_rows(caches[2 * g + 1], h)
            s_old = _masked_logits(q, k_old, tables[2 * g][h], scale)
            s_new = _masked_logits(q, k_new, tables[2 * g + 1][h], scale)
            m = jnp.maximum(jnp.max(s_old, axis=-1, keepdims=True), jnp.max(s_new, axis=-1, keepdims=True))
            p_old = jnp.exp(s_old - m)
            p_new = jnp.exp(s_new - m)
            l = jnp.sum(p_old, axis=-1, keepdims=True) + jnp.sum(p_new, axis=-1, keepdims=True)
            o = (jnp.dot(p_old.astype(BF16), v_old.astype(BF16), preferred_element_type=F32)
                 + jnp.dot(p_new.astype(BF16), v_new.astype(BF16), preferred_element_type=F32))
            outs.append(o / l)
            lses.append(m + jnp.log(l))
        top = functools.reduce(jnp.maximum, lses)
        ws = [jnp.exp(lse - top) for lse in lses]
        num = functools.reduce(jnp.add, [w * o for w, o in zip(ws, outs)])
        y_ref[:, h * e:(h + 1) * e] = (num / functools.reduce(jnp.add, ws)).astype(y_ref.dtype)


def attention_sample(q, kv_new, caches, layer, biases, batch, t_new):
    he = A_HEADS * A_HEAD_DIM
    in_specs = [pl.BlockSpec((N_GROUPS * A_HEADS, t_new, A_HEAD_DIM), lambda b: (0, b, 0))]
    args = [q]
    for g in range(N_GROUPS):
        in_specs.append(pl.BlockSpec((2 * A_HEADS, t_new, A_HEAD_DIM), lambda b: (0, b, 0)))
        args.append(kv_new[g])
    tables = []
    for g, (_, dil) in enumerate(A_GROUPS):
        c = caches[g]
        n_past = c.shape[2]
        bg = biases[g]
        by_dist = jnp.concatenate([bg[..., None], jnp.full(bg.shape + (dil - 1,), NEG, F32)], axis=-1)
        by_dist = by_dist.reshape(bg.shape[0], -1)[:, :A_STEPS * dil + 1]
        tab = _toeplitz(by_dist, 0, t_new, n_past + t_new, n_past)
        tab_old, tab_new = tab[:, :, :n_past], tab[:, :, n_past:]
        if dil % t_new == 0 and n_past % dil == 0:
            c = c.reshape(c.shape[0], batch, n_past // dil, dil, 2, A_HEADS, A_HEAD_DIM)
            for t in range(2):
                in_specs.append(pl.BlockSpec((None, None, n_past // dil, t_new, None, A_HEADS, A_HEAD_DIM),
                                             lambda b, t=t: (layer, b, 0, 0, t, 0, 0)))
                args.append(c)
            tab_old = tab_old.reshape(A_HEADS, t_new, n_past // dil, dil)[..., :t_new]
            tab_old = tab_old.reshape(A_HEADS, t_new, (n_past // dil) * t_new)
        else:
            for t in range(2):
                in_specs.append(pl.BlockSpec((None, None, n_past, None, A_HEADS, A_HEAD_DIM),
                                             lambda b, t=t: (layer, b, 0, t, 0, 0)))
                args.append(c)
        tables += [tab_old, tab_new]
    for tab in tables:
        in_specs.append(pl.BlockSpec(tab.shape, lambda b: (0, 0, 0)))
        args.append(tab)
    return pl.pallas_call(
        _attn_sample_kernel,
        grid=(batch,),
        in_specs=in_specs,
        out_specs=pl.BlockSpec((t_new, he), lambda b: (b, 0)),
        out_shape=jax.ShapeDtypeStruct((batch * t_new, he), F32),
        compiler_params=_cparams(("parallel",)),
        name="attention_sample",
    )(*args)


def _log_sigmoid(x):
    return jnp.minimum(x, 0.0) - jnp.log1p(jnp.exp(-jnp.abs(x)))


def _mlstm_kernel(q_ref, k_ref, v_ref, og_ref, gc_ref, gr_ref, bc_ref, br_ref, gn_ref,
                  c0_ref, n0_ref, m0_ref,
                  h_ref, c_out, n_out, m_out,
                  c_scr, n_scr, m_scr, *, chunk):
    c = pl.program_id(1)
    nh, dk, dv = c_scr.shape

    @pl.when(c == 0)
    def _():
        c_scr[...] = c0_ref[...].astype(F32)
        n_scr[...] = n0_ref[...].astype(F32)
        m_scr[...] = m0_ref[...].astype(F32)

    gc = gc_ref[...] + bc_ref[...]
    gr = gr_ref[...] + br_ref[...]
    logf_c = _log_sigmoid(gc[:, nh:])
    logf_r = _log_sigmoid(gr[nh:, :])
    qi = lax.broadcasted_iota(jnp.int32, (chunk, chunk), 0)
    si = lax.broadcasted_iota(jnp.int32, (chunk, chunk), 1)
    causal = si <= qi
    upper = qi <= si
    m_all, n_all = m_scr[...], n_scr[...]
    m_news, n_news = [], []

    def state_stage(h):
        i_col, i_row = gc[:, h:h + 1], gr[h:h + 1, :]
        logf_col, logf_row = logf_c[:, h:h + 1], logf_r[h:h + 1, :]
        b_col = jnp.sum(jnp.where(causal, logf_row, 0.0), axis=1, keepdims=True)
        b_row = jnp.sum(jnp.where(upper, logf_col, 0.0), axis=0, keepdims=True)
        b_last = jnp.sum(logf_row, axis=1, keepdims=True)

        m_prev = m_all[h:h + 1, :]
        a = b_col + m_prev
        dmat = jnp.where(causal, b_col - b_row + i_row, NEG)
        m_t = jnp.maximum(a, jnp.max(dmat, axis=1, keepdims=True))
        w_inter = jnp.exp(a - m_t)

        qf = q_ref[:, h * dk:(h + 1) * dk].astype(F32)
        q = qf.astype(BF16)
        kf = k_ref[:, h * dk:(h + 1) * dk].astype(F32) * (dk ** -0.5)
        k = kf.astype(BF16)
        v = v_ref[:, h * dv:(h + 1) * dv].astype(BF16)
        c_prev = c_scr[h]
        n_prev = n_all[h:h + 1, :]
        qk = lax.dot_general(q, k, (((1,), (1,)), ((), ())), preferred_element_type=F32)
        qc = jnp.dot(q, c_prev.astype(BF16), preferred_element_type=F32)
        qn = jnp.sum(qf * n_prev, axis=1, keepdims=True)

        m_new = jnp.maximum(b_last + m_prev, jnp.max(b_last - b_row + i_row, axis=1, keepdims=True))
        w_state = jnp.exp(b_last - b_col + i_col - m_new)
        decay = jnp.exp(b_last + m_prev - m_new)
        kw = (w_state * kf).astype(BF16)
        c_scr[h] = decay * c_prev + lax.dot_general(kw, v, (((0,), (0,)), ((), ())),
                                                    preferred_element_type=F32)
        n_news.append(decay * n_prev + jnp.sum(w_state * kf, axis=0, keepdims=True))
        m_news.append(m_new)
        return h, qk, dmat, m_t, w_inter, qc, qn, v

    def output_stage(h, qk, dmat, m_t, w_inter, qc, qn, v):
        s = qk * jnp.exp(dmat - m_t)
        num = w_inter * qc + jnp.dot(s.astype(BF16), v, preferred_element_type=F32)
        den = w_inter * qn + jnp.sum(s, axis=1, keepdims=True)
        hh = num / jnp.maximum(jnp.abs(den), jnp.exp(-m_t))
        ht = hh * jax.nn.sigmoid(og_ref[:, h * dv:(h + 1) * dv].astype(F32))
        hn = ht * lax.rsqrt(jnp.mean(ht * ht, axis=-1, keepdims=True) + EPS)
        h_ref[:, h * dv:(h + 1) * dv] = (hn * gn_ref[:, h * dv:(h + 1) * dv]).astype(h_ref.dtype)

    pending = []
    for h in range(nh):
        pending.append(state_stage(h))
        if len(pending) > HEADS_AHEAD:
            output_stage(*pending.pop(0))
    for item in pending:
        output_stage(*item)

    n_scr[...] = jnp.concatenate(n_news, axis=0)
    m_scr[...] = jnp.concatenate(m_news, axis=0)

    @pl.when(c == pl.num_programs(1) - 1)
    def _():
        c_out[...] = c_scr[...].astype(c_out.dtype)
        n_out[...] = n_scr[...].astype(n_out.dtype)
        m_out[...] = m_scr[...].astype(m_out.dtype)


def mlstm(z, gates, b_gate, g_norm, state, batch, seq, chunk, out_dtype):
    c0, n0, m0 = state
    nh = B_HEADS
    dk, dv = c0.shape[-2], c0.shape[-1]
    nc = seq // chunk
    qk_w, v_w = nh * dk, nh * dv
    assert v_w % qk_w == 0
    gcol = gates.reshape(batch * nc, chunk, 2 * nh)
    grow = jnp.swapaxes(gcol, 1, 2)
    row = lambda b, c: b * nc + c
    state_specs = [
        pl.BlockSpec((None, nh, dk, dv), lambda b, c: (b, 0, 0, 0)),
        pl.BlockSpec((None, nh, dk), lambda b, c: (b, 0, 0)),
        pl.BlockSpec((None, nh, 1), lambda b, c: (b, 0, 0)),
    ]
    in_specs = [
        pl.BlockSpec((chunk, qk_w), lambda b, c: (row(b, c), 0)),
        pl.BlockSpec((chunk, qk_w), lambda b, c: (row(b, c), 1)),
        pl.BlockSpec((chunk, v_w), lambda b, c: (row(b, c), 2 * qk_w // v_w)),
        pl.BlockSpec((chunk, v_w), lambda b, c: (row(b, c), 2 * qk_w // v_w + 1)),
        pl.BlockSpec((None, chunk, 2 * nh), lambda b, c: (row(b, c), 0, 0)),
        pl.BlockSpec((None, 2 * nh, chunk), lambda b, c: (row(b, c), 0, 0)),
        pl.BlockSpec((1, 2 * nh), lambda b, c: (0, 0)),
        pl.BlockSpec((2 * nh, 1), lambda b, c: (0, 0)),
        pl.BlockSpec((1, v_w), lambda b, c: (0, 0)),
    ] + state_specs
    out_shape = [
        jax.ShapeDtypeStruct((batch * seq, v_w), out_dtype),
        jax.ShapeDtypeStruct((batch, nh, dk, dv), F32),
        jax.ShapeDtypeStruct((batch, nh, dk), F32),
        jax.ShapeDtypeStruct((batch, nh, 1), F32),
    ]
    bg = b_gate.astype(F32)
    hn, c_new, n_new, m_new = pl.pallas_call(
        functools.partial(_mlstm_kernel, chunk=chunk),
        grid=(batch, nc),
        in_specs=in_specs,
        out_specs=[pl.BlockSpec((chunk, v_w), lambda b, c: (row(b, c), 0))] + state_specs,
        out_shape=out_shape,
        scratch_shapes=[pltpu.VMEM((nh, dk, dv), F32), pltpu.VMEM((nh, dk), F32), pltpu.VMEM((nh, 1), F32)],
        compiler_params=_cparams(("parallel", "arbitrary")),
        name="mlstm",
    )(z, z, z, z, gcol, grow, bg.reshape(1, 2 * nh), bg.reshape(2 * nh, 1),
      g_norm.astype(F32).reshape(1, v_w),
      c0.reshape(batch, nh, dk, dv), n0.reshape(batch, nh, dk), m0.reshape(batch, nh, 1))
    return hn, (c_new, n_new, m_new.reshape(batch, nh))


def kernel(x_prompt, x_sample, cache_kv_w128, cache_kv_w512, cache_kv_w2048, state_C, state_n, state_m,
           norm_mix, norm_ffn, norm_final, rel_bias, w_a_in, w_a_out, w_b_in, b_b_gate, b_b_norm, w_b_out,
           w_ffn_gate, w_ffn_up, w_ffn_down):
    bp, seq, d_model = x_prompt.shape
    bs, t_new, _ = x_sample.shape
    depth = norm_mix.shape[0]
    nh, dk, dv = state_C.shape[2], state_C.shape[3], state_C.shape[4]
    main_w = 2 * nh * dk + 2 * nh * dv
    caches = (cache_kv_w128, cache_kv_w512, cache_kv_w2048)
    w_b_in_t = jnp.swapaxes(w_b_in, 1, 2)

    biases =[_group_bias(rel_bias, g) for g in range(N_GROUPS)]
    band = _band_tables(biases)

    xp = x_prompt.reshape(bp * seq, d_model)
    xs = x_sample.reshape(bs * t_new, d_model)
    kv_p = [[] for _ in range(N_GROUPS)]
    kv_s = [[] for _ in range(N_GROUPS)]
    st_p, st_s = [], []
    for layer in range(depth):
        j = layer // 2
        hp, hs = rmsnorm(xp, xs, norm_mix[layer], BF16)
        if layer % 2 == 0:
            qp, qs = attn_q_proj(hp, hs, w_a_in, j)
            kvp, kvs = [], []
            for g, (window, _) in enumerate(A_GROUPS):
                kp, ks, fp, fs = attn_kv_proj(hp, hs, w_a_in, j, g, bp, seq, min(window, seq))
                kvp.append(kp)
                kvs.append(ks)
                kv_p[g].append(fp)
                kv_s[g].append(fs.reshape(bs, t_new, 2, A_HEADS, A_HEAD_DIM))
            yp = attention_prompt(qp, kvp, band, bp, seq)
            ys = attention_sample(qs, kvs, caches, j, biases, bs, t_new)
            xp, xs = dense(yp, ys, w_a_out, j, res=xp, res_s=xs, name="attn_out_proj")
        else:
            zero = (jnp.zeros((bp, nh, dk, dv), F32), jnp.zeros((bp, nh, dk), F32), jnp.zeros((bp, nh), F32))
            past = (state_C[j], state_n[j], state_m[j])
            zp, zs = dense(hp, hs, w_b_in_t, j, n_cols=main_w, w_t=True, name="mlstm_in_proj")
            gp, gs = gate_proj(hp, hs, w_b_in_t, j, main_w, 2 * nh)
            hnp, sp = mlstm(zp, gp, b_b_gate[j], b_b_norm[j], zero, bp, seq, min(PROMPT_CHUNK, seq), BF16)
            hns, ss = mlstm(zs, gs, b_b_gate[j], b_b_norm[j], past, bs, t_new, t_new, F32)
            xp, xs = dense(hnp, hns, w_b_out, j, res=xp, res_s=xs, name="mlstm_out_proj")
            st_p.append(sp)
            st_s.append(ss)
        hp, hs = rmsnorm(xp, xs, norm_ffn[layer], BF16)
        ap, as_ = ffn_up(hp, hs, w_ffn_gate, w_ffn_up, layer)
        xp, xs = dense(ap, as_, w_ffn_down, layer, res=xp, res_s=xs, name="ffn_down")

    y_prompt, y_sample = rmsnorm(xp, xs, norm_final, F32)
    y_prompt = y_prompt.reshape(bp, seq, d_model)
    y_sample = y_sample.reshape(bs, t_new, d_model)
    kvd = cache_kv_w128.dtype
    sd = state_C.dtype
    kv_out_p = [jnp.stack(kv_p[g]).astype(kvd) for g in range(N_GROUPS)]
    kv_out_s = [jnp.stack(kv_s[g]).astype(kvd) for g in range(N_GROUPS)]
    states_p = [jnp.stack([s[i] for s in st_p]).astype(sd) for i in range(3)]
    states_s = [jnp.stack([s[i] for s in st_s]).astype(sd) for i in range(3)]
    return (y_prompt, y_sample, *kv_out_p, *kv_out_s, *states_p, *states_s)
```

```python
import functools
import math

import jax
import jax.numpy as jnp
from jax import lax
from jax.experimental import pallas as pl
from jax.experimental.pallas import tpu as pltpu

F32 = jnp.float32
BF16 = jnp.bfloat16

EPS = 1e-6
NEG = -1e30

A_GROUPS = ((128, 1), (512, 4), (2048, 16))
N_GROUPS = len(A_GROUPS)
A_HEADS = 8
A_HEAD_DIM = 128
A_STEPS = 128
A_BLOCK = 128
NUM_BUCKETS = 32
MAX_DISTANCE = 2048
B_HEADS = 8
LANES = 128
VMEM_LIMIT = 56 * 1024 * 1024
PROMPT_CHUNK = 256
HEADS_AHEAD = 1
LOGITS_AHEAD = 3


def _cparams(sem):
    return pltpu.CompilerParams(dimension_semantics=sem, vmem_limit_bytes=VMEM_LIMIT)


def _pick(n, cands):
    for c in cands:
        if n % c == 0:
            return c
    return n


def _row_tile(m, fixed_bytes, block_bytes_per_row, temp_bytes_per_row):
    for tm in (1024, 512, 256, 128, 64):
        if m % tm == 0 and fixed_bytes + tm * (2 * block_bytes_per_row + temp_bytes_per_row) <= VMEM_LIMIT:
            return tm
    return m


def _rmsnorm_kernel(x_ref, xs_ref, g_ref, o_ref, os_ref):
    def norm(x):
        return x * lax.rsqrt(jnp.mean(x * x, axis=-1, keepdims=True) + EPS) * g_ref[...]

    @pl.when(pl.program_id(0) == 0)
    def _():
        os_ref[...] = norm(xs_ref[...]).astype(os_ref.dtype)

    o_ref[...] = norm(x_ref[...]).astype(o_ref.dtype)


def _round_kernel(x_ref, xs_ref, o_ref, os_ref):
    @pl.when(pl.program_id(0) == 0)
    def _():
        os_ref[...] = xs_ref[...].astype(os_ref.dtype)

    o_ref[...] = x_ref[...].astype(o_ref.dtype)


def round_bf16(x, xs):
    m, d = x.shape
    ms = xs.shape[0]
    tm = _pick(m, (512, 256, 128, 64))
    return pl.pallas_call(
        _round_kernel,
        grid=(m // tm,),
        in_specs=[pl.BlockSpec((tm, d), lambda i: (i, 0)), pl.BlockSpec((ms, d), lambda i: (0, 0))],
        out_specs=[pl.BlockSpec((tm, d), lambda i: (i, 0)), pl.BlockSpec((ms, d), lambda i: (0, 0))],
        out_shape=[jax.ShapeDtypeStruct((m, d), BF16), jax.ShapeDtypeStruct((ms, d), BF16)],
        compiler_params=_cparams(("arbitrary",)),
        name="round_bf16",
    )(x, xs)


def rmsnorm(x, xs, g, out_dtype):
    m, d = x.shape
    ms = xs.shape[0]
    tm = _pick(m, (512, 256, 128, 64))
    return pl.pallas_call(
        _rmsnorm_kernel,
        grid=(m // tm,),
        in_specs=[pl.BlockSpec((tm, d), lambda i: (i, 0)),
                  pl.BlockSpec((ms, d), lambda i: (0, 0)),
                  pl.BlockSpec((1, d), lambda i: (0, 0))],
        out_specs=[pl.BlockSpec((tm, d), lambda i: (i, 0)),
                   pl.BlockSpec((ms, d), lambda i: (0, 0))],
        out_shape=[jax.ShapeDtypeStruct((m, d), out_dtype), jax.ShapeDtypeStruct((ms, d), out_dtype)],
        compiler_params=_cparams(("arbitrary",)),
        name="rmsnorm",
    )(x, xs, g.reshape(1, d))


def _store_cols(o_ref, val):
    if len(o_ref.shape) == 2:
        o_ref[...] = val
    else:
        e = o_ref.shape[-1]
        for c in range(o_ref.shape[0]):
            o_ref[c] = val[:, c * e:(c + 1) * e]


def _row_rstd(x):
    xf = x.astype(F32)
    return lax.rsqrt(jnp.mean(xf * xf, axis=-1, keepdims=True) + EPS)


def _linear_kernel(*refs, n_w, has_res, swiglu, m_axis, w_t, norm, emit_bf16):
    x_ref, xs_ref = refs[:2]
    p = 2
    g_ref = refs[p] if norm else None
    p += norm
    w_refs = refs[p:p + n_w]
    p += n_w
    r_ref, rs_ref = (refs[p], refs[p + 1]) if has_res else (None, None)
    p += 2 * has_res
    o_ref, os_ref = refs[p], refs[p + 1]
    p += 2
    ob_ref, obs_ref = (refs[p], refs[p + 1]) if emit_bf16 else (None, None)
    p += 2 * emit_bf16
    wb_refs = refs[p:]
    contract = (((1,), (1 if w_t else 0,)), ((), ()))

    def apply(x_ref, r_ref, o_ref, ob_ref):
        if len(x_ref.shape) == 3:
            x = jnp.concatenate([x_ref[c] for c in range(x_ref.shape[0])], axis=1).astype(BF16)
        else:
            x = x_ref[...].astype(BF16)
        accs = [lax.dot_general(x, wb_ref[...], contract, preferred_element_type=F32) for wb_ref in wb_refs]
        if norm:
            rstd = _row_rstd(x)
            accs = [acc * rstd for acc in accs]
        out = accs[0] * jax.nn.sigmoid(accs[0]) * accs[1] if swiglu else accs[0]
        if has_res:
            out = r_ref[...] + out
        _store_cols(o_ref, out.astype(o_ref.dtype))
        if emit_bf16:
            ob_ref[...] = out.astype(BF16)

    @pl.when(pl.program_id(m_axis) == 0)
    def _():
        for w_ref, wb_ref in zip(w_refs, wb_refs):
            w = w_ref[...] * g_ref[...] if norm else w_ref[...]
            wb_ref[...] = w.astype(BF16)
        apply(xs_ref, rs_ref, os_ref, obs_ref)

    apply(x_ref, r_ref, o_ref, ob_ref)


def linear(x, xs, ws, w_index, outer, out_shape, out_block, out_index, tn, *, res=None, res_s=None,
           swiglu=False, out_dtype=F32, w_t=False, norm_gain=None, emit_bf16=False, name="linear"):
    if x.ndim == 3:
        m, k = x.shape[1], x.shape[0] * x.shape[2]
        x_spec = lambda tm: pl.BlockSpec((x.shape[0], tm, x.shape[2]), lambda *g: (0, g[nd], 0))
    else:
        m, k = x.shape
        x_spec = lambda tm: pl.BlockSpec((tm, k), lambda *g: (g[nd], 0))
    ms = xs.shape[0]
    nd = len(outer)
    has_res = res is not None
    norm = norm_gain is not None
    bf16_bytes = jnp.dtype(BF16).itemsize
    x_bytes, o_bytes = x.dtype.itemsize, jnp.dtype(out_dtype).itemsize
    row_bytes = (k * x_bytes + tn * (o_bytes + (res.dtype.itemsize if has_res else 0))
                 + (tn * bf16_bytes if emit_bf16 else 0))
    w_bytes = len(ws) * k * tn * (2 * ws[0].dtype.itemsize + bf16_bytes)
    acc_bytes = len(ws) * tn * jnp.dtype(F32).itemsize
    tm = _row_tile(m, w_bytes + 2 * ms * row_bytes, row_bytes, acc_bytes)
    w_block = (None, tn, k) if w_t else (None, k, tn)
    main_map = lambda *g: out_index(*g)
    sample_map = lambda *g: out_index(*g[:nd], 0)
    in_specs = [x_spec(tm), pl.BlockSpec((ms, k), lambda *g: (0, 0))]
    args = [x, xs]
    if norm:
        gain = norm_gain.astype(F32).reshape((1, k) if w_t else (k, 1))
        in_specs.append(pl.BlockSpec(gain.shape, lambda *g: (0, 0)))
        args.append(gain)
    in_specs += [pl.BlockSpec(w_block, lambda *g: w_index(*g[:nd])) for _ in ws]
    args += ws
    if has_res:
        in_specs += [pl.BlockSpec(out_block(tm), main_map), pl.BlockSpec(out_block(ms), sample_map)]
        args += [res, res_s]
    out_specs = [pl.BlockSpec(out_block(tm), main_map), pl.BlockSpec(out_block(ms), sample_map)]
    out_shapes = [jax.ShapeDtypeStruct(out_shape(m), out_dtype), jax.ShapeDtypeStruct(out_shape(ms), out_dtype)]
    if emit_bf16:
        out_specs += [pl.BlockSpec(out_block(tm), main_map), pl.BlockSpec(out_block(ms), sample_map)]
        out_shapes += [jax.ShapeDtypeStruct(out_shape(m), BF16), jax.ShapeDtypeStruct(out_shape(ms), BF16)]
    return pl.pallas_call(
        functools.partial(_linear_kernel, n_w=len(ws), has_res=has_res, swiglu=swiglu, m_axis=nd, w_t=w_t,
                          norm=norm, emit_bf16=emit_bf16),
        grid=(*outer, m // tm),
        in_specs=in_specs,
        out_specs=out_specs,
        out_shape=out_shapes,
        scratch_shapes=[pltpu.VMEM(w_block[1:], BF16) for _ in ws],
        compiler_params=_cparams(("parallel",) * nd + ("arbitrary",)),
        name=name,
    )(*args)


def dense(x, xs, w, layer, *, res=None, res_s=None, out_dtype=F32, n_cols=None, w_t=False,
          norm_gain=None, emit_bf16=False, name="dense"):
    k = x.shape[1] if x.ndim == 2 else x.shape[0] * x.shape[2]
    n = n_cols or w.shape[1 if w_t else 2]
    tn = _pick(n, (1024, 512, 256, 128)) if k <= 2048 else _pick(n, (512, 256, 128))
    w_index = (lambda j: (layer, j, 0)) if w_t else (lambda j: (layer, 0, j))
    return linear(x, xs, [w], w_index, (n // tn,), lambda m: (m, n),
                  lambda tm: (tm, tn), lambda j, i: (i, j), tn, res=res, res_s=res_s,
                  out_dtype=out_dtype, w_t=w_t, norm_gain=norm_gain, emit_bf16=emit_bf16, name=name)


def ffn_up(x, xs, gain, wg, wu, layer):
    n = wg.shape[2]
    tn = _pick(n, (512, 256, 128))
    return linear(x, xs, [wg, wu], lambda j: (layer, 0, j), (n // tn,), lambda m: (m, n),
                  lambda tm: (tm, tn), lambda j, i: (i, j), tn, swiglu=True, out_dtype=BF16,
                  norm_gain=gain, name="ffn_up")


def attn_q_proj(x, xs, gain, w, layer):
    he = A_HEADS * A_HEAD_DIM
    return linear(x, xs, [w], lambda g: (layer, 0, 3 * g), (N_GROUPS,),
                  lambda m: (N_GROUPS * A_HEADS, m, A_HEAD_DIM),
                  lambda tm: (A_HEADS, tm, A_HEAD_DIM), lambda g, i: (g, i, 0), he,
                  norm_gain=gain, name="attn_q_proj")


def _kv_proj_kernel(x_ref, xs_ref, g_ref, w_ref, o_ref, os_ref, f_ref, fs_ref, wb_ref, *,
                    tiles_per_batch, first_kept, keep_blk):
    i = pl.program_id(1)
    nh, e = f_ref.shape[-2:]

    def split_heads(dst_ref, rows):
        flat = dst_ref.reshape(rows.shape[0] * nh, e)
        for h in range(nh):
            flat[pl.ds(h, rows.shape[0], stride=nh), :] = rows[:, h * e:(h + 1) * e]

    def project(x):
        return jnp.dot(x, wb_ref[...], preferred_element_type=F32) * _row_rstd(x)

    @pl.when(i == 0)
    def _():
        wb_ref[...] = (w_ref[...] * g_ref[...]).astype(BF16)
        acc = project(xs_ref[...])
        _store_cols(os_ref, acc)
        split_heads(fs_ref, acc)

    acc = project(x_ref[...])
    _store_cols(o_ref, acc)

    @pl.when(i % tiles_per_batch >= first_kept)
    def _():
        split_heads(f_ref, acc[acc.shape[0] - keep_blk:])


def attn_kv_proj(x, xs, gain, w, layer, group, batch, seq, keep):
    m, k = x.shape
    ms = xs.shape[0]
    he = A_HEADS * A_HEAD_DIM
    tm = _pick(seq, (1024, 512, 256, 128))
    keep_blk = min(keep, tm)
    assert keep % keep_blk == 0 and tm % keep_blk == 0
    tpb = seq // tm
    first_kept = tpb - keep // keep_blk if keep >= tm else tpb - 1

    def kept_map(t, i):
        return (i // tpb, jnp.maximum(i % tpb - first_kept, 0), t, 0, 0)

    return pl.pallas_call(
        functools.partial(_kv_proj_kernel, tiles_per_batch=tpb, first_kept=first_kept, keep_blk=keep_blk),
        grid=(2, m // tm),
        in_specs=[pl.BlockSpec((tm, k), lambda t, i: (i, 0)),
                  pl.BlockSpec((ms, k), lambda t, i: (0, 0)),
                  pl.BlockSpec((k, 1), lambda t, i: (0, 0)),
                  pl.BlockSpec((None, k, he), lambda t, i: (layer, 0, 3 * group + 1 + t))],
        out_specs=[pl.BlockSpec((A_HEADS, tm, A_HEAD_DIM), lambda t, i: (t, i, 0)),
                   pl.BlockSpec((A_HEADS, ms, A_HEAD_DIM), lambda t, i: (t, 0, 0)),
                   pl.BlockSpec((None, keep_blk, None, A_HEADS, A_HEAD_DIM), kept_map),
                   pl.BlockSpec((ms, None, A_HEADS, A_HEAD_DIM), lambda t, i: (0, t, 0, 0))],
        out_shape=[jax.ShapeDtypeStruct((2 * A_HEADS, m, A_HEAD_DIM), F32),
                   jax.ShapeDtypeStruct((2 * A_HEADS, ms, A_HEAD_DIM), F32),
                   jax.ShapeDtypeStruct((batch, keep, 2, A_HEADS, A_HEAD_DIM), F32),
                   jax.ShapeDtypeStruct((ms, 2, A_HEADS, A_HEAD_DIM), F32)],
        scratch_shapes=[pltpu.VMEM((k, he), BF16)],
        compiler_params=_cparams(("parallel", "arbitrary")),
        name="attn_kv_proj",
    )(x, xs, gain.astype(F32).reshape(k, 1), w)


def _gate_kernel(x_ref, xs_ref, g_ref, w_ref, o_ref, os_ref):
    w = (w_ref[...] * g_ref[...]).astype(BF16)
    contract = (((1,), (1,)), ((), ()))

    def project(x):
        return lax.dot_general(x, w, contract, preferred_element_type=F32) * _row_rstd(x)

    @pl.when(pl.program_id(0) == 0)
    def _():
        os_ref[...] = project(xs_ref[...])

    o_ref[...] = project(x_ref[...])


def gate_proj(x, xs, gain, w_t, layer, row0, n):
    m, k = x.shape
    ms = xs.shape[0]
    assert row0 % n == 0
    tm = _pick(m, (512, 256, 128, 64))
    return pl.pallas_call(
        _gate_kernel,
        grid=(m // tm,),
        in_specs=[pl.BlockSpec((tm, k), lambda i: (i, 0)),
                  pl.BlockSpec((ms, k), lambda i: (0, 0)),
                  pl.BlockSpec((1, k), lambda i: (0, 0)),
                  pl.BlockSpec((None, n, k), lambda i: (layer, row0 // n, 0))],
        out_specs=[pl.BlockSpec((tm, n), lambda i: (i, 0)), pl.BlockSpec((ms, n), lambda i: (0, 0))],
        out_shape=[jax.ShapeDtypeStruct((m, n), F32), jax.ShapeDtypeStruct((ms, n), F32)],
        compiler_params=_cparams(("arbitrary",)),
        name="gate_proj",
    )(x, xs, gain.astype(F32).reshape(1, k), w_t)


def _t5_bucket(dist):
    max_exact = NUM_BUCKETS // 2
    d = jnp.maximum(dist.astype(F32), 1.0)
    large = max_exact + jnp.log(d / max_exact) / math.log(MAX_DISTANCE / max_exact) * (NUM_BUCKETS - max_exact)
    large = jnp.minimum(large.astype(jnp.int32), NUM_BUCKETS - 1)
    return jnp.where(dist < max_exact, dist, large)


def _group_bias(rel_bias, g):
    dil = A_GROUPS[g][1]
    buckets = _t5_bucket(jnp.arange(A_STEPS + 1, dtype=jnp.int32) * dil)
    onehot = (buckets[:, None] == jnp.arange(NUM_BUCKETS)[None, :]).astype(F32)
    cols = rel_bias[:, g * A_HEADS:(g + 1) * A_HEADS].astype(F32)
    return jnp.einsum("kb,bh->hk", onehot, cols, precision=lax.Precision.HIGHEST)


def _toeplitz(f, lo, n_rows, n_cols, offset):
    first, last = offset - n_cols + 1, offset + n_rows - 1
    pad_lo, pad_hi = max(lo - first, 0), max(last - (lo + f.shape[-1] - 1), 0)
    fe = jnp.pad(f, [(0, 0)] * (f.ndim - 1) + [(pad_lo, pad_hi)], constant_values=NEG)
    base = lo - pad_lo
    ext = lambda a, b: fe[..., a - base:b - base]
    period = n_rows + n_cols - 1
    w = jnp.concatenate([ext(first, offset + 1)[..., ::-1], ext(offset + 1, last + 1)[..., ::-1]], axis=-1)
    flat = jnp.tile(w, n_rows)[..., :n_rows * (period - 1)]
    return flat.reshape(*f.shape[:-1], n_rows, period - 1)[..., :n_cols]


def _band_tables(biases):
    return jnp.stack([_toeplitz(b, 0, A_BLOCK, 2 * A_BLOCK, A_BLOCK) for b in biases], axis=1)


def _rows(start, size, stride):
    return pl.ds(start, size) if stride == 1 else pl.ds(start, size, stride=stride)


def _masked_logits(q, k, tab, scale):
    s = lax.dot_general(q.astype(BF16), k.astype(BF16), (((1,), (1,)), ((), ())),
                        preferred_element_type=F32) * scale
    return jnp.where(tab > 0.5 * NEG, s + tab, NEG)


def _attn_prompt_kernel(*refs, seq):
    qkv = refs[:3 * N_GROUPS]
    band_ref, y_ref, o_scr, lse_scr = refs[3 * N_GROUPS:]
    scale = A_HEAD_DIM ** -0.5
    q_blk = A_BLOCK

    def logits(g, dil, q_start, k_start, n_keys):
        q_ref, k_ref, v_ref = qkv[3 * g:3 * g + 3]
        q = q_ref[_rows(q_start, q_blk, dil), :]
        k = k_ref[_rows(k_start, n_keys, dil), :]
        v = v_ref[_rows(k_start, n_keys, dil), :]
        tab = band_ref[g] if n_keys == 2 * q_blk else band_ref[g, :, q_blk:]
        return _masked_logits(q, k, tab, scale), v.astype(BF16)

    def finish(g, dil, q_start, s, v):
        m = jnp.max(s, axis=-1, keepdims=True)
        p = jnp.exp(s - m)
        l = jnp.sum(p, axis=-1, keepdims=True)
        o = jnp.dot(p.astype(BF16), v, preferred_element_type=F32)
        rows = _rows(q_start, q_blk, dil)
        o_scr[g, rows, :] = o / l
        lse_scr[g, rows, :] = jnp.broadcast_to(m + jnp.log(l), (q_blk, LANES))

    blocks = []
    for g, (_, dil) in enumerate(A_GROUPS):
        for r in range(dil):
            blocks.append((g, dil, r, r, q_blk))
            for n in range(1, seq // dil // q_blk):
                start = r + dil * q_blk * (n - 1)
                blocks.append((g, dil, start + dil * q_blk, start, 2 * q_blk))
    pending = []
    for g, dil, q_start, k_start, n_keys in blocks:
        s, v = logits(g, dil, q_start, k_start, n_keys)
        pending.append((g, dil, q_start, s, v))
        if len(pending) > LOGITS_AHEAD:
            finish(*pending.pop(0))
    for item in pending:
        finish(*item)

    chunk = 256

    def merge(c, carry):
        rows = pl.ds(pl.multiple_of(c * chunk, chunk), chunk)
        lses = [lse_scr[g, rows, :] for g in range(N_GROUPS)]
        top = functools.reduce(jnp.maximum, lses)
        ws = [jnp.exp(lse - top) for lse in lses]
        num = functools.reduce(jnp.add, [w * o_scr[g, rows, :] for g, w in enumerate(ws)])
        y_ref[rows, :] = (num / functools.reduce(jnp.add, ws)).astype(y_ref.dtype)
        return carry

    lax.fori_loop(0, seq // chunk, merge, 0)


def attention_prompt(q, kv, band, batch, seq):
    e = A_HEAD_DIM
    assert all(seq % (dil * A_BLOCK) == 0 for _, dil in A_GROUPS)
    in_specs, args = [], []
    for g in range(N_GROUPS):
        in_specs.append(pl.BlockSpec((None, seq, e), lambda b, h, g=g: (g * A_HEADS + h, b, 0)))
        in_specs.append(pl.BlockSpec((None, seq, e), lambda b, h: (h, b, 0)))
        in_specs.append(pl.BlockSpec((None, seq, e), lambda b, h: (A_HEADS + h, b, 0)))
        args += [q, kv[g], kv[g]]
    in_specs.append(pl.BlockSpec((None, N_GROUPS, A_BLOCK, 2 * A_BLOCK), lambda b, h: (h, 0, 0, 0)))
    return pl.pallas_call(
        functools.partial(_attn_prompt_kernel, seq=seq),
        grid=(batch, A_HEADS),
        in_specs=in_specs,
        out_specs=pl.BlockSpec((None, seq, e), lambda b, h: (h, b, 0)),
        out_shape=jax.ShapeDtypeStruct((A_HEADS, batch * seq, e), BF16),
        scratch_shapes=[pltpu.VMEM((N_GROUPS, seq, e), F32),
                        pltpu.VMEM((N_GROUPS, seq, LANES), F32)],
        compiler_params=_cparams(("parallel", "parallel")),
        name="attention_prompt",
    )(*args, band)


def _attn_sample_kernel(*refs):
    q_ref = refs[0]
    kv_new = refs[1:1 + N_GROUPS]
    caches = refs[1 + N_GROUPS:1 + 3 * N_GROUPS]
    tables = refs[1 + 3 * N_GROUPS:1 + 5 * N_GROUPS]
    y_ref = refs[1 + 5 * N_GROUPS]
    scale = A_HEAD_DIM ** -0.5
    e = A_HEAD_DIM

    def head_rows(ref, h):
        rows = math.prod(ref.shape[:-2])
        return ref.reshape(rows * A_HEADS, e)[pl.ds(h, rows, stride=A_HEADS), :]

    for h in range(A_HEADS):
        lses, outs = [], []
        for g in range(N_GROUPS):
            q = q_ref[g * A_HEADS + h]
            k_new = kv_new[g][h]
            v_new = kv_new[g][A_HEADS + h]
            k_old = head_rows(caches[2 * g], h)
            v_old = head_rows(caches[2 * g + 1], h)
            s_old = _masked_logits(q, k_old, tables[2 * g][h], scale)
            s_new = _masked_logits(q, k_new, tables[2 * g + 1][h], scale)
            m = jnp.maximum(jnp.max(s_old, axis=-1, keepdims=True), jnp.max(s_new, axis=-1, keepdims=True))
            p_old = jnp.exp(s_old - m)
            p_new = jnp.exp(s_new - m)
            l = jnp.sum(p_old, axis=-1, keepdims=True) + jnp.sum(p_new, axis=-1, keepdims=True)
            o = (jnp.dot(p_old.astype(BF16), v_old.astype(BF16), preferred_element_type=F32)
                 + jnp.dot(p_new.astype(BF16), v_new.astype(BF16), preferred_element_type=F32))
            outs.append(o / l)
            lses.append(m + jnp.log(l))
        top = functools.reduce(jnp.maximum, lses)
        ws = [jnp.exp(lse - top) for lse in lses]
        num = functools.reduce(jnp.add, [w * o for w, o in zip(ws, outs)])
        y_ref[:, h * e:(h + 1) * e] = (num / functools.reduce(jnp.add, ws)).astype(y_ref.dtype)


def attention_sample(q, kv_new, caches, layer, biases, batch, t_new):
    he = A_HEADS * A_HEAD_DIM
    in_specs = [pl.BlockSpec((N_GROUPS * A_HEADS, t_new, A_HEAD_DIM), lambda b: (0, b, 0))]
    args = [q]
    for g in range(N_GROUPS):
        in_specs.append(pl.BlockSpec((2 * A_HEADS, t_new, A_HEAD_DIM), lambda b: (0, b, 0)))
        args.append(kv_new[g])
    tables = []
    for g, (_, dil) in enumerate(A_GROUPS):
        c = caches[g]
        n_past = c.shape[2]
        bg = biases[g]
        by_dist = jnp.concatenate([bg[..., None], jnp.full(bg.shape + (dil - 1,), NEG, F32)], axis=-1)
        by_dist = by_dist.reshape(bg.shape[0], -1)[:, :A_STEPS * dil + 1]
        tab = _toeplitz(by_dist, 0, t_new, n_past + t_new, n_past)
        tab_old, tab_new = tab[:, :, :n_past], tab[:, :, n_past:]
        if dil % t_new == 0 and n_past % dil == 0:
            c = c.reshape(c.shape[0], batch, n_past // dil, dil, 2, A_HEADS, A_HEAD_DIM)
            for t in range(2):
                in_specs.append(pl.BlockSpec((None, None, n_past // dil, t_new, None, A_HEADS, A_HEAD_DIM),
                                             lambda b, t=t: (layer, b, 0, 0, t, 0, 0)))
                args.append(c)
            tab_old = tab_old.reshape(A_HEADS, t_new, n_past // dil, dil)[..., :t_new]
            tab_old = tab_old.reshape(A_HEADS, t_new, (n_past // dil) * t_new)
        else:
            for t in range(2):
                in_specs.append(pl.BlockSpec((None, None, n_past, None, A_HEADS, A_HEAD_DIM),
                                             lambda b, t=t: (layer, b, 0, t, 0, 0)))
                args.append(c)
        tables += [tab_old, tab_new]
    for tab in tables:
        in_specs.append(pl.BlockSpec(tab.shape, lambda b: (0, 0, 0)))
        args.append(tab)
    return pl.pallas_call(
        _attn_sample_kernel,
        grid=(batch,),
        in_specs=in_specs,
        out_specs=pl.BlockSpec((t_new, he), lambda b: (b, 0)),
        out_shape=jax.ShapeDtypeStruct((batch * t_new, he), F32),
        compiler_params=_cparams(("parallel",)),
        name="attention_sample",
    )(*args)


def _log_sigmoid(x):
    return jnp.minimum(x, 0.0) - jnp.log1p(jnp.exp(-jnp.abs(x)))


def _mlstm_kernel(q_ref, k_ref, v_ref, og_ref, gc_ref, gr_ref, bc_ref, br_ref, gn_ref,
                  c0_ref, n0_ref, m0_ref,
                  h_ref, c_out, n_out, m_out,
                  c_scr, n_scr, m_scr, *, chunk):
    c = pl.program_id(1)
    nh, dk, dv = c_scr.shape

    @pl.when(c == 0)
    def _():
        c_scr[...] = c0_ref[...].astype(F32)
        n_scr[...] = n0_ref[...].astype(F32)
        m_scr[...] = m0_ref[...].astype(F32)

    gc = gc_ref[...] + bc_ref[...]
    gr = gr_ref[...] + br_ref[...]
    logf_c = _log_sigmoid(gc[:, nh:])
    logf_r = _log_sigmoid(gr[nh:, :])
    qi = lax.broadcasted_iota(jnp.int32, (chunk, chunk), 0)
    si = lax.broadcasted_iota(jnp.int32, (chunk, chunk), 1)
    causal = si <= qi
    upper = qi <= si
    m_all, n_all = m_scr[...], n_scr[...]
    m_news, n_news = [], []

    def state_stage(h):
        i_col, i_row = gc[:, h:h + 1], gr[h:h + 1, :]
        logf_col, logf_row = logf_c[:, h:h + 1], logf_r[h:h + 1, :]
        b_col = jnp.sum(jnp.where(causal, logf_row, 0.0), axis=1, keepdims=True)
        b_row = jnp.sum(jnp.where(upper, logf_col, 0.0), axis=0, keepdims=True)
        b_last = jnp.sum(logf_row, axis=1, keepdims=True)

        m_prev = m_all[h:h + 1, :]
        a = b_col + m_prev
        dmat = jnp.where(causal, b_col - b_row + i_row, NEG)
        m_t = jnp.maximum(a, jnp.max(dmat, axis=1, keepdims=True))
        w_inter = jnp.exp(a - m_t)

        qf = q_ref[:, h * dk:(h + 1) * dk].astype(F32)
        q = qf.astype(BF16)
        kf = k_ref[:, h * dk:(h + 1) * dk].astype(F32) * (dk ** -0.5)
        k = kf.astype(BF16)
        v = v_ref[:, h * dv:(h + 1) * dv].astype(BF16)
        c_prev = c_scr[h]
        n_prev = n_all[h:h + 1, :]
        qk = lax.dot_general(q, k, (((1,), (1,)), ((), ())), preferred_element_type=F32)
        qc = jnp.dot(q, c_prev.astype(BF16), preferred_element_type=F32)
        qn = jnp.sum(qf * n_prev, axis=1, keepdims=True)

        m_new = jnp.maximum(b_last + m_prev, jnp.max(b_last - b_row + i_row, axis=1, keepdims=True))
        w_state = jnp.exp(b_last - b_col + i_col - m_new)
        decay = jnp.exp(b_last + m_prev - m_new)
        kw = (w_state * kf).astype(BF16)
        c_scr[h] = decay * c_prev + lax.dot_general(kw, v, (((0,), (0,)), ((), ())),
                                                    preferred_element_type=F32)
        n_news.append(decay * n_prev + jnp.sum(w_state * kf, axis=0, keepdims=True))
        m_news.append(m_new)
        return h, qk, dmat, m_t, w_inter, qc, qn, v

    def output_stage(h, qk, dmat, m_t, w_inter, qc, qn, v):
        s = qk * jnp.exp(dmat - m_t)
        num = w_inter * qc + jnp.dot(s.astype(BF16), v, preferred_element_type=F32)
        den = w_inter * qn + jnp.sum(s, axis=1, keepdims=True)
        hh = num / jnp.maximum(jnp.abs(den), jnp.exp(-m_t))
        ht = hh * jax.nn.sigmoid(og_ref[:, h * dv:(h + 1) * dv].astype(F32))
        hn = ht * lax.rsqrt(jnp.mean(ht * ht, axis=-1, keepdims=True) + EPS)
        h_ref[:, h * dv:(h + 1) * dv] = (hn * gn_ref[:, h * dv:(h + 1) * dv]).astype(h_ref.dtype)

    pending = []
    for h in range(nh):
        pending.append(state_stage(h))
        if len(pending) > HEADS_AHEAD:
            output_stage(*pending.pop(0))
    for item in pending:
        output_stage(*item)

    n_scr[...] = jnp.concatenate(n_news, axis=0)
    m_scr[...] = jnp.concatenate(m_news, axis=0)

    @pl.when(c == pl.num_programs(1) - 1)
    def _():
        c_out[...] = c_scr[...].astype(c_out.dtype)
        n_out[...] = n_scr[...].astype(n_out.dtype)
        m_out[...] = m_scr[...].astype(m_out.dtype)


def mlstm(z, gates, b_gate, g_norm, state, batch, seq, chunk, out_dtype):
    c0, n0, m0 = state
    nh = B_HEADS
    dk, dv = c0.shape[-2], c0.shape[-1]
    nc = seq // chunk
    qk_w, v_w = nh * dk, nh * dv
    assert v_w % qk_w == 0
    gcol = gates.reshape(batch * nc, chunk, 2 * nh)
    grow = jnp.swapaxes(gcol, 1, 2)
    row = lambda b, c: b * nc + c
    state_specs = [
        pl.BlockSpec((None, nh, dk, dv), lambda b, c: (b, 0, 0, 0)),
        pl.BlockSpec((None, nh, dk), lambda b, c: (b, 0, 0)),
        pl.BlockSpec((None, nh, 1), lambda b, c: (b, 0, 0)),
    ]
    in_specs = [
        pl.BlockSpec((chunk, qk_w), lambda b, c: (row(b, c), 0)),
        pl.BlockSpec((chunk, qk_w), lambda b, c: (row(b, c), 1)),
        pl.BlockSpec((chunk, v_w), lambda b, c: (row(b, c), 2 * qk_w // v_w)),
        pl.BlockSpec((chunk, v_w), lambda b, c: (row(b, c), 2 * qk_w // v_w + 1)),
        pl.BlockSpec((None, chunk, 2 * nh), lambda b, c: (row(b, c), 0, 0)),
        pl.BlockSpec((None, 2 * nh, chunk), lambda b, c: (row(b, c), 0, 0)),
        pl.BlockSpec((1, 2 * nh), lambda b, c: (0, 0)),
        pl.BlockSpec((2 * nh, 1), lambda b, c: (0, 0)),
        pl.BlockSpec((1, v_w), lambda b, c: (0, 0)),
    ] + state_specs
    out_shape = [
        jax.ShapeDtypeStruct((batch * seq, v_w), out_dtype),
        jax.ShapeDtypeStruct((batch, nh, dk, dv), F32),
        jax.ShapeDtypeStruct((batch, nh, dk), F32),
        jax.ShapeDtypeStruct((batch, nh, 1), F32),
    ]
    bg = b_gate.astype(F32)
    hn, c_new, n_new, m_new = pl.pallas_call(
        functools.partial(_mlstm_kernel, chunk=chunk),
        grid=(batch, nc),
        in_specs=in_specs,
        out_specs=[pl.BlockSpec((chunk, v_w), lambda b, c: (row(b, c), 0))] + state_specs,
        out_shape=out_shape,
        scratch_shapes=[pltpu.VMEM((nh, dk, dv), F32), pltpu.VMEM((nh, dk), F32), pltpu.VMEM((nh, 1), F32)],
        compiler_params=_cparams(("parallel", "arbitrary")),
        name="mlstm",
    )(z, z, z, z, gcol, grow, bg.reshape(1, 2 * nh), bg.reshape(2 * nh, 1),
      g_norm.astype(F32).reshape(1, v_w),
      c0.reshape(batch, nh, dk, dv), n0.reshape(batch, nh, dk), m0.reshape(batch, nh, 1))
    return hn, (c_new, n_new, m_new.reshape(batch, nh))


def kernel(x_prompt, x_sample, cache_kv_w128, cache_kv_w512, cache_kv_w2048, state_C, state_n, state_m,
           norm_mix, norm_ffn, norm_final, rel_bias, w_a_in, w_a_out, w_b_in, b_b_gate, b_b_norm, w_b_out,
           w_ffn_gate, w_ffn_up, w_ffn_down):
    bp, seq, d_model = x_prompt.shape
    bs, t_new, _ = x_sample.shape
    depth = norm_mix.shape[0]
    nh, dk, dv = state_C.shape[2], state_C.shape[3], state_C.shape[4]
    main_w = 2 * nh * dk + 2 * nh * dv
    caches = (cache_kv_w128, cache_kv_w512, cache_kv_w2048)
    w_b_in_t = jnp.swapaxes(w_b_in, 1, 2)

    biases =[_group_bias(rel_bias, g) for g in range(N_GROUPS)]
    band = _band_tables(biases)

    xp = x_prompt.reshape(bp * seq, d_model)
    xs = x_sample.reshape(bs * t_new, d_model)
    xbp, xbs = round_bf16(xp, xs)
    kv_p =[[] for _ in range(N_GROUPS)]
    kv_s = [[] for _ in range(N_GROUPS)]
    st_p, st_s = [], []
    for layer in range(depth):
        j = layer // 2
        gain = norm_mix[layer]
        if layer % 2 == 0:
            qp, qs = attn_q_proj(xbp, xbs, gain, w_a_in, j)
            kvp, kvs = [], []
            for g, (window, _) in enumerate(A_GROUPS):
                kp, ks, fp, fs = attn_kv_proj(xbp, xbs, gain, w_a_in, j, g, bp, seq, min(window, seq))
                kvp.append(kp)
                kvs.append(ks)
                kv_p[g].append(fp)
                kv_s[g].append(fs.reshape(bs, t_new, 2, A_HEADS, A_HEAD_DIM))
            yp = attention_prompt(qp, kvp, band, bp, seq)
            ys = attention_sample(qs, kvs, caches, j, biases, bs, t_new)
            xp, xs, xbp, xbs = dense(yp, ys, w_a_out, j, res=xp, res_s=xs, emit_bf16=True,
                                     name="attn_out_proj")
        else:
            zero = (jnp.zeros((bp, nh, dk, dv), F32), jnp.zeros((bp, nh, dk), F32), jnp.zeros((bp, nh), F32))
            past = (state_C[j], state_n[j], state_m[j])
            zp, zs = dense(xbp, xbs, w_b_in_t, j, n_cols=main_w, w_t=True, norm_gain=gain,
                           name="mlstm_in_proj")
            gp, gs = gate_proj(xbp, xbs, gain, w_b_in_t, j, main_w, 2 * nh)
            hnp, sp = mlstm(zp, gp, b_b_gate[j], b_b_norm[j], zero, bp, seq, min(PROMPT_CHUNK, seq), BF16)
            hns, ss = mlstm(zs, gs, b_b_gate[j], b_b_norm[j], past, bs, t_new, t_new, F32)
            xp, xs, xbp, xbs = dense(hnp, hns, w_b_out, j, res=xp, res_s=xs, emit_bf16=True,
                                     name="mlstm_out_proj")
            st_p.append(sp)
            st_s.append(ss)
        ap, as_ = ffn_up(xbp, xbs, norm_ffn[layer], w_ffn_gate, w_ffn_up, layer)
        if layer + 1 < depth:
            xp, xs, xbp, xbs = dense(ap, as_, w_ffn_down, layer, res=xp, res_s=xs, emit_bf16=True,
                                     name="ffn_down")
        else:
            xp, xs = dense(ap, as_, w_ffn_down, layer, res=xp, res_s=xs, name="ffn_down")

    y_prompt, y_sample = rmsnorm(xp, xs, norm_final, F32)
    y_prompt = y_prompt.reshape(bp, seq, d_model)
    y_sample = y_sample.reshape(bs, t_new, d_model)
    kvd = cache_kv_w128.dtype
    sd = state_C.dtype
    kv_out_p = [jnp.stack(kv_p[g]).astype(kvd) for g in range(N_GROUPS)]
    kv_out_s = [jnp.stack(kv_s[g]).astype(kvd) for g in range(N_GROUPS)]
    states_p = [jnp.stack([s[i] for s in st_p]).astype(sd) for i in range(3)]
    states_s = [jnp.stack([s[i] for s in st_s]).astype(sd) for i in range(3)]
    return (y_prompt, y_sample, *kv_out_p, *kv_out_s, *states_p, *states_s)
```

```python
import functools
import math

import jax
import jax.numpy as jnp
from jax import lax
from jax.experimental import pallas as pl
from jax.experimental.pallas import tpu as pltpu

F32 = jnp.float32
BF16 = jnp.bfloat16

EPS = 1e-6
NEG = -1e30

A_GROUPS = ((128, 1), (512, 4), (2048, 16))
N_GROUPS = len(A_GROUPS)
A_HEADS = 8
A_HEAD_DIM = 128
A_STEPS = 128
A_BLOCK = 128
NUM_BUCKETS = 32
MAX_DISTANCE = 2048
B_HEADS = 8
LANES = 128
VMEM_LIMIT = 56 * 1024 * 1024
PROMPT_CHUNK = 256
HEADS_AHEAD = 1
LOGITS_AHEAD = 3


def _cparams(sem):
    return pltpu.CompilerParams(dimension_semantics=sem, vmem_limit_bytes=VMEM_LIMIT)


def _pick(n, cands):
    for c in cands:
        if n % c == 0:
            return c
    return n


def _row_tile(m, fixed_bytes, block_bytes_per_row, temp_bytes_per_row):
    for tm in (1024, 512, 256, 128, 64):
        if m % tm == 0 and fixed_bytes + tm * (2 * block_bytes_per_row + temp_bytes_per_row) <= VMEM_LIMIT:
            return tm
    return m


def _rmsnorm_kernel(x_ref, xs_ref, g_ref, o_ref, os_ref):
    def norm(x):
        return x * lax.rsqrt(jnp.mean(x * x, axis=-1, keepdims=True) + EPS) * g_ref[...]

    @pl.when(pl.program_id(0) == 0)
    def _():
        os_ref[...] = norm(xs_ref[...]).astype(os_ref.dtype)

    o_ref[...] = norm(x_ref[...]).astype(o_ref.dtype)


def _round_kernel(x_ref, xs_ref, o_ref, os_ref):
    @pl.when(pl.program_id(0) == 0)
    def _():
        os_ref[...] = xs_ref[...].astype(os_ref.dtype)

    o_ref[...] = x_ref[...].astype(o_ref.dtype)


def round_bf16(x, xs):
    m, d = x.shape
    ms = xs.shape[0]
    tm = _pick(m, (512, 256, 128, 64))
    return pl.pallas_call(
        _round_kernel,
        grid=(m // tm,),
        in_specs=[pl.BlockSpec((tm, d), lambda i: (i, 0)), pl.BlockSpec((ms, d), lambda i: (0, 0))],
        out_specs=[pl.BlockSpec((tm, d), lambda i: (i, 0)), pl.BlockSpec((ms, d), lambda i: (0, 0))],
        out_shape=[jax.ShapeDtypeStruct((m, d), BF16), jax.ShapeDtypeStruct((ms, d), BF16)],
        compiler_params=_cparams(("arbitrary",)),
        name="round_bf16",
    )(x, xs)


def rmsnorm(x, xs, g, out_dtype):
    m, d = x.shape
    ms = xs.shape[0]
    tm = _pick(m, (512, 256, 128, 64))
    return pl.pallas_call(
        _rmsnorm_kernel,
        grid=(m // tm,),
        in_specs=[pl.BlockSpec((tm, d), lambda i: (i, 0)),
                  pl.BlockSpec((ms, d), lambda i: (0, 0)),
                  pl.BlockSpec((1, d), lambda i: (0, 0))],
        out_specs=[pl.BlockSpec((tm, d), lambda i: (i, 0)),
                   pl.BlockSpec((ms, d), lambda i: (0, 0))],
        out_shape=[jax.ShapeDtypeStruct((m, d), out_dtype), jax.ShapeDtypeStruct((ms, d), out_dtype)],
        compiler_params=_cparams(("arbitrary",)),
        name="rmsnorm",
    )(x, xs, g.reshape(1, d))


def _store_cols(o_ref, val):
    if len(o_ref.shape) == 2:
        o_ref[...] = val
    else:
        e = o_ref.shape[-1]
        for c in range(o_ref.shape[0]):
            o_ref[c] = val[:, c * e:(c + 1) * e]


def _row_rstd(x):
    xf = x.astype(F32)
    return lax.rsqrt(jnp.mean(xf * xf, axis=-1, keepdims=True) + EPS)


def _linear_kernel(*refs, n_w, has_res, swiglu, m_axis, w_t, norm, emit_bf16):
    x_ref, xs_ref = refs[:2]
    p = 2
    g_ref = refs[p] if norm else None
    p += norm
    w_refs = refs[p:p + n_w]
    p += n_w
    r_ref, rs_ref = (refs[p], refs[p + 1]) if has_res else (None, None)
    p += 2 * has_res
    o_ref, os_ref = refs[p], refs[p + 1]
    p += 2
    ob_ref, obs_ref = (refs[p], refs[p + 1]) if emit_bf16 else (None, None)
    p += 2 * emit_bf16
    wb_refs = refs[p:]
    contract = (((1,), (1 if w_t else 0,)), ((), ()))

    def apply(x_ref, r_ref, o_ref, ob_ref):
        if len(x_ref.shape) == 3:
            x = jnp.concatenate([x_ref[c] for c in range(x_ref.shape[0])], axis=1).astype(BF16)
        else:
            x = x_ref[...].astype(BF16)
        accs = [lax.dot_general(x, wb_ref[...], contract, preferred_element_type=F32) for wb_ref in wb_refs]
        if norm:
            rstd = _row_rstd(x)
            accs = [acc * rstd for acc in accs]
        out = accs[0] * jax.nn.sigmoid(accs[0]) * accs[1] if swiglu else accs[0]
        if has_res:
            out = r_ref[...] + out
        _store_cols(o_ref, out.astype(o_ref.dtype))
        if emit_bf16:
            ob_ref[...] = out.astype(BF16)

    @pl.when(pl.program_id(m_axis) == 0)
    def _():
        for w_ref, wb_ref in zip(w_refs, wb_refs):
            w = w_ref[...] * g_ref[...] if norm else w_ref[...]
            wb_ref[...] = w.astype(BF16)
        apply(xs_ref, rs_ref, os_ref, obs_ref)

    apply(x_ref, r_ref, o_ref, ob_ref)


def linear(x, xs, ws, w_index, outer, out_shape, out_block, out_index, tn, *, res=None, res_s=None,
           swiglu=False, out_dtype=F32, w_t=False, norm_gain=None, emit_bf16=False, name="linear"):
    if x.ndim == 3:
        m, k = x.shape[1], x.shape[0] * x.shape[2]
        x_spec = lambda tm: pl.BlockSpec((x.shape[0], tm, x.shape[2]), lambda *g: (0, g[nd], 0))
    else:
        m, k = x.shape
        x_spec = lambda tm: pl.BlockSpec((tm, k), lambda *g: (g[nd], 0))
    ms = xs.shape[0]
    nd = len(outer)
    has_res = res is not None
    norm = norm_gain is not None
    bf16_bytes = jnp.dtype(BF16).itemsize
    x_bytes, o_bytes = x.dtype.itemsize, jnp.dtype(out_dtype).itemsize
    row_bytes = (k * x_bytes + tn * (o_bytes + (res.dtype.itemsize if has_res else 0))
                 + (tn * bf16_bytes if emit_bf16 else 0))
    w_bytes = len(ws) * k * tn * (2 * ws[0].dtype.itemsize + bf16_bytes)
    acc_bytes = len(ws) * tn * jnp.dtype(F32).itemsize
    tm = _row_tile(m, w_bytes + 2 * ms * row_bytes, row_bytes, acc_bytes)
    w_block = (None, tn, k) if w_t else (None, k, tn)
    main_map = lambda *g: out_index(*g)
    sample_map = lambda *g: out_index(*g[:nd], 0)
    in_specs = [x_spec(tm), pl.BlockSpec((ms, k), lambda *g: (0, 0))]
    args = [x, xs]
    if norm:
        gain = norm_gain.astype(F32).reshape((1, k) if w_t else (k, 1))
        in_specs.append(pl.BlockSpec(gain.shape, lambda *g: (0, 0)))
        args.append(gain)
    in_specs += [pl.BlockSpec(w_block, lambda *g: w_index(*g[:nd])) for _ in ws]
    args += ws
    if has_res:
        in_specs += [pl.BlockSpec(out_block(tm), main_map), pl.BlockSpec(out_block(ms), sample_map)]
        args += [res, res_s]
    out_specs = [pl.BlockSpec(out_block(tm), main_map), pl.BlockSpec(out_block(ms), sample_map)]
    out_shapes = [jax.ShapeDtypeStruct(out_shape(m), out_dtype), jax.ShapeDtypeStruct(out_shape(ms), out_dtype)]
    if emit_bf16:
        out_specs += [pl.BlockSpec(out_block(tm), main_map), pl.BlockSpec(out_block(ms), sample_map)]
        out_shapes += [jax.ShapeDtypeStruct(out_shape(m), BF16), jax.ShapeDtypeStruct(out_shape(ms), BF16)]
    return pl.pallas_call(
        functools.partial(_linear_kernel, n_w=len(ws), has_res=has_res, swiglu=swiglu, m_axis=nd, w_t=w_t,
                          norm=norm, emit_bf16=emit_bf16),
        grid=(*outer, m // tm),
        in_specs=in_specs,
        out_specs=out_specs,
        out_shape=out_shapes,
        scratch_shapes=[pltpu.VMEM(w_block[1:], BF16) for _ in ws],
        compiler_params=_cparams(("parallel",) * nd + ("arbitrary",)),
        name=name,
    )(*args)


def dense(x, xs, w, layer, *, res=None, res_s=None, out_dtype=F32, n_cols=None, w_t=False,
          norm_gain=None, emit_bf16=False, name="dense"):
    k = x.shape[1] if x.ndim == 2 else x.shape[0] * x.shape[2]
    n = n_cols or w.shape[1 if w_t else 2]
    tn = _pick(n, (1024, 512, 256, 128)) if k <= 2048 else _pick(n, (512, 256, 128))
    w_index = (lambda j: (layer, j, 0)) if w_t else (lambda j: (layer, 0, j))
    return linear(x, xs, [w], w_index, (n // tn,), lambda m: (m, n),
                  lambda tm: (tm, tn), lambda j, i: (i, j), tn, res=res, res_s=res_s,
                  out_dtype=out_dtype, w_t=w_t, norm_gain=norm_gain, emit_bf16=emit_bf16, name=name)


def ffn_up(x, xs, gain, wg, wu, layer):
    n = wg.shape[2]
    tn = _pick(n, (512, 256, 128))
    return linear(x, xs, [wg, wu], lambda j: (layer, 0, j), (n // tn,), lambda m: (m, n),
                  lambda tm: (tm, tn), lambda j, i: (i, j), tn, swiglu=True, out_dtype=BF16,
                  norm_gain=gain, name="ffn_up")


def attn_q_proj(x, xs, gain, w, layer):
    he = A_HEADS * A_HEAD_DIM
    return linear(x, xs, [w], lambda g: (layer, 0, 3 * g), (N_GROUPS,),
                  lambda m: (N_GROUPS * A_HEADS, m, A_HEAD_DIM),
                  lambda tm: (A_HEADS, tm, A_HEAD_DIM), lambda g, i: (g, i, 0), he,
                  norm_gain=gain, name="attn_q_proj")


def _kv_proj_kernel(x_ref, xs_ref, g_ref, w_ref, o_ref, os_ref, f_ref, fs_ref, wb_ref, *,
                    tiles_per_batch, first_kept, keep_blk):
    i = pl.program_id(1)
    nh, e = f_ref.shape[-2:]

    def split_heads(dst_ref, rows):
        flat = dst_ref.reshape(rows.shape[0] * nh, e)
        for h in range(nh):
            flat[pl.ds(h, rows.shape[0], stride=nh), :] = rows[:, h * e:(h + 1) * e]

    def project(x):
        return jnp.dot(x, wb_ref[...], preferred_element_type=F32) * _row_rstd(x)

    @pl.when(i == 0)
    def _():
        wb_ref[...] = (w_ref[...] * g_ref[...]).astype(BF16)
        acc = project(xs_ref[...])
        _store_cols(os_ref, acc)
        split_heads(fs_ref, acc)

    acc = project(x_ref[...])
    _store_cols(o_ref, acc)

    if first_kept == 0:
        split_heads(f_ref, acc[acc.shape[0] - keep_blk:])
    else:
        @pl.when(i % tiles_per_batch >= first_kept)
        def _():
            split_heads(f_ref, acc[acc.shape[0] - keep_blk:])


def attn_kv_proj(x, xs, gain, w, layer, group, batch, seq, keep):
    m, k = x.shape
    ms = xs.shape[0]
    he = A_HEADS * A_HEAD_DIM
    tm = _pick(seq, (1024, 512, 256, 128))
    keep_blk = min(keep, tm)
    assert keep % keep_blk == 0 and tm % keep_blk == 0
    tpb = seq // tm
    first_kept = tpb - keep // keep_blk if keep >= tm else tpb - 1

    def kept_map(t, i):
        return (i // tpb, jnp.maximum(i % tpb - first_kept, 0), t, 0, 0)

    return pl.pallas_call(
        functools.partial(_kv_proj_kernel, tiles_per_batch=tpb, first_kept=first_kept, keep_blk=keep_blk),
        grid=(2, m // tm),
        in_specs=[pl.BlockSpec((tm, k), lambda t, i: (i, 0)),
                  pl.BlockSpec((ms, k), lambda t, i: (0, 0)),
                  pl.BlockSpec((k, 1), lambda t, i: (0, 0)),
                  pl.BlockSpec((None, k, he), lambda t, i: (layer, 0, 3 * group + 1 + t))],
        out_specs=[pl.BlockSpec((A_HEADS, tm, A_HEAD_DIM), lambda t, i: (t, i, 0)),
                   pl.BlockSpec((A_HEADS, ms, A_HEAD_DIM), lambda t, i: (t, 0, 0)),
                   pl.BlockSpec((None, keep_blk, None, A_HEADS, A_HEAD_DIM), kept_map),
                   pl.BlockSpec((ms, None, A_HEADS, A_HEAD_DIM), lambda t, i: (0, t, 0, 0))],
        out_shape=[jax.ShapeDtypeStruct((2 * A_HEADS, m, A_HEAD_DIM), F32),
                   jax.ShapeDtypeStruct((2 * A_HEADS, ms, A_HEAD_DIM), F32),
                   jax.ShapeDtypeStruct((batch, keep, 2, A_HEADS, A_HEAD_DIM), F32),
                   jax.ShapeDtypeStruct((ms, 2, A_HEADS, A_HEAD_DIM), F32)],
        scratch_shapes=[pltpu.VMEM((k, he), BF16)],
        compiler_params=_cparams(("parallel", "arbitrary")),
        name="attn_kv_proj",
    )(x, xs, gain.astype(F32).reshape(k, 1), w)


def _gate_kernel(x_ref, xs_ref, g_ref, w_ref, o_ref, os_ref):
    w = (w_ref[...] * g_ref[...]).astype(BF16)
    contract = (((1,), (1,)), ((), ()))

    def project(x):
        return lax.dot_general(x, w, contract, preferred_element_type=F32) * _row_rstd(x)

    @pl.when(pl.program_id(0) == 0)
    def _():
        os_ref[...] = project(xs_ref[...])

    o_ref[...] = project(x_ref[...])


def gate_proj(x, xs, gain, w_t, layer, row0, n):
    m, k = x.shape
    ms = xs.shape[0]
    assert row0 % n == 0
    tm = _pick(m, (512, 256, 128, 64))
    return pl.pallas_call(
        _gate_kernel,
        grid=(m // tm,),
        in_specs=[pl.BlockSpec((tm, k), lambda i: (i, 0)),
                  pl.BlockSpec((ms, k), lambda i: (0, 0)),
                  pl.BlockSpec((1, k), lambda i: (0, 0)),
                  pl.BlockSpec((None, n, k), lambda i: (layer, row0 // n, 0))],
        out_specs=[pl.BlockSpec((tm, n), lambda i: (i, 0)), pl.BlockSpec((ms, n), lambda i: (0, 0))],
        out_shape=[jax.ShapeDtypeStruct((m, n), F32), jax.ShapeDtypeStruct((ms, n), F32)],
        compiler_params=_cparams(("arbitrary",)),
        name="gate_proj",
    )(x, xs, gain.astype(F32).reshape(1, k), w_t)


def _t5_bucket(dist):
    max_exact = NUM_BUCKETS // 2
    d = jnp.maximum(dist.astype(F32), 1.0)
    large = max_exact + jnp.log(d / max_exact) / math.log(MAX_DISTANCE / max_exact) * (NUM_BUCKETS - max_exact)
    large = jnp.minimum(large.astype(jnp.int32), NUM_BUCKETS - 1)
    return jnp.where(dist < max_exact, dist, large)


def _group_bias(rel_bias, g):
    dil = A_GROUPS[g][1]
    buckets = _t5_bucket(jnp.arange(A_STEPS + 1, dtype=jnp.int32) * dil)
    onehot = (buckets[:, None] == jnp.arange(NUM_BUCKETS)[None, :]).astype(F32)
    cols = rel_bias[:, g * A_HEADS:(g + 1) * A_HEADS].astype(F32)
    return jnp.einsum("kb,bh->hk", onehot, cols, precision=lax.Precision.HIGHEST)


def _toeplitz(f, lo, n_rows, n_cols, offset):
    first, last = offset - n_cols + 1, offset + n_rows - 1
    pad_lo, pad_hi = max(lo - first, 0), max(last - (lo + f.shape[-1] - 1), 0)
    fe = jnp.pad(f, [(0, 0)] * (f.ndim - 1) + [(pad_lo, pad_hi)], constant_values=NEG)
    base = lo - pad_lo
    ext = lambda a, b: fe[..., a - base:b - base]
    period = n_rows + n_cols - 1
    w = jnp.concatenate([ext(first, offset + 1)[..., ::-1], ext(offset + 1, last + 1)[..., ::-1]], axis=-1)
    flat = jnp.tile(w, n_rows)[..., :n_rows * (period - 1)]
    return flat.reshape(*f.shape[:-1], n_rows, period - 1)[..., :n_cols]


def _band_tables(biases):
    return jnp.stack([_toeplitz(b, 0, A_BLOCK, 2 * A_BLOCK, A_BLOCK) for b in biases], axis=1)


def _rows(start, size, stride):
    return pl.ds(start, size) if stride == 1 else pl.ds(start, size, stride=stride)


def _masked_logits(q, k, tab, scale):
    s = lax.dot_general(q.astype(BF16), k.astype(BF16), (((1,), (1,)), ((), ())),
                        preferred_element_type=F32) * scale
    return jnp.where(tab > 0.5 * NEG, s + tab, NEG)


def _attn_prompt_kernel(*refs, seq):
    qkv = refs[:3 * N_GROUPS]
    band_ref, y_ref, o_scr, lse_scr = refs[3 * N_GROUPS:]
    scale = A_HEAD_DIM ** -0.5
    q_blk = A_BLOCK

    def logits(g, dil, q_start, k_start, n_keys):
        q_ref, k_ref, v_ref = qkv[3 * g:3 * g + 3]
        q = q_ref[_rows(q_start, q_blk, dil), :]
        k = k_ref[_rows(k_start, n_keys, dil), :]
        v = v_ref[_rows(k_start, n_keys, dil), :]
        tab = band_ref[g] if n_keys == 2 * q_blk else band_ref[g, :, q_blk:]
        return _masked_logits(q, k, tab, scale), v.astype(BF16)

    def finish(g, dil, q_start, s, v):
        m = jnp.max(s, axis=-1, keepdims=True)
        p = jnp.exp(s - m)
        l = jnp.sum(p, axis=-1, keepdims=True)
        o = jnp.dot(p.astype(BF16), v, preferred_element_type=F32)
        rows = _rows(q_start, q_blk, dil)
        o_scr[g, rows, :] = o / l
        lse_scr[g, rows, :] = jnp.broadcast_to(m + jnp.log(l), (q_blk, LANES))

    blocks = []
    for g, (_, dil) in enumerate(A_GROUPS):
        for r in range(dil):
            blocks.append((g, dil, r, r, q_blk))
            for n in range(1, seq // dil // q_blk):
                start = r + dil * q_blk * (n - 1)
                blocks.append((g, dil, start + dil * q_blk, start, 2 * q_blk))
    pending = []
    for g, dil, q_start, k_start, n_keys in blocks:
        s, v = logits(g, dil, q_start, k_start, n_keys)
        pending.append((g, dil, q_start, s, v))
        if len(pending) > LOGITS_AHEAD:
            finish(*pending.pop(0))
    for item in pending:
        finish(*item)

    chunk = 256

    def merge(c, carry):
        rows = pl.ds(pl.multiple_of(c * chunk, chunk), chunk)
        lses = [lse_scr[g, rows, :] for g in range(N_GROUPS)]
        top = functools.reduce(jnp.maximum, lses)
        ws = [jnp.exp(lse - top) for lse in lses]
        num = functools.reduce(jnp.add, [w * o_scr[g, rows, :] for g, w in enumerate(ws)])
        y_ref[rows, :] = (num / functools.reduce(jnp.add, ws)).astype(y_ref.dtype)
        return carry

    lax.fori_loop(0, seq // chunk, merge, 0)


def attention_prompt(q, kv, band, batch, seq):
    e = A_HEAD_DIM
    assert all(seq % (dil * A_BLOCK) == 0 for _, dil in A_GROUPS)
    in_specs, args = [], []
    for g in range(N_GROUPS):
        in_specs.append(pl.BlockSpec((None, seq, e), lambda b, h, g=g: (g * A_HEADS + h, b, 0)))
        in_specs.append(pl.BlockSpec((None, seq, e), lambda b, h: (h, b, 0)))
        in_specs.append(pl.BlockSpec((None, seq, e), lambda b, h: (A_HEADS + h, b, 0)))
        args += [q, kv[g], kv[g]]
    in_specs.append(pl.BlockSpec((None, N_GROUPS, A_BLOCK, 2 * A_BLOCK), lambda b, h: (h, 0, 0, 0)))
    return pl.pallas_call(
        functools.partial(_attn_prompt_kernel, seq=seq),
        grid=(batch, A_HEADS),
        in_specs=in_specs,
        out_specs=pl.BlockSpec((None, seq, e), lambda b, h: (h, b, 0)),
        out_shape=jax.ShapeDtypeStruct((A_HEADS, batch * seq, e), BF16),
        scratch_shapes=[pltpu.VMEM((N_GROUPS, seq, e), F32),
                        pltpu.VMEM((N_GROUPS, seq, LANES), F32)],
        compiler_params=_cparams(("parallel", "parallel")),
        name="attention_prompt",
    )(*args, band)


def _attn_sample_kernel(*refs):
    q_ref = refs[0]
    kv_new = refs[1:1 + N_GROUPS]
    caches = refs[1 + N_GROUPS:1 + 3 * N_GROUPS]
    tables = refs[1 + 3 * N_GROUPS:1 + 5 * N_GROUPS]
    y_ref = refs[1 + 5 * N_GROUPS]
    scale = A_HEAD_DIM ** -0.5
    e = A_HEAD_DIM

    def head_rows(ref, h):
        rows = math.prod(ref.shape[:-2])
        return ref.reshape(rows * A_HEADS, e)[pl.ds(h, rows, stride=A_HEADS), :]

    def logits(h, g):
        q = q_ref[g * A_HEADS + h]
        s_old = _masked_logits(q, head_rows(caches[2 * g], h), tables[2 * g][h], scale)
        s_new = _masked_logits(q, kv_new[g][h], tables[2 * g + 1][h], scale)
        return s_old, s_new

    results = {}

    def finish(h, g, s_old, s_new):
        v_old = head_rows(caches[2 * g + 1], h)
        v_new = kv_new[g][A_HEADS + h]
        m = jnp.maximum(jnp.max(s_old, axis=-1, keepdims=True), jnp.max(s_new, axis=-1, keepdims=True))
        p_old = jnp.exp(s_old - m)
        p_new = jnp.exp(s_new - m)
        l = jnp.sum(p_old, axis=-1, keepdims=True) + jnp.sum(p_new, axis=-1, keepdims=True)
        o = (jnp.dot(p_old.astype(BF16), v_old.astype(BF16), preferred_element_type=F32)
             + jnp.dot(p_new.astype(BF16), v_new.astype(BF16), preferred_element_type=F32))
        results[h, g] = (o / l, m + jnp.log(l))
        if g == N_GROUPS - 1:
            outs, lses = zip(*[results.pop((h, gg)) for gg in range(N_GROUPS)])
            top = functools.reduce(jnp.maximum, lses)
            ws = [jnp.exp(lse - top) for lse in lses]
            num = functools.reduce(jnp.add, [w * o for w, o in zip(ws, outs)])
            y_ref[:, h * e:(h + 1) * e] = (num / functools.reduce(jnp.add, ws)).astype(y_ref.dtype)

    pending = []
    for h in range(A_HEADS):
        for g in range(N_GROUPS):
            pending.append((h, g, *logits(h, g)))
            if len(pending) > LOGITS_AHEAD:
                finish(*pending.pop(0))
    for item in pending:
        finish(*item)


def attention_sample(q, kv_new, caches, layer, biases, batch, t_new):
    he = A_HEADS * A_HEAD_DIM
    in_specs = [pl.BlockSpec((N_GROUPS * A_HEADS, t_new, A_HEAD_DIM), lambda b: (0, b, 0))]
    args = [q]
    for g in range(N_GROUPS):
        in_specs.append(pl.BlockSpec((2 * A_HEADS, t_new, A_HEAD_DIM), lambda b: (0, b, 0)))
        args.append(kv_new[g])
    tables = []
    for g, (_, dil) in enumerate(A_GROUPS):
        c = caches[g]
        n_past = c.shape[2]
        bg = biases[g]
        by_dist = jnp.concatenate([bg[..., None], jnp.full(bg.shape + (dil - 1,), NEG, F32)], axis=-1)
        by_dist = by_dist.reshape(bg.shape[0], -1)[:, :A_STEPS * dil + 1]
        tab = _toeplitz(by_dist, 0, t_new, n_past + t_new, n_past)
        tab_old, tab_new = tab[:, :, :n_past], tab[:, :, n_past:]
        if dil % t_new == 0 and n_past % dil == 0:
            c = c.reshape(c.shape[0], batch, n_past // dil, dil, 2, A_HEADS, A_HEAD_DIM)
            for t in range(2):
                in_specs.append(pl.BlockSpec((None, None, n_past // dil, t_new, None, A_HEADS, A_HEAD_DIM),
                                             lambda b, t=t: (layer, b, 0, 0, t, 0, 0)))
                args.append(c)
            tab_old = tab_old.reshape(A_HEADS, t_new, n_past // dil, dil)[..., :t_new]
            tab_old = tab_old.reshape(A_HEADS, t_new, (n_past // dil) * t_new)
        else:
            for t in range(2):
                in_specs.append(pl.BlockSpec((None, None, n_past, None, A_HEADS, A_HEAD_DIM),
                                             lambda b, t=t: (layer, b, 0, t, 0, 0)))
                args.append(c)
        tables += [tab_old, tab_new]
    for tab in tables:
        in_specs.append(pl.BlockSpec(tab.shape, lambda b: (0, 0, 0)))
        args.append(tab)
    return pl.pallas_call(
        _attn_sample_kernel,
        grid=(batch,),
        in_specs=in_specs,
        out_specs=pl.BlockSpec((t_new, he), lambda b: (b, 0)),
        out_shape=jax.ShapeDtypeStruct((batch * t_new, he), F32),
        compiler_params=_cparams(("parallel",)),
        name="attention_sample",
    )(*args)


def _log_sigmoid(x):
    return jnp.minimum(x, 0.0) - jnp.log1p(jnp.exp(-jnp.abs(x)))


def _mlstm_kernel(q_ref, k_ref, v_ref, og_ref, gc_ref, gr_ref, bc_ref, br_ref, gn_ref,
                  c0_ref, n0_ref, m0_ref,
                  h_ref, c_out, n_out, m_out,
                  c_scr, n_scr, m_scr, *, chunk):
    c = pl.program_id(1)
    nh, dk, dv = c_scr.shape

    @pl.when(c == 0)
    def _():
        c_scr[...] = c0_ref[...].astype(F32)
        n_scr[...] = n0_ref[...].astype(F32)
        m_scr[...] = m0_ref[...].astype(F32)

    gc = gc_ref[...] + bc_ref[...]
    gr = gr_ref[...] + br_ref[...]
    logf_c = _log_sigmoid(gc[:, nh:])
    logf_r = _log_sigmoid(gr[nh:, :])
    qi = lax.broadcasted_iota(jnp.int32, (chunk, chunk), 0)
    si = lax.broadcasted_iota(jnp.int32, (chunk, chunk), 1)
    causal = si <= qi
    upper = qi <= si
    m_all, n_all = m_scr[...], n_scr[...]
    m_news, n_news = [], []

    def state_stage(h):
        i_col, i_row = gc[:, h:h + 1], gr[h:h + 1, :]
        logf_col, logf_row = logf_c[:, h:h + 1], logf_r[h:h + 1, :]
        b_col = jnp.sum(jnp.where(causal, logf_row, 0.0), axis=1, keepdims=True)
        b_row = jnp.sum(jnp.where(upper, logf_col, 0.0), axis=0, keepdims=True)
        b_last = jnp.sum(logf_row, axis=1, keepdims=True)

        m_prev = m_all[h:h + 1, :]
        a = b_col + m_prev
        dmat = jnp.where(causal, b_col - b_row + i_row, NEG)
        m_t = jnp.maximum(a, jnp.max(dmat, axis=1, keepdims=True))
        w_inter = jnp.exp(a - m_t)

        qf = q_ref[:, h * dk:(h + 1) * dk].astype(F32)
        q = qf.astype(BF16)
        kf = k_ref[:, h * dk:(h + 1) * dk].astype(F32) * (dk ** -0.5)
        k = kf.astype(BF16)
        v = v_ref[:, h * dv:(h + 1) * dv].astype(BF16)
        c_prev = c_scr[h]
        n_prev = n_all[h:h + 1, :]
        qk = lax.dot_general(q, k, (((1,), (1,)), ((), ())), preferred_element_type=F32)
        qc = jnp.dot(q, c_prev.astype(BF16), preferred_element_type=F32)
        qn = jnp.sum(qf * n_prev, axis=1, keepdims=True)

        m_new = jnp.maximum(b_last + m_prev, jnp.max(b_last - b_row + i_row, axis=1, keepdims=True))
        w_state = jnp.exp(b_last - b_col + i_col - m_new)
        decay = jnp.exp(b_last + m_prev - m_new)
        kw = (w_state * kf).astype(BF16)
        n_news.append(decay * n_prev + jnp.sum(w_state * kf, axis=0, keepdims=True))
        m_news.append(m_new)
        return h, qk, dmat, m_t, w_inter, qc, qn, v, kw, decay * c_prev

    def output_stage(h, qk, dmat, m_t, w_inter, qc, qn, v, kw, c_decayed):
        s = qk * jnp.exp(dmat - m_t)
        num = w_inter * qc + jnp.dot(s.astype(BF16), v, preferred_element_type=F32)
        c_scr[h] = c_decayed + lax.dot_general(kw, v, (((0,), (0,)), ((), ())), preferred_element_type=F32)
        den = w_inter * qn + jnp.sum(s, axis=1, keepdims=True)
        hh = num / jnp.maximum(jnp.abs(den), jnp.exp(-m_t))
        ht = hh * jax.nn.sigmoid(og_ref[:, h * dv:(h + 1) * dv].astype(F32))
        hn = ht * lax.rsqrt(jnp.mean(ht * ht, axis=-1, keepdims=True) + EPS)
        h_ref[:, h * dv:(h + 1) * dv] = (hn * gn_ref[:, h * dv:(h + 1) * dv]).astype(h_ref.dtype)

    pending = []
    for h in range(nh):
        pending.append(state_stage(h))
        if len(pending) > HEADS_AHEAD:
            output_stage(*pending.pop(0))
    for item in pending:
        output_stage(*item)

    n_scr[...] = jnp.concatenate(n_news, axis=0)
    m_scr[...] = jnp.concatenate(m_news, axis=0)

    @pl.when(c == pl.num_programs(1) - 1)
    def _():
        c_out[...] = c_scr[...].astype(c_out.dtype)
        n_out[...] = n_scr[...].astype(n_out.dtype)
        m_out[...] = m_scr[...].astype(m_out.dtype)


def mlstm(z, gates, b_gate, g_norm, state, batch, seq, chunk, out_dtype):
    c0, n0, m0 = state
    nh = B_HEADS
    dk, dv = c0.shape[-2], c0.shape[-1]
    nc = seq // chunk
    qk_w, v_w = nh * dk, nh * dv
    assert v_w % qk_w == 0
    gcol = gates.reshape(batch * nc, chunk, 2 * nh)
    grow = jnp.swapaxes(gcol, 1, 2)
    row = lambda b, c: b * nc + c
    state_specs = [
        pl.BlockSpec((None, nh, dk, dv), lambda b, c: (b, 0, 0, 0)),
        pl.BlockSpec((None, nh, dk), lambda b, c: (b, 0, 0)),
        pl.BlockSpec((None, nh, 1), lambda b, c: (b, 0, 0)),
    ]
    in_specs = [
        pl.BlockSpec((chunk, qk_w), lambda b, c: (row(b, c), 0)),
        pl.BlockSpec((chunk, qk_w), lambda b, c: (row(b, c), 1)),
        pl.BlockSpec((chunk, v_w), lambda b, c: (row(b, c), 2 * qk_w // v_w)),
        pl.BlockSpec((chunk, v_w), lambda b, c: (row(b, c), 2 * qk_w // v_w + 1)),
        pl.BlockSpec((None, chunk, 2 * nh), lambda b, c: (row(b, c), 0, 0)),
        pl.BlockSpec((None, 2 * nh, chunk), lambda b, c: (row(b, c), 0, 0)),
        pl.BlockSpec((1, 2 * nh), lambda b, c: (0, 0)),
        pl.BlockSpec((2 * nh, 1), lambda b, c: (0, 0)),
        pl.BlockSpec((1, v_w), lambda b, c: (0, 0)),
    ] + state_specs
    out_shape = [
        jax.ShapeDtypeStruct((batch * seq, v_w), out_dtype),
        jax.ShapeDtypeStruct((batch, nh, dk, dv), F32),
        jax.ShapeDtypeStruct((batch, nh, dk), F32),
        jax.ShapeDtypeStruct((batch, nh, 1), F32),
    ]
    bg = b_gate.astype(F32)
    hn, c_new, n_new, m_new = pl.pallas_call(
        functools.partial(_mlstm_kernel, chunk=chunk),
        grid=(batch, nc),
        in_specs=in_specs,
        out_specs=[pl.BlockSpec((chunk, v_w), lambda b, c: (row(b, c), 0))] + state_specs,
        out_shape=out_shape,
        scratch_shapes=[pltpu.VMEM((nh, dk, dv), F32), pltpu.VMEM((nh, dk), F32), pltpu.VMEM((nh, 1), F32)],
        compiler_params=_cparams(("parallel", "arbitrary")),
        name="mlstm",
    )(z, z, z, z, gcol, grow, bg.reshape(1, 2 * nh), bg.reshape(2 * nh, 1),
      g_norm.astype(F32).reshape(1, v_w),
      c0.reshape(batch, nh, dk, dv), n0.reshape(batch, nh, dk), m0.reshape(batch, nh, 1))
    return hn, (c_new, n_new, m_new.reshape(batch, nh))


def kernel(x_prompt, x_sample, cache_kv_w128, cache_kv_w512, cache_kv_w2048, state_C, state_n, state_m,
           norm_mix, norm_ffn, norm_final, rel_bias, w_a_in, w_a_out, w_b_in, b_b_gate, b_b_norm, w_b_out,
           w_ffn_gate, w_ffn_up, w_ffn_down):
    bp, seq, d_model = x_prompt.shape
    bs, t_new, _ = x_sample.shape
    depth = norm_mix.shape[0]
    nh, dk, dv = state_C.shape[2], state_C.shape[3], state_C.shape[4]
    main_w = 2 * nh * dk + 2 * nh * dv
    caches = (cache_kv_w128, cache_kv_w512, cache_kv_w2048)
    w_b_in_t = jnp.swapaxes(w_b_in, 1, 2)

    biases =[_group_bias(rel_bias, g) for g in range(N_GROUPS)]
    band = _band_tables(biases)

    xp = x_prompt.reshape(bp * seq, d_model)
    xs = x_sample.reshape(bs * t_new, d_model)
    xbp, xbs = round_bf16(xp, xs)
    kv_p =[[] for _ in range(N_GROUPS)]
    kv_s = [[] for _ in range(N_GROUPS)]
    st_p, st_s = [], []
    for layer in range(depth):
        j = layer // 2
        gain = norm_mix[layer]
        if layer % 2 == 0:
            qp, qs = attn_q_proj(xbp, xbs, gain, w_a_in, j)
            kvp, kvs = [], []
            for g, (window, _) in enumerate(A_GROUPS):
                kp, ks, fp, fs = attn_kv_proj(xbp, xbs, gain, w_a_in, j, g, bp, seq, min(window, seq))
                kvp.append(kp)
                kvs.append(ks)
                kv_p[g].append(fp)
                kv_s[g].append(fs.reshape(bs, t_new, 2, A_HEADS, A_HEAD_DIM))
            yp = attention_prompt(qp, kvp, band, bp, seq)
            ys = attention_sample(qs, kvs, caches, j, biases, bs, t_new)
            xp, xs, xbp, xbs = dense(yp, ys, w_a_out, j, res=xp, res_s=xs, emit_bf16=True,
                                     name="attn_out_proj")
        else:
            zero = (jnp.zeros((bp, nh, dk, dv), F32), jnp.zeros((bp, nh, dk), F32), jnp.zeros((bp, nh), F32))
            past = (state_C[j], state_n[j], state_m[j])
            zp, zs = dense(xbp, xbs, w_b_in_t, j, n_cols=main_w, w_t=True, norm_gain=gain,
                           name="mlstm_in_proj")
            gp, gs = gate_proj(xbp, xbs, gain, w_b_in_t, j, main_w, 2 * nh)
            hnp, sp = mlstm(zp, gp, b_b_gate[j], b_b_norm[j], zero, bp, seq, min(PROMPT_CHUNK, seq), BF16)
            hns, ss = mlstm(zs, gs, b_b_gate[j], b_b_norm[j], past, bs, t_new, t_new, F32)
            xp, xs, xbp, xbs = dense(hnp, hns, w_b_out, j, res=xp, res_s=xs, emit_bf16=True,
                                     name="mlstm_out_proj")
            st_p.append(sp)
            st_s.append(ss)
        ap, as_ = ffn_up(xbp, xbs, norm_ffn[layer], w_ffn_gate, w_ffn_up, layer)
        if layer + 1 < depth:
            xp, xs, xbp, xbs = dense(ap, as_, w_ffn_down, layer, res=xp, res_s=xs, emit_bf16=True,
                                     name="ffn_down")
        else:
            xp, xs = dense(ap, as_, w_ffn_down, layer, res=xp, res_s=xs, name="ffn_down")

    y_prompt, y_sample = rmsnorm(xp, xs, norm_final, F32)
    y_prompt = y_prompt.reshape(bp, seq, d_model)
    y_sample = y_sample.reshape(bs, t_new, d_model)
    kvd = cache_kv_w128.dtype
    sd = state_C.dtype
    kv_out_p = [jnp.stack(kv_p[g]).astype(kvd) for g in range(N_GROUPS)]
    kv_out_s = [jnp.stack(kv_s[g]).astype(kvd) for g in range(N_GROUPS)]
    states_p = [jnp.stack([s[i] for s in st_p]).astype(sd) for i in range(3)]
    states_s = [jnp.stack([s[i] for s in st_s]).astype(sd) for i in range(3)]
    return (y_prompt, y_sample, *kv_out_p, *kv_out_s, *states_p, *states_s)
```

```python
import functools
import math

import jax
import jax.numpy as jnp
from jax import lax
from jax.experimental import pallas as pl
from jax.experimental.pallas import tpu as pltpu

F32 = jnp.float32
BF16 = jnp.bfloat16

EPS = 1e-6
NEG = -1e30

A_GROUPS = ((128, 1), (512, 4), (2048, 16))
N_GROUPS = len(A_GROUPS)
A_HEADS = 8
A_HEAD_DIM = 128
A_STEPS = 128
A_BLOCK = 128
NUM_BUCKETS = 32
MAX_DISTANCE = 2048
B_HEADS = 8
LANES = 128
VMEM_LIMIT = 56 * 1024 * 1024
PROMPT_CHUNK = 256
HEADS_AHEAD = 1
LOGITS_AHEAD = 3


def _cparams(sem):
    return pltpu.CompilerParams(dimension_semantics=sem, vmem_limit_bytes=VMEM_LIMIT)


def _pick(n, cands):
    for c in cands:
        if n % c == 0:
            return c
    return n


def _row_tile(m, fixed_bytes, block_bytes_per_row, temp_bytes_per_row):
    for tm in (1024, 512, 256, 128, 64):
        if m % tm == 0 and fixed_bytes + tm * (2 * block_bytes_per_row + temp_bytes_per_row) <= VMEM_LIMIT:
            return tm
    return m


def _rmsnorm_kernel(x_ref, xs_ref, g_ref, o_ref, os_ref):
    def norm(x):
        return x * lax.rsqrt(jnp.mean(x * x, axis=-1, keepdims=True) + EPS) * g_ref[...]

    @pl.when(pl.program_id(0) == 0)
    def _():
        os_ref[...] = norm(xs_ref[...]).astype(os_ref.dtype)

    o_ref[...] = norm(x_ref[...]).astype(o_ref.dtype)


def _round_kernel(x_ref, xs_ref, o_ref, os_ref):
    @pl.when(pl.program_id(0) == 0)
    def _():
        os_ref[...] = xs_ref[...].astype(os_ref.dtype)

    o_ref[...] = x_ref[...].astype(o_ref.dtype)


def round_bf16(x, xs):
    m, d = x.shape
    ms = xs.shape[0]
    tm = _pick(m, (512, 256, 128, 64))
    return pl.pallas_call(
        _round_kernel,
        grid=(m // tm,),
        in_specs=[pl.BlockSpec((tm, d), lambda i: (i, 0)), pl.BlockSpec((ms, d), lambda i: (0, 0))],
        out_specs=[pl.BlockSpec((tm, d), lambda i: (i, 0)), pl.BlockSpec((ms, d), lambda i: (0, 0))],
        out_shape=[jax.ShapeDtypeStruct((m, d), BF16), jax.ShapeDtypeStruct((ms, d), BF16)],
        compiler_params=_cparams(("arbitrary",)),
        name="round_bf16",
    )(x, xs)


def rmsnorm(x, xs, g, out_dtype):
    m, d = x.shape
    ms = xs.shape[0]
    tm = _pick(m, (512, 256, 128, 64))
    return pl.pallas_call(
        _rmsnorm_kernel,
        grid=(m // tm,),
        in_specs=[pl.BlockSpec((tm, d), lambda i: (i, 0)),
                  pl.BlockSpec((ms, d), lambda i: (0, 0)),
                  pl.BlockSpec((1, d), lambda i: (0, 0))],
        out_specs=[pl.BlockSpec((tm, d), lambda i: (i, 0)),
                   pl.BlockSpec((ms, d), lambda i: (0, 0))],
        out_shape=[jax.ShapeDtypeStruct((m, d), out_dtype), jax.ShapeDtypeStruct((ms, d), out_dtype)],
        compiler_params=_cparams(("arbitrary",)),
        name="rmsnorm",
    )(x, xs, g.reshape(1, d))


def _store_cols(o_ref, val):
    if len(o_ref.shape) == 2:
        o_ref[...] = val
    else:
        e = o_ref.shape[-1]
        for c in range(o_ref.shape[0]):
            o_ref[c] = val[:, c * e:(c + 1) * e]


def _row_rstd(x):
    xf = x.astype(F32)
    return lax.rsqrt(jnp.mean(xf * xf, axis=-1, keepdims=True) + EPS)


def _linear_kernel(*refs, n_w, has_res, swiglu, m_axis, w_t, norm, emit_bf16):
    x_ref, xs_ref = refs[:2]
    p = 2
    g_ref = refs[p] if norm else None
    p += norm
    w_refs = refs[p:p + n_w]
    p += n_w
    r_ref, rs_ref = (refs[p], refs[p + 1]) if has_res else (None, None)
    p += 2 * has_res
    o_ref, os_ref = refs[p], refs[p + 1]
    p += 2
    ob_ref, obs_ref = (refs[p], refs[p + 1]) if emit_bf16 else (None, None)
    p += 2 * emit_bf16
    wb_refs = refs[p:]
    contract = (((1,), (1 if w_t else 0,)), ((), ()))

    def apply(x_ref, r_ref, o_ref, ob_ref):
        if len(x_ref.shape) == 3:
            x = jnp.concatenate([x_ref[c] for c in range(x_ref.shape[0])], axis=1).astype(BF16)
        else:
            x = x_ref[...].astype(BF16)
        accs = [lax.dot_general(x, wb_ref[...], contract, preferred_element_type=F32) for wb_ref in wb_refs]
        if norm:
            rstd = _row_rstd(x)
            accs = [acc * rstd for acc in accs]
        out = accs[0] * jax.nn.sigmoid(accs[0]) * accs[1] if swiglu else accs[0]
        if has_res:
            out = r_ref[...] + out
        _store_cols(o_ref, out.astype(o_ref.dtype))
        if emit_bf16:
            ob_ref[...] = out.astype(BF16)

    @pl.when(pl.program_id(m_axis) == 0)
    def _():
        for w_ref, wb_ref in zip(w_refs, wb_refs):
            w = w_ref[...] * g_ref[...] if norm else w_ref[...]
            wb_ref[...] = w.astype(BF16)
        apply(xs_ref, rs_ref, os_ref, obs_ref)

    apply(x_ref, r_ref, o_ref, ob_ref)


def linear(x, xs, ws, w_index, outer, out_shape, out_block, out_index, tn, *, res=None, res_s=None,
           swiglu=False, out_dtype=F32, w_t=False, norm_gain=None, emit_bf16=False, name="linear"):
    if x.ndim == 3:
        m, k = x.shape[1], x.shape[0] * x.shape[2]
        x_spec = lambda tm: pl.BlockSpec((x.shape[0], tm, x.shape[2]), lambda *g: (0, g[nd], 0))
    else:
        m, k = x.shape
        x_spec = lambda tm: pl.BlockSpec((tm, k), lambda *g: (g[nd], 0))
    ms = xs.shape[0]
    nd = len(outer)
    has_res = res is not None
    norm = norm_gain is not None
    bf16_bytes = jnp.dtype(BF16).itemsize
    x_bytes, o_bytes = x.dtype.itemsize, jnp.dtype(out_dtype).itemsize
    row_bytes = (k * x_bytes + tn * (o_bytes + (res.dtype.itemsize if has_res else 0))
                 + (tn * bf16_bytes if emit_bf16 else 0))
    w_bytes = len(ws) * k * tn * (2 * ws[0].dtype.itemsize + bf16_bytes)
    acc_bytes = len(ws) * tn * jnp.dtype(F32).itemsize
    tm = _row_tile(m, w_bytes + 2 * ms * row_bytes, row_bytes, acc_bytes)
    w_block = (None, tn, k) if w_t else (None, k, tn)
    main_map = lambda *g: out_index(*g)
    sample_map = lambda *g: out_index(*g[:nd], 0)
    in_specs = [x_spec(tm), pl.BlockSpec((ms, k), lambda *g: (0, 0))]
    args = [x, xs]
    if norm:
        gain = norm_gain.astype(F32).reshape((1, k) if w_t else (k, 1))
        in_specs.append(pl.BlockSpec(gain.shape, lambda *g: (0, 0)))
        args.append(gain)
    in_specs += [pl.BlockSpec(w_block, lambda *g: w_index(*g[:nd])) for _ in ws]
    args += ws
    if has_res:
        in_specs += [pl.BlockSpec(out_block(tm), main_map), pl.BlockSpec(out_block(ms), sample_map)]
        args += [res, res_s]
    out_specs = [pl.BlockSpec(out_block(tm), main_map), pl.BlockSpec(out_block(ms), sample_map)]
    out_shapes = [jax.ShapeDtypeStruct(out_shape(m), out_dtype), jax.ShapeDtypeStruct(out_shape(ms), out_dtype)]
    if emit_bf16:
        out_specs += [pl.BlockSpec(out_block(tm), main_map), pl.BlockSpec(out_block(ms), sample_map)]
        out_shapes += [jax.ShapeDtypeStruct(out_shape(m), BF16), jax.ShapeDtypeStruct(out_shape(ms), BF16)]
    return pl.pallas_call(
        functools.partial(_linear_kernel, n_w=len(ws), has_res=has_res, swiglu=swiglu, m_axis=nd, w_t=w_t,
                          norm=norm, emit_bf16=emit_bf16),
        grid=(*outer, m // tm),
        in_specs=in_specs,
        out_specs=out_specs,
        out_shape=out_shapes,
        scratch_shapes=[pltpu.VMEM(w_block[1:], BF16) for _ in ws],
        compiler_params=_cparams(("parallel",) * nd + ("arbitrary",)),
        name=name,
    )(*args)


def dense(x, xs, w, layer, *, res=None, res_s=None, out_dtype=F32, n_cols=None, w_t=False,
          norm_gain=None, emit_bf16=False, name="dense"):
    k = x.shape[1] if x.ndim == 2 else x.shape[0] * x.shape[2]
    n = n_cols or w.shape[1 if w_t else 2]
    tn = _pick(n, (1024, 512, 256, 128)) if k <= 2048 else _pick(n, (512, 256, 128))
    w_index = (lambda j: (layer, j, 0)) if w_t else (lambda j: (layer, 0, j))
    return linear(x, xs, [w], w_index, (n // tn,), lambda m: (m, n),
                  lambda tm: (tm, tn), lambda j, i: (i, j), tn, res=res, res_s=res_s,
                  out_dtype=out_dtype, w_t=w_t, norm_gain=norm_gain, emit_bf16=emit_bf16, name=name)


def ffn_up(x, xs, gain, wg, wu, layer):
    n = wg.shape[2]
    tn = _pick(n, (512, 256, 128))
    return linear(x, xs, [wg, wu], lambda j: (layer, 0, j), (n // tn,), lambda m: (m, n),
                  lambda tm: (tm, tn), lambda j, i: (i, j), tn, swiglu=True, out_dtype=BF16,
                  norm_gain=gain, name="ffn_up")


def attn_q_proj(x, xs, gain, w, layer):
    he = A_HEADS * A_HEAD_DIM
    return linear(x, xs, [w], lambda g: (layer, 0, 3 * g), (N_GROUPS,),
                  lambda m: (N_GROUPS * A_HEADS, m, A_HEAD_DIM),
                  lambda tm: (A_HEADS, tm, A_HEAD_DIM), lambda g, i: (g, i, 0), he,
                  norm_gain=gain, name="attn_q_proj")


def _kv_proj_kernel(*refs, tiles_per_batch, first_kept, keep_blk, n_prev):
    x_ref, xs_ref, g_ref, w_ref = refs[:4]
    prev_ref = refs[4] if n_prev else None
    o_ref, os_ref, f_ref, fs_ref, wb_ref = refs[4 + bool(n_prev):]
    i = pl.program_id(1)
    nh, e = f_ref.shape[-2:]

    def keep_rows(acc):
        if n_prev:
            f_ref[pl.ds(0, n_prev)] = prev_ref[...]
        split_heads(f_ref.at[n_prev], acc[acc.shape[0] - keep_blk:])

    def split_heads(dst_ref, rows):
        flat = dst_ref.reshape(rows.shape[0] * nh, e)
        for h in range(nh):
            flat[pl.ds(h, rows.shape[0], stride=nh), :] = rows[:, h * e:(h + 1) * e]

    def project(x):
        return jnp.dot(x, wb_ref[...], preferred_element_type=F32) * _row_rstd(x)

    @pl.when(i == 0)
    def _():
        wb_ref[...] = (w_ref[...] * g_ref[...]).astype(BF16)
        acc = project(xs_ref[...])
        _store_cols(os_ref, acc)
        split_heads(fs_ref, acc)

    acc = project(x_ref[...])
    _store_cols(o_ref, acc)

    if first_kept == 0:
        keep_rows(acc)
    else:
        @pl.when(i % tiles_per_batch >= first_kept)
        def _():
            keep_rows(acc)


def attn_kv_proj(x, xs, gain, w, layer, group, batch, seq, keep, prev_kept=None):
    m, k = x.shape
    ms = xs.shape[0]
    he = A_HEADS * A_HEAD_DIM
    n_prev = 0 if prev_kept is None else prev_kept.shape[0]
    f32_bytes = jnp.dtype(F32).itemsize
    w_bytes = k * he * (2 * w.dtype.itemsize + jnp.dtype(BF16).itemsize)
    row_bytes = k * x.dtype.itemsize + he * f32_bytes * (2 + 2 * n_prev)
    tm = _row_tile(seq, w_bytes, row_bytes, he * f32_bytes)
    keep_blk = min(keep, tm)
    assert keep % keep_blk == 0 and tm % keep_blk == 0
    tpb = seq // tm
    first_kept = tpb - keep // keep_blk if keep >= tm else tpb - 1

    def kept_map(t, i):
        return (0, i // tpb, jnp.maximum(i % tpb - first_kept, 0), t, 0, 0)

    kept_block = lambda layers: (layers, None, keep_blk, None, A_HEADS, A_HEAD_DIM)
    in_specs = [pl.BlockSpec((tm, k), lambda t, i: (i, 0)),
                pl.BlockSpec((ms, k), lambda t, i: (0, 0)),
                pl.BlockSpec((k, 1), lambda t, i: (0, 0)),
                pl.BlockSpec((None, k, he), lambda t, i: (layer, 0, 3 * group + 1 + t))]
    args = [x, xs, gain.astype(F32).reshape(k, 1), w]
    if n_prev:
        in_specs.append(pl.BlockSpec(kept_block(n_prev), kept_map))
        args.append(prev_kept)
    return pl.pallas_call(
        functools.partial(_kv_proj_kernel, tiles_per_batch=tpb, first_kept=first_kept, keep_blk=keep_blk,
                          n_prev=n_prev),
        grid=(2, m // tm),
        in_specs=in_specs,
        out_specs=[pl.BlockSpec((A_HEADS, tm, A_HEAD_DIM), lambda t, i: (t, i, 0)),
                   pl.BlockSpec((A_HEADS, ms, A_HEAD_DIM), lambda t, i: (t, 0, 0)),
                   pl.BlockSpec(kept_block(n_prev + 1), kept_map),
                   pl.BlockSpec((ms, None, A_HEADS, A_HEAD_DIM), lambda t, i: (0, t, 0, 0))],
        out_shape=[jax.ShapeDtypeStruct((2 * A_HEADS, m, A_HEAD_DIM), F32),
                   jax.ShapeDtypeStruct((2 * A_HEADS, ms, A_HEAD_DIM), F32),
                   jax.ShapeDtypeStruct((n_prev + 1, batch, keep, 2, A_HEADS, A_HEAD_DIM), F32),
                   jax.ShapeDtypeStruct((ms, 2, A_HEADS, A_HEAD_DIM), F32)],
        scratch_shapes=[pltpu.VMEM((k, he), BF16)],
        compiler_params=_cparams(("parallel", "arbitrary")),
        name="attn_kv_proj",
    )(*args)


def _gate_kernel(x_ref, xs_ref, g_ref, w_ref, o_ref, os_ref):
    w = (w_ref[...] * g_ref[...]).astype(BF16)
    contract = (((1,), (1,)), ((), ()))

    def project(x):
        return lax.dot_general(x, w, contract, preferred_element_type=F32) * _row_rstd(x)

    @pl.when(pl.program_id(0) == 0)
    def _():
        os_ref[...] = project(xs_ref[...])

    o_ref[...] = project(x_ref[...])


def gate_proj(x, xs, gain, w_t, layer, row0, n):
    m, k = x.shape
    ms = xs.shape[0]
    assert row0 % n == 0
    tm = _pick(m, (512, 256, 128, 64))
    return pl.pallas_call(
        _gate_kernel,
        grid=(m // tm,),
        in_specs=[pl.BlockSpec((tm, k), lambda i: (i, 0)),
                  pl.BlockSpec((ms, k), lambda i: (0, 0)),
                  pl.BlockSpec((1, k), lambda i: (0, 0)),
                  pl.BlockSpec((None, n, k), lambda i: (layer, row0 // n, 0))],
        out_specs=[pl.BlockSpec((tm, n), lambda i: (i, 0)), pl.BlockSpec((ms, n), lambda i: (0, 0))],
        out_shape=[jax.ShapeDtypeStruct((m, n), F32), jax.ShapeDtypeStruct((ms, n), F32)],
        compiler_params=_cparams(("arbitrary",)),
        name="gate_proj",
    )(x, xs, gain.astype(F32).reshape(1, k), w_t)


def _t5_bucket(dist):
    max_exact = NUM_BUCKETS // 2
    d = jnp.maximum(dist.astype(F32), 1.0)
    large = max_exact + jnp.log(d / max_exact) / math.log(MAX_DISTANCE / max_exact) * (NUM_BUCKETS - max_exact)
    large = jnp.minimum(large.astype(jnp.int32), NUM_BUCKETS - 1)
    return jnp.where(dist < max_exact, dist, large)


def _group_bias(rel_bias, g):
    dil = A_GROUPS[g][1]
    buckets = _t5_bucket(jnp.arange(A_STEPS + 1, dtype=jnp.int32) * dil)
    onehot = (buckets[:, None] == jnp.arange(NUM_BUCKETS)[None, :]).astype(F32)
    cols = rel_bias[:, g * A_HEADS:(g + 1) * A_HEADS].astype(F32)
    return jnp.einsum("kb,bh->hk", onehot, cols, precision=lax.Precision.HIGHEST)


def _toeplitz(f, lo, n_rows, n_cols, offset):
    first, last = offset - n_cols + 1, offset + n_rows - 1
    pad_lo, pad_hi = max(lo - first, 0), max(last - (lo + f.shape[-1] - 1), 0)
    fe = jnp.pad(f, [(0, 0)] * (f.ndim - 1) + [(pad_lo, pad_hi)], constant_values=NEG)
    base = lo - pad_lo
    ext = lambda a, b: fe[..., a - base:b - base]
    period = n_rows + n_cols - 1
    w = jnp.concatenate([ext(first, offset + 1)[..., ::-1], ext(offset + 1, last + 1)[..., ::-1]], axis=-1)
    flat = jnp.tile(w, n_rows)[..., :n_rows * (period - 1)]
    return flat.reshape(*f.shape[:-1], n_rows, period - 1)[..., :n_cols]


def _band_tables(biases):
    return jnp.stack([_toeplitz(b, 0, A_BLOCK, 2 * A_BLOCK, A_BLOCK) for b in biases], axis=1)


def _rows(start, size, stride):
    return pl.ds(start, size) if stride == 1 else pl.ds(start, size, stride=stride)


def _masked_logits(q, k, tab, scale):
    s = lax.dot_general(q.astype(BF16), k.astype(BF16), (((1,), (1,)), ((), ())),
                        preferred_element_type=F32) * scale
    return jnp.where(tab > 0.5 * NEG, s + tab, NEG)


def _attn_prompt_kernel(*refs, seq):
    qkv = refs[:3 * N_GROUPS]
    band_ref, y_ref, o_scr, lse_scr = refs[3 * N_GROUPS:]
    scale = A_HEAD_DIM ** -0.5
    q_blk = A_BLOCK

    def logits(g, dil, q_start, k_start, n_keys):
        q_ref, k_ref, v_ref = qkv[3 * g:3 * g + 3]
        q = q_ref[_rows(q_start, q_blk, dil), :]
        k = k_ref[_rows(k_start, n_keys, dil), :]
        v = v_ref[_rows(k_start, n_keys, dil), :]
        tab = band_ref[g] if n_keys == 2 * q_blk else band_ref[g, :, q_blk:]
        return _masked_logits(q, k, tab, scale), v.astype(BF16)

    def finish(g, dil, q_start, s, v):
        m = jnp.max(s, axis=-1, keepdims=True)
        p = jnp.exp(s - m)
        l = jnp.sum(p, axis=-1, keepdims=True)
        o = jnp.dot(p.astype(BF16), v, preferred_element_type=F32)
        rows = _rows(q_start, q_blk, dil)
        o_scr[g, rows, :] = o / l
        lse_scr[g, rows, :] = jnp.broadcast_to(m + jnp.log(l), (q_blk, LANES))

    blocks = []
    for g, (_, dil) in enumerate(A_GROUPS):
        for r in range(dil):
            blocks.append((g, dil, r, r, q_blk))
            for n in range(1, seq // dil // q_blk):
                start = r + dil * q_blk * (n - 1)
                blocks.append((g, dil, start + dil * q_blk, start, 2 * q_blk))
    pending = []
    for g, dil, q_start, k_start, n_keys in blocks:
        s, v = logits(g, dil, q_start, k_start, n_keys)
        pending.append((g, dil, q_start, s, v))
        if len(pending) > LOGITS_AHEAD:
            finish(*pending.pop(0))
    for item in pending:
        finish(*item)

    chunk = 256

    def merge(c, carry):
        rows = pl.ds(pl.multiple_of(c * chunk, chunk), chunk)
        lses = [lse_scr[g, rows, :] for g in range(N_GROUPS)]
        top = functools.reduce(jnp.maximum, lses)
        ws = [jnp.exp(lse - top) for lse in lses]
        num = functools.reduce(jnp.add, [w * o_scr[g, rows, :] for g, w in enumerate(ws)])
        y_ref[rows, :] = (num / functools.reduce(jnp.add, ws)).astype(y_ref.dtype)
        return carry

    lax.fori_loop(0, seq // chunk, merge, 0)


def attention_prompt(q, kv, band, batch, seq):
    e = A_HEAD_DIM
    assert all(seq % (dil * A_BLOCK) == 0 for _, dil in A_GROUPS)
    in_specs, args = [], []
    for g in range(N_GROUPS):
        in_specs.append(pl.BlockSpec((None, seq, e), lambda b, h, g=g: (g * A_HEADS + h, b, 0)))
        in_specs.append(pl.BlockSpec((None, seq, e), lambda b, h: (h, b, 0)))
        in_specs.append(pl.BlockSpec((None, seq, e), lambda b, h: (A_HEADS + h, b, 0)))
        args += [q, kv[g], kv[g]]
    in_specs.append(pl.BlockSpec((None, N_GROUPS, A_BLOCK, 2 * A_BLOCK), lambda b, h: (h, 0, 0, 0)))
    return pl.pallas_call(
        functools.partial(_attn_prompt_kernel, seq=seq),
        grid=(batch, A_HEADS),
        in_specs=in_specs,
        out_specs=pl.BlockSpec((None, seq, e), lambda b, h: (h, b, 0)),
        out_shape=jax.ShapeDtypeStruct((A_HEADS, batch * seq, e), BF16),
        scratch_shapes=[pltpu.VMEM((N_GROUPS, seq, e), F32),
                        pltpu.VMEM((N_GROUPS, seq, LANES), F32)],
        compiler_params=_cparams(("parallel", "parallel")),
        name="attention_prompt",
    )(*args, band)


def _attn_sample_kernel(*refs):
    q_ref = refs[0]
    kv_new = refs[1:1 + N_GROUPS]
    caches = refs[1 + N_GROUPS:1 + 3 * N_GROUPS]
    tables = refs[1 + 3 * N_GROUPS:1 + 5 * N_GROUPS]
    y_ref = refs[1 + 5 * N_GROUPS]
    scale = A_HEAD_DIM ** -0.5
    e = A_HEAD_DIM

    def head_rows(ref, h):
        rows = math.prod(ref.shape[:-2])
        return ref.reshape(rows * A_HEADS, e)[pl.ds(h, rows, stride=A_HEADS), :]

    def logits(h, g):
        q = q_ref[g * A_HEADS + h]
        s_old = _masked_logits(q, head_rows(caches[2 * g], h), tables[2 * g][h], scale)
        s_new = _masked_logits(q, kv_new[g][h], tables[2 * g + 1][h], scale)
        return s_old, s_new

    results = {}

    def finish(h, g, s_old, s_new):
        v_old = head_rows(caches[2 * g + 1], h)
        v_new = kv_new[g][A_HEADS + h]
        m = jnp.maximum(jnp.max(s_old, axis=-1, keepdims=True), jnp.max(s_new, axis=-1, keepdims=True))
        p_old = jnp.exp(s_old - m)
        p_new = jnp.exp(s_new - m)
        l = jnp.sum(p_old, axis=-1, keepdims=True) + jnp.sum(p_new, axis=-1, keepdims=True)
        o = (jnp.dot(p_old.astype(BF16), v_old.astype(BF16), preferred_element_type=F32)
             + jnp.dot(p_new.astype(BF16), v_new.astype(BF16), preferred_element_type=F32))
        results[h, g] = (o / l, m + jnp.log(l))
        if g == N_GROUPS - 1:
            outs, lses = zip(*[results.pop((h, gg)) for gg in range(N_GROUPS)])
            top = functools.reduce(jnp.maximum, lses)
            ws = [jnp.exp(lse - top) for lse in lses]
            num = functools.reduce(jnp.add, [w * o for w, o in zip(ws, outs)])
            y_ref[:, h * e:(h + 1) * e] = (num / functools.reduce(jnp.add, ws)).astype(y_ref.dtype)

    pending = []
    for h in range(A_HEADS):
        for g in range(N_GROUPS):
            pending.append((h, g, *logits(h, g)))
            if len(pending) > LOGITS_AHEAD:
                finish(*pending.pop(0))
    for item in pending:
        finish(*item)


def attention_sample(q, kv_new, caches, layer, biases, batch, t_new):
    he = A_HEADS * A_HEAD_DIM
    in_specs = [pl.BlockSpec((N_GROUPS * A_HEADS, t_new, A_HEAD_DIM), lambda b: (0, b, 0))]
    args = [q]
    for g in range(N_GROUPS):
        in_specs.append(pl.BlockSpec((2 * A_HEADS, t_new, A_HEAD_DIM), lambda b: (0, b, 0)))
        args.append(kv_new[g])
    tables = []
    for g, (_, dil) in enumerate(A_GROUPS):
        c = caches[g]
        n_past = c.shape[2]
        bg = biases[g]
        by_dist = jnp.concatenate([bg[..., None], jnp.full(bg.shape + (dil - 1,), NEG, F32)], axis=-1)
        by_dist = by_dist.reshape(bg.shape[0], -1)[:, :A_STEPS * dil + 1]
        tab = _toeplitz(by_dist, 0, t_new, n_past + t_new, n_past)
        tab_old, tab_new = tab[:, :, :n_past], tab[:, :, n_past:]
        if dil % t_new == 0 and n_past % dil == 0:
            c = c.reshape(c.shape[0], batch, n_past // dil, dil, 2, A_HEADS, A_HEAD_DIM)
            for t in range(2):
                in_specs.append(pl.BlockSpec((None, None, n_past // dil, t_new, None, A_HEADS, A_HEAD_DIM),
                                             lambda b, t=t: (layer, b, 0, 0, t, 0, 0)))
                args.append(c)
            tab_old = tab_old.reshape(A_HEADS, t_new, n_past // dil, dil)[..., :t_new]
            tab_old = tab_old.reshape(A_HEADS, t_new, (n_past // dil) * t_new)
        else:
            for t in range(2):
                in_specs.append(pl.BlockSpec((None, None, n_past, None, A_HEADS, A_HEAD_DIM),
                                             lambda b, t=t: (layer, b, 0, t, 0, 0)))
                args.append(c)
        tables += [tab_old, tab_new]
    for tab in tables:
        in_specs.append(pl.BlockSpec(tab.shape, lambda b: (0, 0, 0)))
        args.append(tab)
    return pl.pallas_call(
        _attn_sample_kernel,
        grid=(batch,),
        in_specs=in_specs,
        out_specs=pl.BlockSpec((t_new, he), lambda b: (b, 0)),
        out_shape=jax.ShapeDtypeStruct((batch * t_new, he), F32),
        compiler_params=_cparams(("parallel",)),
        name="attention_sample",
    )(*args)


def _log_sigmoid(x):
    return jnp.minimum(x, 0.0) - jnp.log1p(jnp.exp(-jnp.abs(x)))


def _mlstm_kernel(q_ref, k_ref, v_ref, og_ref, gc_ref, gr_ref, bc_ref, br_ref, gn_ref,
                  c0_ref, n0_ref, m0_ref,
                  h_ref, c_out, n_out, m_out,
                  c_scr, n_scr, m_scr, *, chunk):
    c = pl.program_id(1)
    nh, dk, dv = c_scr.shape

    @pl.when(c == 0)
    def _():
        c_scr[...] = c0_ref[...].astype(F32)
        n_scr[...] = n0_ref[...].astype(F32)
        m_scr[...] = m0_ref[...].astype(F32)

    gc = gc_ref[...] + bc_ref[...]
    gr = gr_ref[...] + br_ref[...]
    logf_c = _log_sigmoid(gc[:, nh:])
    logf_r = _log_sigmoid(gr[nh:, :])
    qi = lax.broadcasted_iota(jnp.int32, (chunk, chunk), 0)
    si = lax.broadcasted_iota(jnp.int32, (chunk, chunk), 1)
    causal = si <= qi
    upper = qi <= si
    m_all, n_all = m_scr[...], n_scr[...]
    m_news, n_news = [], []

    def state_stage(h):
        i_col, i_row = gc[:, h:h + 1], gr[h:h + 1, :]
        logf_col, logf_row = logf_c[:, h:h + 1], logf_r[h:h + 1, :]
        b_col = jnp.sum(jnp.where(causal, logf_row, 0.0), axis=1, keepdims=True)
        b_row = jnp.sum(jnp.where(upper, logf_col, 0.0), axis=0, keepdims=True)
        b_last = jnp.sum(logf_row, axis=1, keepdims=True)

        m_prev = m_all[h:h + 1, :]
        a = b_col + m_prev
        dmat = jnp.where(causal, b_col - b_row + i_row, NEG)
        m_t = jnp.maximum(a, jnp.max(dmat, axis=1, keepdims=True))
        w_inter = jnp.exp(a - m_t)

        qf = q_ref[:, h * dk:(h + 1) * dk].astype(F32)
        q = qf.astype(BF16)
        kf = k_ref[:, h * dk:(h + 1) * dk].astype(F32) * (dk ** -0.5)
        k = kf.astype(BF16)
        v = v_ref[:, h * dv:(h + 1) * dv].astype(BF16)
        c_prev = c_scr[h]
        n_prev = n_all[h:h + 1, :]
        qk = lax.dot_general(q, k, (((1,), (1,)), ((), ())), preferred_element_type=F32)
        qc = jnp.dot(q, c_prev.astype(BF16), preferred_element_type=F32)
        qn = jnp.sum(qf * n_prev, axis=1, keepdims=True)

        m_new = jnp.maximum(b_last + m_prev, jnp.max(b_last - b_row + i_row, axis=1, keepdims=True))
        w_state = jnp.exp(b_last - b_col + i_col - m_new)
        decay = jnp.exp(b_last + m_prev - m_new)
        kw = (w_state * kf).astype(BF16)
        n_news.append(decay * n_prev + jnp.sum(w_state * kf, axis=0, keepdims=True))
        m_news.append(m_new)
        return h, qk, dmat, m_t, w_inter, qc, qn, v, kw, decay * c_prev

    def output_stage(h, qk, dmat, m_t, w_inter, qc, qn, v, kw, c_decayed):
        s = qk * jnp.exp(dmat - m_t)
        num = w_inter * qc + jnp.dot(s.astype(BF16), v, preferred_element_type=F32)
        c_scr[h] = c_decayed + lax.dot_general(kw, v, (((0,), (0,)), ((), ())), preferred_element_type=F32)
        den = w_inter * qn + jnp.sum(s, axis=1, keepdims=True)
        hh = num / jnp.maximum(jnp.abs(den), jnp.exp(-m_t))
        ht = hh * jax.nn.sigmoid(og_ref[:, h * dv:(h + 1) * dv].astype(F32))
        hn = ht * lax.rsqrt(jnp.mean(ht * ht, axis=-1, keepdims=True) + EPS)
        h_ref[:, h * dv:(h + 1) * dv] = (hn * gn_ref[:, h * dv:(h + 1) * dv]).astype(h_ref.dtype)

    pending = []
    for h in range(nh):
        pending.append(state_stage(h))
        if len(pending) > HEADS_AHEAD:
            output_stage(*pending.pop(0))
    for item in pending:
        output_stage(*item)

    n_scr[...] = jnp.concatenate(n_news, axis=0)
    m_scr[...] = jnp.concatenate(m_news, axis=0)

    @pl.when(c == pl.num_programs(1) - 1)
    def _():
        c_out[...] = c_scr[...].astype(c_out.dtype)
        n_out[...] = n_scr[...].astype(n_out.dtype)
        m_out[...] = m_scr[...].astype(m_out.dtype)


def mlstm(z, gates, b_gate, g_norm, state, batch, seq, chunk, out_dtype):
    c0, n0, m0 = state
    nh = B_HEADS
    dk, dv = c0.shape[-2], c0.shape[-1]
    nc = seq // chunk
    qk_w, v_w = nh * dk, nh * dv
    assert v_w % qk_w == 0
    gcol = gates.reshape(batch * nc, chunk, 2 * nh)
    grow = jnp.swapaxes(gcol, 1, 2)
    row = lambda b, c: b * nc + c
    state_specs = [
        pl.BlockSpec((None, nh, dk, dv), lambda b, c: (b, 0, 0, 0)),
        pl.BlockSpec((None, nh, dk), lambda b, c: (b, 0, 0)),
        pl.BlockSpec((None, nh, 1), lambda b, c: (b, 0, 0)),
    ]
    in_specs = [
        pl.BlockSpec((chunk, qk_w), lambda b, c: (row(b, c), 0)),
        pl.BlockSpec((chunk, qk_w), lambda b, c: (row(b, c), 1)),
        pl.BlockSpec((chunk, v_w), lambda b, c: (row(b, c), 2 * qk_w // v_w)),
        pl.BlockSpec((chunk, v_w), lambda b, c: (row(b, c), 2 * qk_w // v_w + 1)),
        pl.BlockSpec((None, chunk, 2 * nh), lambda b, c: (row(b, c), 0, 0)),
        pl.BlockSpec((None, 2 * nh, chunk), lambda b, c: (row(b, c), 0, 0)),
        pl.BlockSpec((1, 2 * nh), lambda b, c: (0, 0)),
        pl.BlockSpec((2 * nh, 1), lambda b, c: (0, 0)),
        pl.BlockSpec((1, v_w), lambda b, c: (0, 0)),
    ] + state_specs
    out_shape = [
        jax.ShapeDtypeStruct((batch * seq, v_w), out_dtype),
        jax.ShapeDtypeStruct((batch, nh, dk, dv), F32),
        jax.ShapeDtypeStruct((batch, nh, dk), F32),
        jax.ShapeDtypeStruct((batch, nh, 1), F32),
    ]
    bg = b_gate.astype(F32)
    hn, c_new, n_new, m_new = pl.pallas_call(
        functools.partial(_mlstm_kernel, chunk=chunk),
        grid=(batch, nc),
        in_specs=in_specs,
        out_specs=[pl.BlockSpec((chunk, v_w), lambda b, c: (row(b, c), 0))] + state_specs,
        out_shape=out_shape,
        scratch_shapes=[pltpu.VMEM((nh, dk, dv), F32), pltpu.VMEM((nh, dk), F32), pltpu.VMEM((nh, 1), F32)],
        compiler_params=_cparams(("parallel", "arbitrary")),
        name="mlstm",
    )(z, z, z, z, gcol, grow, bg.reshape(1, 2 * nh), bg.reshape(2 * nh, 1),
      g_norm.astype(F32).reshape(1, v_w),
      c0.reshape(batch, nh, dk, dv), n0.reshape(batch, nh, dk), m0.reshape(batch, nh, 1))
    return hn, (c_new, n_new, m_new.reshape(batch, nh))


def kernel(x_prompt, x_sample, cache_kv_w128, cache_kv_w512, cache_kv_w2048, state_C, state_n, state_m,
           norm_mix, norm_ffn, norm_final, rel_bias, w_a_in, w_a_out, w_b_in, b_b_gate, b_b_norm, w_b_out,
           w_ffn_gate, w_ffn_up, w_ffn_down):
    bp, seq, d_model = x_prompt.shape
    bs, t_new, _ = x_sample.shape
    depth = norm_mix.shape[0]
    nh, dk, dv = state_C.shape[2], state_C.shape[3], state_C.shape[4]
    main_w = 2 * nh * dk + 2 * nh * dv
    caches = (cache_kv_w128, cache_kv_w512, cache_kv_w2048)
    w_b_in_t = jnp.swapaxes(w_b_in, 1, 2)

    biases =[_group_bias(rel_bias, g) for g in range(N_GROUPS)]
    band = _band_tables(biases)

    xp = x_prompt.reshape(bp * seq, d_model)
    xs = x_sample.reshape(bs * t_new, d_model)
    xbp, xbs = round_bf16(xp, xs)
    kv_p = [None] * N_GROUPS
    kv_s = [[] for _ in range(N_GROUPS)]
    st_p, st_s = [], []
    for layer in range(depth):
        j = layer // 2
        gain = norm_mix[layer]
        if layer % 2 == 0:
            qp, qs = attn_q_proj(xbp, xbs, gain, w_a_in, j)
            kvp, kvs = [], []
            for g, (window, _) in enumerate(A_GROUPS):
                kp, ks, kv_p[g], fs = attn_kv_proj(xbp, xbs, gain, w_a_in, j, g, bp, seq, min(window, seq),
                                                   prev_kept=kv_p[g])
                kvp.append(kp)
                kvs.append(ks)
                kv_s[g].append(fs.reshape(bs, t_new, 2, A_HEADS, A_HEAD_DIM))
            yp = attention_prompt(qp, kvp, band, bp, seq)
            ys = attention_sample(qs, kvs, caches, j, biases, bs, t_new)
            xp, xs, xbp, xbs = dense(yp, ys, w_a_out, j, res=xp, res_s=xs, emit_bf16=True,
                                     name="attn_out_proj")
        else:
            zero = (jnp.zeros((bp, nh, dk, dv), F32), jnp.zeros((bp, nh, dk), F32), jnp.zeros((bp, nh), F32))
            past = (state_C[j], state_n[j], state_m[j])
            zp, zs = dense(xbp, xbs, w_b_in_t, j, n_cols=main_w, w_t=True, norm_gain=gain,
                           name="mlstm_in_proj")
            gp, gs = gate_proj(xbp, xbs, gain, w_b_in_t, j, main_w, 2 * nh)
            hnp, sp = mlstm(zp, gp, b_b_gate[j], b_b_norm[j], zero, bp, seq, min(PROMPT_CHUNK, seq), BF16)
            hns, ss = mlstm(zs, gs, b_b_gate[j], b_b_norm[j], past, bs, t_new, t_new, F32)
            xp, xs, xbp, xbs = dense(hnp, hns, w_b_out, j, res=xp, res_s=xs, emit_bf16=True,
                                     name="mlstm_out_proj")
            st_p.append(sp)
            st_s.append(ss)
        ap, as_ = ffn_up(xbp, xbs, norm_ffn[layer], w_ffn_gate, w_ffn_up, layer)
        if layer + 1 < depth:
            xp, xs, xbp, xbs = dense(ap, as_, w_ffn_down, layer, res=xp, res_s=xs, emit_bf16=True,
                                     name="ffn_down")
        else:
            xp, xs = dense(ap, as_, w_ffn_down, layer, res=xp, res_s=xs, name="ffn_down")

    y_prompt, y_sample = rmsnorm(xp, xs, norm_final, F32)
    y_prompt = y_prompt.reshape(bp, seq, d_model)
    y_sample = y_sample.reshape(bs, t_new, d_model)
    kvd = cache_kv_w128.dtype
    sd = state_C.dtype
    kv_out_p = [kv_p[g].astype(kvd) for g in range(N_GROUPS)]
    kv_out_s = [jnp.stack(kv_s[g]).astype(kvd) for g in range(N_GROUPS)]
    states_p = [jnp.stack([s[i] for s in st_p]).astype(sd) for i in range(3)]
    states_s = [jnp.stack([s[i] for s in st_s]).astype(sd) for i in range(3)]
    return (y_prompt, y_sample, *kv_out_p, *kv_out_s, *states_p, *states_s)
```

```python
import functools
import math

import jax
import jax.numpy as jnp
from jax import lax
from jax.experimental import pallas as pl
from jax.experimental.pallas import tpu as pltpu

F32 = jnp.float32
BF16 = jnp.bfloat16

EPS = 1e-6
NEG = -1e30

A_GROUPS = ((128, 1), (512, 4), (2048, 16))
N_GROUPS = len(A_GROUPS)
A_HEADS = 8
A_HEAD_DIM = 128
A_STEPS = 128
A_BLOCK = 128
NUM_BUCKETS = 32
MAX_DISTANCE = 2048
B_HEADS = 8
LANES = 128
VMEM_LIMIT = 56 * 1024 * 1024
PROMPT_CHUNK = 256
HEADS_AHEAD = 1
LOGITS_AHEAD = 3


def _cparams(sem):
    return pltpu.CompilerParams(dimension_semantics=sem, vmem_limit_bytes=VMEM_LIMIT)


def _pick(n, cands):
    for c in cands:
        if n % c == 0:
            return c
    return n


def _row_tile(m, fixed_bytes, block_bytes_per_row, temp_bytes_per_row):
    for tm in (1024, 512, 256, 128, 64):
        if m % tm == 0 and fixed_bytes + tm * (2 * block_bytes_per_row + temp_bytes_per_row) <= VMEM_LIMIT:
            return tm
    return m


def _rmsnorm_kernel(x_ref, xs_ref, g_ref, o_ref, os_ref):
    def norm(x):
        return x * lax.rsqrt(jnp.mean(x * x, axis=-1, keepdims=True) + EPS) * g_ref[...]

    @pl.when(pl.program_id(0) == 0)
    def _():
        os_ref[...] = norm(xs_ref[...]).astype(os_ref.dtype)

    o_ref[...] = norm(x_ref[...]).astype(o_ref.dtype)


def _round_kernel(x_ref, xs_ref, o_ref, os_ref):
    @pl.when(pl.program_id(0) == 0)
    def _():
        os_ref[...] = xs_ref[...].astype(os_ref.dtype)

    o_ref[...] = x_ref[...].astype(o_ref.dtype)


def round_bf16(x, xs):
    m, d = x.shape
    ms = xs.shape[0]
    tm = _pick(m, (512, 256, 128, 64))
    return pl.pallas_call(
        _round_kernel,
        grid=(m // tm,),
        in_specs=[pl.BlockSpec((tm, d), lambda i: (i, 0)), pl.BlockSpec((ms, d), lambda i: (0, 0))],
        out_specs=[pl.BlockSpec((tm, d), lambda i: (i, 0)), pl.BlockSpec((ms, d), lambda i: (0, 0))],
        out_shape=[jax.ShapeDtypeStruct((m, d), BF16), jax.ShapeDtypeStruct((ms, d), BF16)],
        compiler_params=_cparams(("arbitrary",)),
        name="round_bf16",
    )(x, xs)


def rmsnorm(x, xs, g, out_dtype):
    m, d = x.shape
    ms = xs.shape[0]
    tm = _pick(m, (512, 256, 128, 64))
    return pl.pallas_call(
        _rmsnorm_kernel,
        grid=(m // tm,),
        in_specs=[pl.BlockSpec((tm, d), lambda i: (i, 0)),
                  pl.BlockSpec((ms, d), lambda i: (0, 0)),
                  pl.BlockSpec((1, d), lambda i: (0, 0))],
        out_specs=[pl.BlockSpec((tm, d), lambda i: (i, 0)),
                   pl.BlockSpec((ms, d), lambda i: (0, 0))],
        out_shape=[jax.ShapeDtypeStruct((m, d), out_dtype), jax.ShapeDtypeStruct((ms, d), out_dtype)],
        compiler_params=_cparams(("arbitrary",)),
        name="rmsnorm",
    )(x, xs, g.reshape(1, d))


def _store_cols(o_ref, val):
    if len(o_ref.shape) == 2:
        o_ref[...] = val
    else:
        e = o_ref.shape[-1]
        for c in range(o_ref.shape[0]):
            o_ref[c] = val[:, c * e:(c + 1) * e]


def _row_rstd(x):
    xf = x.astype(F32)
    return lax.rsqrt(jnp.mean(xf * xf, axis=-1, keepdims=True) + EPS)


def _linear_kernel(*refs, n_w, has_res, swiglu, m_axis, w_t, norm, emit_bf16):
    x_ref, xs_ref = refs[:2]
    p = 2
    g_ref = refs[p] if norm else None
    p += norm
    w_refs = refs[p:p + n_w]
    p += n_w
    r_ref, rs_ref = (refs[p], refs[p + 1]) if has_res else (None, None)
    p += 2 * has_res
    o_ref, os_ref = refs[p], refs[p + 1]
    p += 2
    ob_ref, obs_ref = (refs[p], refs[p + 1]) if emit_bf16 else (None, None)
    p += 2 * emit_bf16
    wb_refs = refs[p:]
    contract = (((1,), (1 if w_t else 0,)), ((), ()))

    def apply(x_ref, r_ref, o_ref, ob_ref):
        if len(x_ref.shape) == 3:
            x = jnp.concatenate([x_ref[c] for c in range(x_ref.shape[0])], axis=1).astype(BF16)
        else:
            x = x_ref[...].astype(BF16)
        accs = [lax.dot_general(x, wb_ref[...], contract, preferred_element_type=F32) for wb_ref in wb_refs]
        if norm:
            rstd = _row_rstd(x)
            accs = [acc * rstd for acc in accs]
        out = accs[0] * jax.nn.sigmoid(accs[0]) * accs[1] if swiglu else accs[0]
        if has_res:
            out = r_ref[...] + out
        _store_cols(o_ref, out.astype(o_ref.dtype))
        if emit_bf16:
            ob_ref[...] = out.astype(BF16)

    @pl.when(pl.program_id(m_axis) == 0)
    def _():
        for w_ref, wb_ref in zip(w_refs, wb_refs):
            w = w_ref[...] * g_ref[...] if norm else w_ref[...]
            wb_ref[...] = w.astype(BF16)
        apply(xs_ref, rs_ref, os_ref, obs_ref)

    apply(x_ref, r_ref, o_ref, ob_ref)


def linear(x, xs, ws, w_index, outer, out_shape, out_block, out_index, tn, *, res=None, res_s=None,
           swiglu=False, out_dtype=F32, w_t=False, norm_gain=None, emit_bf16=False, name="linear"):
    if x.ndim == 3:
        m, k = x.shape[1], x.shape[0] * x.shape[2]
        x_spec = lambda tm: pl.BlockSpec((x.shape[0], tm, x.shape[2]), lambda *g: (0, g[nd], 0))
    else:
        m, k = x.shape
        x_spec = lambda tm: pl.BlockSpec((tm, k), lambda *g: (g[nd], 0))
    ms = xs.shape[0]
    nd = len(outer)
    has_res = res is not None
    norm = norm_gain is not None
    bf16_bytes = jnp.dtype(BF16).itemsize
    x_bytes, o_bytes = x.dtype.itemsize, jnp.dtype(out_dtype).itemsize
    row_bytes = (k * x_bytes + tn * (o_bytes + (res.dtype.itemsize if has_res else 0))
                 + (tn * bf16_bytes if emit_bf16 else 0))
    w_bytes = len(ws) * k * tn * (2 * ws[0].dtype.itemsize + bf16_bytes)
    acc_bytes = len(ws) * tn * jnp.dtype(F32).itemsize
    tm = _row_tile(m, w_bytes + 2 * ms * row_bytes, row_bytes, acc_bytes)
    w_block = (None, tn, k) if w_t else (None, k, tn)
    main_map = lambda *g: out_index(*g)
    sample_map = lambda *g: out_index(*g[:nd], 0)
    in_specs = [x_spec(tm), pl.BlockSpec((ms, k), lambda *g: (0, 0))]
    args = [x, xs]
    if norm:
        gain = norm_gain.astype(F32).reshape((1, k) if w_t else (k, 1))
        in_specs.append(pl.BlockSpec(gain.shape, lambda *g: (0, 0)))
        args.append(gain)
    in_specs += [pl.BlockSpec(w_block, lambda *g: w_index(*g[:nd])) for _ in ws]
    args += ws
    if has_res:
        in_specs += [pl.BlockSpec(out_block(tm), main_map), pl.BlockSpec(out_block(ms), sample_map)]
        args += [res, res_s]
    out_specs = [pl.BlockSpec(out_block(tm), main_map), pl.BlockSpec(out_block(ms), sample_map)]
    out_shapes = [jax.ShapeDtypeStruct(out_shape(m), out_dtype), jax.ShapeDtypeStruct(out_shape(ms), out_dtype)]
    if emit_bf16:
        out_specs += [pl.BlockSpec(out_block(tm), main_map), pl.BlockSpec(out_block(ms), sample_map)]
        out_shapes += [jax.ShapeDtypeStruct(out_shape(m), BF16), jax.ShapeDtypeStruct(out_shape(ms), BF16)]
    return pl.pallas_call(
        functools.partial(_linear_kernel, n_w=len(ws), has_res=has_res, swiglu=swiglu, m_axis=nd, w_t=w_t,
                          norm=norm, emit_bf16=emit_bf16),
        grid=(*outer, m // tm),
        in_specs=in_specs,
        out_specs=out_specs,
        out_shape=out_shapes,
        scratch_shapes=[pltpu.VMEM(w_block[1:], BF16) for _ in ws],
        compiler_params=_cparams(("parallel",) * nd + ("arbitrary",)),
        name=name,
    )(*args)


def dense(x, xs, w, layer, *, res=None, res_s=None, out_dtype=F32, n_cols=None, w_t=False,
          norm_gain=None, emit_bf16=False, name="dense"):
    k = x.shape[1] if x.ndim == 2 else x.shape[0] * x.shape[2]
    n = n_cols or w.shape[1 if w_t else 2]
    tn = _pick(n, (1024, 512, 256, 128)) if k <= 2048 else _pick(n, (512, 256, 128))
    w_index = (lambda j: (layer, j, 0)) if w_t else (lambda j: (layer, 0, j))
    return linear(x, xs, [w], w_index, (n // tn,), lambda m: (m, n),
                  lambda tm: (tm, tn), lambda j, i: (i, j), tn, res=res, res_s=res_s,
                  out_dtype=out_dtype, w_t=w_t, norm_gain=norm_gain, emit_bf16=emit_bf16, name=name)


def ffn_up(x, xs, gain, wg, wu, layer):
    n = wg.shape[2]
    tn = _pick(n, (512, 256, 128))
    return linear(x, xs, [wg, wu], lambda j: (layer, 0, j), (n // tn,), lambda m: (m, n),
                  lambda tm: (tm, tn), lambda j, i: (i, j), tn, swiglu=True, out_dtype=BF16,
                  norm_gain=gain, name="ffn_up")


def attn_q_proj(x, xs, gain, w, layer):
    he = A_HEADS * A_HEAD_DIM
    return linear(x, xs, [w], lambda g: (layer, 0, 3 * g), (N_GROUPS,),
                  lambda m: (N_GROUPS * A_HEADS, m, A_HEAD_DIM),
                  lambda tm: (A_HEADS, tm, A_HEAD_DIM), lambda g, i: (g, i, 0), he,
                  norm_gain=gain, name="attn_q_proj")


def _kv_proj_kernel(*refs, tiles_per_batch, first_kept, keep_blk, n_prev):
    x_ref, xs_ref, g_ref, w_ref = refs[:4]
    prev_ref = refs[4] if n_prev else None
    o_ref, os_ref, f_ref, fs_ref, wb_ref = refs[4 + bool(n_prev):]
    i = pl.program_id(1)
    nh, e = f_ref.shape[-2:]

    def keep_rows(acc):
        if n_prev:
            f_ref[pl.ds(0, n_prev)] = prev_ref[...]
        split_heads(f_ref.at[n_prev], acc[acc.shape[0] - keep_blk:])

    def split_heads(dst_ref, rows):
        flat = dst_ref.reshape(rows.shape[0] * nh, e)
        for h in range(nh):
            flat[pl.ds(h, rows.shape[0], stride=nh), :] = rows[:, h * e:(h + 1) * e]

    def project(x):
        return jnp.dot(x, wb_ref[...], preferred_element_type=F32) * _row_rstd(x)

    @pl.when(i == 0)
    def _():
        wb_ref[...] = (w_ref[...] * g_ref[...]).astype(BF16)
        acc = project(xs_ref[...])
        _store_cols(os_ref, acc)
        split_heads(fs_ref, acc)

    acc = project(x_ref[...])
    _store_cols(o_ref, acc)

    if first_kept == 0:
        keep_rows(acc)
    else:
        @pl.when(i % tiles_per_batch >= first_kept)
        def _():
            keep_rows(acc)


def attn_kv_proj(x, xs, gain, w, layer, group, batch, seq, keep, prev_kept=None):
    m, k = x.shape
    ms = xs.shape[0]
    he = A_HEADS * A_HEAD_DIM
    n_prev = 0 if prev_kept is None else prev_kept.shape[0]
    f32_bytes = jnp.dtype(F32).itemsize
    w_bytes = k * he * (2 * w.dtype.itemsize + jnp.dtype(BF16).itemsize)
    kept_bytes = lambda rows: 2 * min(keep, rows) * he * f32_bytes * (1 + 2 * n_prev)
    row_bytes = k * x.dtype.itemsize + he * f32_bytes
    small_bytes = 2 * k * LANES * f32_bytes + 2 * ms * (row_bytes + he * f32_bytes)
    tm = next(t for t in (1024, 512, 256, 128)
              if _row_tile(t, w_bytes + small_bytes + kept_bytes(t), row_bytes, 2 * he * f32_bytes) == t
              and seq % t == 0)
    keep_blk = min(keep, tm)
    assert keep % keep_blk == 0 and tm % keep_blk == 0
    tpb = seq // tm
    first_kept = tpb - keep // keep_blk if keep >= tm else tpb - 1

    def kept_map(t, i):
        return (0, i // tpb, jnp.maximum(i % tpb - first_kept, 0), t, 0, 0)

    kept_block = lambda layers: (layers, None, keep_blk, None, A_HEADS, A_HEAD_DIM)
    in_specs = [pl.BlockSpec((tm, k), lambda t, i: (i, 0)),
                pl.BlockSpec((ms, k), lambda t, i: (0, 0)),
                pl.BlockSpec((k, 1), lambda t, i: (0, 0)),
                pl.BlockSpec((None, k, he), lambda t, i: (layer, 0, 3 * group + 1 + t))]
    args = [x, xs, gain.astype(F32).reshape(k, 1), w]
    if n_prev:
        in_specs.append(pl.BlockSpec(kept_block(n_prev), kept_map))
        args.append(prev_kept)
    return pl.pallas_call(
        functools.partial(_kv_proj_kernel, tiles_per_batch=tpb, first_kept=first_kept, keep_blk=keep_blk,
                          n_prev=n_prev),
        grid=(2, m // tm),
        in_specs=in_specs,
        out_specs=[pl.BlockSpec((A_HEADS, tm, A_HEAD_DIM), lambda t, i: (t, i, 0)),
                   pl.BlockSpec((A_HEADS, ms, A_HEAD_DIM), lambda t, i: (t, 0, 0)),
                   pl.BlockSpec(kept_block(n_prev + 1), kept_map),
                   pl.BlockSpec((ms, None, A_HEADS, A_HEAD_DIM), lambda t, i: (0, t, 0, 0))],
        out_shape=[jax.ShapeDtypeStruct((2 * A_HEADS, m, A_HEAD_DIM), F32),
                   jax.ShapeDtypeStruct((2 * A_HEADS, ms, A_HEAD_DIM), F32),
                   jax.ShapeDtypeStruct((n_prev + 1, batch, keep, 2, A_HEADS, A_HEAD_DIM), F32),
                   jax.ShapeDtypeStruct((ms, 2, A_HEADS, A_HEAD_DIM), F32)],
        scratch_shapes=[pltpu.VMEM((k, he), BF16)],
        compiler_params=_cparams(("parallel", "arbitrary")),
        name="attn_kv_proj",
    )(*args)


def _gate_kernel(x_ref, xs_ref, g_ref, w_ref, o_ref, os_ref):
    w = (w_ref[...] * g_ref[...]).astype(BF16)
    contract = (((1,), (1,)), ((), ()))

    def project(x):
        return lax.dot_general(x, w, contract, preferred_element_type=F32) * _row_rstd(x)

    @pl.when(pl.program_id(0) == 0)
    def _():
        os_ref[...] = project(xs_ref[...])

    o_ref[...] = project(x_ref[...])


def gate_proj(x, xs, gain, w_t, layer, row0, n):
    m, k = x.shape
    ms = xs.shape[0]
    assert row0 % n == 0
    tm = _pick(m, (512, 256, 128, 64))
    return pl.pallas_call(
        _gate_kernel,
        grid=(m // tm,),
        in_specs=[pl.BlockSpec((tm, k), lambda i: (i, 0)),
                  pl.BlockSpec((ms, k), lambda i: (0, 0)),
                  pl.BlockSpec((1, k), lambda i: (0, 0)),
                  pl.BlockSpec((None, n, k), lambda i: (layer, row0 // n, 0))],
        out_specs=[pl.BlockSpec((tm, n), lambda i: (i, 0)), pl.BlockSpec((ms, n), lambda i: (0, 0))],
        out_shape=[jax.ShapeDtypeStruct((m, n), F32), jax.ShapeDtypeStruct((ms, n), F32)],
        compiler_params=_cparams(("arbitrary",)),
        name="gate_proj",
    )(x, xs, gain.astype(F32).reshape(1, k), w_t)


def _t5_bucket(dist):
    max_exact = NUM_BUCKETS // 2
    d = jnp.maximum(dist.astype(F32), 1.0)
    large = max_exact + jnp.log(d / max_exact) / math.log(MAX_DISTANCE / max_exact) * (NUM_BUCKETS - max_exact)
    large = jnp.minimum(large.astype(jnp.int32), NUM_BUCKETS - 1)
    return jnp.where(dist < max_exact, dist, large)


def _group_bias(rel_bias, g):
    dil = A_GROUPS[g][1]
    buckets = _t5_bucket(jnp.arange(A_STEPS + 1, dtype=jnp.int32) * dil)
    onehot = (buckets[:, None] == jnp.arange(NUM_BUCKETS)[None, :]).astype(F32)
    cols = rel_bias[:, g * A_HEADS:(g + 1) * A_HEADS].astype(F32)
    return jnp.einsum("kb,bh->hk", onehot, cols, precision=lax.Precision.HIGHEST)


def _toeplitz(f, lo, n_rows, n_cols, offset):
    first, last = offset - n_cols + 1, offset + n_rows - 1
    pad_lo, pad_hi = max(lo - first, 0), max(last - (lo + f.shape[-1] - 1), 0)
    fe = jnp.pad(f, [(0, 0)] * (f.ndim - 1) + [(pad_lo, pad_hi)], constant_values=NEG)
    base = lo - pad_lo
    ext = lambda a, b: fe[..., a - base:b - base]
    period = n_rows + n_cols - 1
    w = jnp.concatenate([ext(first, offset + 1)[..., ::-1], ext(offset + 1, last + 1)[..., ::-1]], axis=-1)
    flat = jnp.tile(w, n_rows)[..., :n_rows * (period - 1)]
    return flat.reshape(*f.shape[:-1], n_rows, period - 1)[..., :n_cols]


def _band_tables(biases):
    return jnp.stack([_toeplitz(b, 0, A_BLOCK, 2 * A_BLOCK, A_BLOCK) for b in biases], axis=1)


def _rows(start, size, stride):
    return pl.ds(start, size) if stride == 1 else pl.ds(start, size, stride=stride)


def _masked_logits(q, k, tab, scale):
    s = lax.dot_general(q.astype(BF16), k.astype(BF16), (((1,), (1,)), ((), ())),
                        preferred_element_type=F32) * scale
    return jnp.where(tab > 0.5 * NEG, s + tab, NEG)


def _attn_prompt_kernel(*refs, seq):
    qkv = refs[:3 * N_GROUPS]
    band_ref, y_ref, o_scr, lse_scr = refs[3 * N_GROUPS:]
    scale = A_HEAD_DIM ** -0.5
    q_blk = A_BLOCK

    def logits(g, dil, q_start, k_start, n_keys):
        q_ref, k_ref, v_ref = qkv[3 * g:3 * g + 3]
        q = q_ref[_rows(q_start, q_blk, dil), :]
        k = k_ref[_rows(k_start, n_keys, dil), :]
        v = v_ref[_rows(k_start, n_keys, dil), :]
        tab = band_ref[g] if n_keys == 2 * q_blk else band_ref[g, :, q_blk:]
        return _masked_logits(q, k, tab, scale), v.astype(BF16)

    def finish(g, dil, q_start, s, v):
        m = jnp.max(s, axis=-1, keepdims=True)
        p = jnp.exp(s - m)
        l = jnp.sum(p, axis=-1, keepdims=True)
        o = jnp.dot(p.astype(BF16), v, preferred_element_type=F32)
        rows = _rows(q_start, q_blk, dil)
        o_scr[g, rows, :] = o / l
        lse_scr[g, rows, :] = jnp.broadcast_to(m + jnp.log(l), (q_blk, LANES))

    blocks = []
    for g, (_, dil) in enumerate(A_GROUPS):
        for r in range(dil):
            blocks.append((g, dil, r, r, q_blk))
            for n in range(1, seq // dil // q_blk):
                start = r + dil * q_blk * (n - 1)
                blocks.append((g, dil, start + dil * q_blk, start, 2 * q_blk))
    pending = []
    for g, dil, q_start, k_start, n_keys in blocks:
        s, v = logits(g, dil, q_start, k_start, n_keys)
        pending.append((g, dil, q_start, s, v))
        if len(pending) > LOGITS_AHEAD:
            finish(*pending.pop(0))
    for item in pending:
        finish(*item)

    chunk = 256

    def merge(c, carry):
        rows = pl.ds(pl.multiple_of(c * chunk, chunk), chunk)
        lses = [lse_scr[g, rows, :] for g in range(N_GROUPS)]
        top = functools.reduce(jnp.maximum, lses)
        ws = [jnp.exp(lse - top) for lse in lses]
        num = functools.reduce(jnp.add, [w * o_scr[g, rows, :] for g, w in enumerate(ws)])
        y_ref[rows, :] = (num / functools.reduce(jnp.add, ws)).astype(y_ref.dtype)
        return carry

    lax.fori_loop(0, seq // chunk, merge, 0)


def attention_prompt(q, kv, band, batch, seq):
    e = A_HEAD_DIM
    assert all(seq % (dil * A_BLOCK) == 0 for _, dil in A_GROUPS)
    in_specs, args = [], []
    for g in range(N_GROUPS):
        in_specs.append(pl.BlockSpec((None, seq, e), lambda b, h, g=g: (g * A_HEADS + h, b, 0)))
        in_specs.append(pl.BlockSpec((None, seq, e), lambda b, h: (h, b, 0)))
        in_specs.append(pl.BlockSpec((None, seq, e), lambda b, h: (A_HEADS + h, b, 0)))
        args += [q, kv[g], kv[g]]
    in_specs.append(pl.BlockSpec((None, N_GROUPS, A_BLOCK, 2 * A_BLOCK), lambda b, h: (h, 0, 0, 0)))
    return pl.pallas_call(
        functools.partial(_attn_prompt_kernel, seq=seq),
        grid=(batch, A_HEADS),
        in_specs=in_specs,
        out_specs=pl.BlockSpec((None, seq, e), lambda b, h: (h, b, 0)),
        out_shape=jax.ShapeDtypeStruct((A_HEADS, batch * seq, e), BF16),
        scratch_shapes=[pltpu.VMEM((N_GROUPS, seq, e), F32),
                        pltpu.VMEM((N_GROUPS, seq, LANES), F32)],
        compiler_params=_cparams(("parallel", "parallel")),
        name="attention_prompt",
    )(*args, band)


def _attn_sample_kernel(*refs):
    q_ref = refs[0]
    kv_new = refs[1:1 + N_GROUPS]
    caches = refs[1 + N_GROUPS:1 + 3 * N_GROUPS]
    tables = refs[1 + 3 * N_GROUPS:1 + 5 * N_GROUPS]
    y_ref = refs[1 + 5 * N_GROUPS]
    scale = A_HEAD_DIM ** -0.5
    e = A_HEAD_DIM

    def head_rows(ref, h):
        rows = math.prod(ref.shape[:-2])
        return ref.reshape(rows * A_HEADS, e)[pl.ds(h, rows, stride=A_HEADS), :]

    def logits(h, g):
        q = q_ref[g * A_HEADS + h]
        s_old = _masked_logits(q, head_rows(caches[2 * g], h), tables[2 * g][h], scale)
        s_new = _masked_logits(q, kv_new[g][h], tables[2 * g + 1][h], scale)
        return s_old, s_new

    results = {}

    def finish(h, g, s_old, s_new):
        v_old = head_rows(caches[2 * g + 1], h)
        v_new = kv_new[g][A_HEADS + h]
        m = jnp.maximum(jnp.max(s_old, axis=-1, keepdims=True), jnp.max(s_new, axis=-1, keepdims=True))
        p_old = jnp.exp(s_old - m)
        p_new = jnp.exp(s_new - m)
        l = jnp.sum(p_old, axis=-1, keepdims=True) + jnp.sum(p_new, axis=-1, keepdims=True)
        o = (jnp.dot(p_old.astype(BF16), v_old.astype(BF16), preferred_element_type=F32)
             + jnp.dot(p_new.astype(BF16), v_new.astype(BF16), preferred_element_type=F32))
        results[h, g] = (o / l, m + jnp.log(l))
        if g == N_GROUPS - 1:
            outs, lses = zip(*[results.pop((h, gg)) for gg in range(N_GROUPS)])
            top = functools.reduce(jnp.maximum, lses)
            ws = [jnp.exp(lse - top) for lse in lses]
            num = functools.reduce(jnp.add, [w * o for w, o in zip(ws, outs)])
            y_ref[:, h * e:(h + 1) * e] = (num / functools.reduce(jnp.add, ws)).astype(y_ref.dtype)

    pending = []
    for h in range(A_HEADS):
        for g in range(N_GROUPS):
            pending.append((h, g, *logits(h, g)))
            if len(pending) > LOGITS_AHEAD:
                finish(*pending.pop(0))
    for item in pending:
        finish(*item)


def attention_sample(q, kv_new, caches, layer, biases, batch, t_new):
    he = A_HEADS * A_HEAD_DIM
    in_specs = [pl.BlockSpec((N_GROUPS * A_HEADS, t_new, A_HEAD_DIM), lambda b: (0, b, 0))]
    args = [q]
    for g in range(N_GROUPS):
        in_specs.append(pl.BlockSpec((2 * A_HEADS, t_new, A_HEAD_DIM), lambda b: (0, b, 0)))
        args.append(kv_new[g])
    tables = []
    for g, (_, dil) in enumerate(A_GROUPS):
        c = caches[g]
        n_past = c.shape[2]
        bg = biases[g]
        by_dist = jnp.concatenate([bg[..., None], jnp.full(bg.shape + (dil - 1,), NEG, F32)], axis=-1)
        by_dist = by_dist.reshape(bg.shape[0], -1)[:, :A_STEPS * dil + 1]
        tab = _toeplitz(by_dist, 0, t_new, n_past + t_new, n_past)
        tab_old, tab_new = tab[:, :, :n_past], tab[:, :, n_past:]
        if dil % t_new == 0 and n_past % dil == 0:
            c = c.reshape(c.shape[0], batch, n_past // dil, dil, 2, A_HEADS, A_HEAD_DIM)
            for t in range(2):
                in_specs.append(pl.BlockSpec((None, None, n_past // dil, t_new, None, A_HEADS, A_HEAD_DIM),
                                             lambda b, t=t: (layer, b, 0, 0, t, 0, 0)))
                args.append(c)
            tab_old = tab_old.reshape(A_HEADS, t_new, n_past // dil, dil)[..., :t_new]
            tab_old = tab_old.reshape(A_HEADS, t_new, (n_past // dil) * t_new)
        else:
            for t in range(2):
                in_specs.append(pl.BlockSpec((None, None, n_past, None, A_HEADS, A_HEAD_DIM),
                                             lambda b, t=t: (layer, b, 0, t, 0, 0)))
                args.append(c)
        tables += [tab_old, tab_new]
    for tab in tables:
        in_specs.append(pl.BlockSpec(tab.shape, lambda b: (0, 0, 0)))
        args.append(tab)
    return pl.pallas_call(
        _attn_sample_kernel,
        grid=(batch,),
        in_specs=in_specs,
        out_specs=pl.BlockSpec((t_new, he), lambda b: (b, 0)),
        out_shape=jax.ShapeDtypeStruct((batch * t_new, he), F32),
        compiler_params=_cparams(("parallel",)),
        name="attention_sample",
    )(*args)


def _log_sigmoid(x):
    return jnp.minimum(x, 0.0) - jnp.log1p(jnp.exp(-jnp.abs(x)))


def _mlstm_kernel(q_ref, k_ref, v_ref, og_ref, gc_ref, gr_ref, bc_ref, br_ref, gn_ref,
                  c0_ref, n0_ref, m0_ref,
                  h_ref, c_out, n_out, m_out,
                  c_scr, n_scr, m_scr, *, chunk):
    c = pl.program_id(1)
    nh, dk, dv = c_scr.shape

    @pl.when(c == 0)
    def _():
        c_scr[...] = c0_ref[...].astype(F32)
        n_scr[...] = n0_ref[...].astype(F32)
        m_scr[...] = m0_ref[...].astype(F32)

    gc = gc_ref[...] + bc_ref[...]
    gr = gr_ref[...] + br_ref[...]
    logf_c = _log_sigmoid(gc[:, nh:])
    logf_r = _log_sigmoid(gr[nh:, :])
    qi = lax.broadcasted_iota(jnp.int32, (chunk, chunk), 0)
    si = lax.broadcasted_iota(jnp.int32, (chunk, chunk), 1)
    causal = si <= qi
    upper = qi <= si
    m_all, n_all = m_scr[...], n_scr[...]
    m_news, n_news = [], []

    def state_stage(h):
        i_col, i_row = gc[:, h:h + 1], gr[h:h + 1, :]
        logf_col, logf_row = logf_c[:, h:h + 1], logf_r[h:h + 1, :]
        b_col = jnp.sum(jnp.where(causal, logf_row, 0.0), axis=1, keepdims=True)
        b_row = jnp.sum(jnp.where(upper, logf_col, 0.0), axis=0, keepdims=True)
        b_last = jnp.sum(logf_row, axis=1, keepdims=True)

        m_prev = m_all[h:h + 1, :]
        a = b_col + m_prev
        dmat = jnp.where(causal, b_col - b_row + i_row, NEG)
        m_t = jnp.maximum(a, jnp.max(dmat, axis=1, keepdims=True))
        w_inter = jnp.exp(a - m_t)

        qf = q_ref[:, h * dk:(h + 1) * dk].astype(F32)
        q = qf.astype(BF16)
        kf = k_ref[:, h * dk:(h + 1) * dk].astype(F32) * (dk ** -0.5)
        k = kf.astype(BF16)
        v = v_ref[:, h * dv:(h + 1) * dv].astype(BF16)
        c_prev = c_scr[h]
        n_prev = n_all[h:h + 1, :]
        qk = lax.dot_general(q, k, (((1,), (1,)), ((), ())), preferred_element_type=F32)
        qc = jnp.dot(q, c_prev.astype(BF16), preferred_element_type=F32)
        qn = jnp.sum(qf * n_prev, axis=1, keepdims=True)

        m_new = jnp.maximum(b_last + m_prev, jnp.max(b_last - b_row + i_row, axis=1, keepdims=True))
        w_state = jnp.exp(b_last - b_col + i_col - m_new)
        decay = jnp.exp(b_last + m_prev - m_new)
        kw = (w_state * kf).astype(BF16)
        n_news.append(decay * n_prev + jnp.sum(w_state * kf, axis=0, keepdims=True))
        m_news.append(m_new)
        return h, qk, dmat, m_t, w_inter, qc, qn, v, kw, decay * c_prev

    def output_stage(h, qk, dmat, m_t, w_inter, qc, qn, v, kw, c_decayed):
        s = qk * jnp.exp(dmat - m_t)
        num = w_inter * qc + jnp.dot(s.astype(BF16), v, preferred_element_type=F32)
        c_scr[h] = c_decayed + lax.dot_general(kw, v, (((0,), (0,)), ((), ())), preferred_element_type=F32)
        den = w_inter * qn + jnp.sum(s, axis=1, keepdims=True)
        hh = num / jnp.maximum(jnp.abs(den), jnp.exp(-m_t))
        ht = hh * jax.nn.sigmoid(og_ref[:, h * dv:(h + 1) * dv].astype(F32))
        hn = ht * lax.rsqrt(jnp.mean(ht * ht, axis=-1, keepdims=True) + EPS)
        h_ref[:, h * dv:(h + 1) * dv] = (hn * gn_ref[:, h * dv:(h + 1) * dv]).astype(h_ref.dtype)

    pending = []
    for h in range(nh):
        pending.append(state_stage(h))
        if len(pending) > HEADS_AHEAD:
            output_stage(*pending.pop(0))
    for item in pending:
        output_stage(*item)

    n_scr[...] = jnp.concatenate(n_news, axis=0)
    m_scr[...] = jnp.concatenate(m_news, axis=0)

    @pl.when(c == pl.num_programs(1) - 1)
    def _():
        c_out[...] = c_scr[...].astype(c_out.dtype)
        n_out[...] = n_scr[...].astype(n_out.dtype)
        m_out[...] = m_scr[...].astype(m_out.dtype)


def mlstm(z, gates, b_gate, g_norm, state, batch, seq, chunk, out_dtype):
    c0, n0, m0 = state
    nh = B_HEADS
    dk, dv = c0.shape[-2], c0.shape[-1]
    nc = seq // chunk
    qk_w, v_w = nh * dk, nh * dv
    assert v_w % qk_w == 0
    gcol = gates.reshape(batch * nc, chunk, 2 * nh)
    grow = jnp.swapaxes(gcol, 1, 2)
    row = lambda b, c: b * nc + c
    state_specs = [
        pl.BlockSpec((None, nh, dk, dv), lambda b, c: (b, 0, 0, 0)),
        pl.BlockSpec((None, nh, dk), lambda b, c: (b, 0, 0)),
        pl.BlockSpec((None, nh, 1), lambda b, c: (b, 0, 0)),
    ]
    in_specs = [
        pl.BlockSpec((chunk, qk_w), lambda b, c: (row(b, c), 0)),
        pl.BlockSpec((chunk, qk_w), lambda b, c: (row(b, c), 1)),
        pl.BlockSpec((chunk, v_w), lambda b, c: (row(b, c), 2 * qk_w // v_w)),
        pl.BlockSpec((chunk, v_w), lambda b, c: (row(b, c), 2 * qk_w // v_w + 1)),
        pl.BlockSpec((None, chunk, 2 * nh), lambda b, c: (row(b, c), 0, 0)),
        pl.BlockSpec((None, 2 * nh, chunk), lambda b, c: (row(b, c), 0, 0)),
        pl.BlockSpec((1, 2 * nh), lambda b, c: (0, 0)),
        pl.BlockSpec((2 * nh, 1), lambda b, c: (0, 0)),
        pl.BlockSpec((1, v_w), lambda b, c: (0, 0)),
    ] + state_specs
    out_shape = [
        jax.ShapeDtypeStruct((batch * seq, v_w), out_dtype),
        jax.ShapeDtypeStruct((batch, nh, dk, dv), F32),
        jax.ShapeDtypeStruct((batch, nh, dk), F32),
        jax.ShapeDtypeStruct((batch, nh, 1), F32),
    ]
    bg = b_gate.astype(F32)
    hn, c_new, n_new, m_new = pl.pallas_call(
        functools.partial(_mlstm_kernel, chunk=chunk),
        grid=(batch, nc),
        in_specs=in_specs,
        out_specs=[pl.BlockSpec((chunk, v_w), lambda b, c: (row(b, c), 0))] + state_specs,
        out_shape=out_shape,
        scratch_shapes=[pltpu.VMEM((nh, dk, dv), F32), pltpu.VMEM((nh, dk), F32), pltpu.VMEM((nh, 1), F32)],
        compiler_params=_cparams(("parallel", "arbitrary")),
        name="mlstm",
    )(z, z, z, z, gcol, grow, bg.reshape(1, 2 * nh), bg.reshape(2 * nh, 1),
      g_norm.astype(F32).reshape(1, v_w),
      c0.reshape(batch, nh, dk, dv), n0.reshape(batch, nh, dk), m0.reshape(batch, nh, 1))
    return hn, (c_new, n_new, m_new.reshape(batch, nh))


def kernel(x_prompt, x_sample, cache_kv_w128, cache_kv_w512, cache_kv_w2048, state_C, state_n, state_m,
           norm_mix, norm_ffn, norm_final, rel_bias, w_a_in, w_a_out, w_b_in, b_b_gate, b_b_norm, w_b_out,
           w_ffn_gate, w_ffn_up, w_ffn_down):
    bp, seq, d_model = x_prompt.shape
    bs, t_new, _ = x_sample.shape
    depth = norm_mix.shape[0]
    nh, dk, dv = state_C.shape[2], state_C.shape[3], state_C.shape[4]
    main_w = 2 * nh * dk + 2 * nh * dv
    caches = (cache_kv_w128, cache_kv_w512, cache_kv_w2048)
    w_b_in_t = jnp.swapaxes(w_b_in, 1, 2)

    biases =[_group_bias(rel_bias, g) for g in range(N_GROUPS)]
    band = _band_tables(biases)

    xp = x_prompt.reshape(bp * seq, d_model)
    xs = x_sample.reshape(bs * t_new, d_model)
    xbp, xbs = round_bf16(xp, xs)
    kv_p = [None] * N_GROUPS
    kv_s = [[] for _ in range(N_GROUPS)]
    st_p, st_s = [], []
    for layer in range(depth):
        j = layer // 2
        gain = norm_mix[layer]
        if layer % 2 == 0:
            qp, qs = attn_q_proj(xbp, xbs, gain, w_a_in, j)
            kvp, kvs = [], []
            for g, (window, _) in enumerate(A_GROUPS):
                kp, ks, kv_p[g], fs = attn_kv_proj(xbp, xbs, gain, w_a_in, j, g, bp, seq, min(window, seq),
                                                   prev_kept=kv_p[g])
                kvp.append(kp)
                kvs.append(ks)
                kv_s[g].append(fs.reshape(bs, t_new, 2, A_HEADS, A_HEAD_DIM))
            yp = attention_prompt(qp, kvp, band, bp, seq)
            ys = attention_sample(qs, kvs, caches, j, biases, bs, t_new)
            xp, xs, xbp, xbs = dense(yp, ys, w_a_out, j, res=xp, res_s=xs, emit_bf16=True,
                                     name="attn_out_proj")
        else:
            zero = (jnp.zeros((bp, nh, dk, dv), F32), jnp.zeros((bp, nh, dk), F32), jnp.zeros((bp, nh), F32))
            past = (state_C[j], state_n[j], state_m[j])
            zp, zs = dense(xbp, xbs, w_b_in_t, j, n_cols=main_w, w_t=True, norm_gain=gain,
                           name="mlstm_in_proj")
            gp, gs = gate_proj(xbp, xbs, gain, w_b_in_t, j, main_w, 2 * nh)
            hnp, sp = mlstm(zp, gp, b_b_gate[j], b_b_norm[j], zero, bp, seq, min(PROMPT_CHUNK, seq), BF16)
            hns, ss = mlstm(zs, gs, b_b_gate[j], b_b_norm[j], past, bs, t_new, t_new, F32)
            xp, xs, xbp, xbs = dense(hnp, hns, w_b_out, j, res=xp, res_s=xs, emit_bf16=True,
                                     name="mlstm_out_proj")
            st_p.append(sp)
            st_s.append(ss)
        ap, as_ = ffn_up(xbp, xbs, norm_ffn[layer], w_ffn_gate, w_ffn_up, layer)
        if layer + 1 < depth:
            xp, xs, xbp, xbs = dense(ap, as_, w_ffn_down, layer, res=xp, res_s=xs, emit_bf16=True,
                                     name="ffn_down")
        else:
            xp, xs = dense(ap, as_, w_ffn_down, layer, res=xp, res_s=xs, name="ffn_down")

    y_prompt, y_sample = rmsnorm(xp, xs, norm_final, F32)
    y_prompt = y_prompt.reshape(bp, seq, d_model)
    y_sample = y_sample.reshape(bs, t_new, d_model)
    kvd = cache_kv_w128.dtype
    sd = state_C.dtype
    kv_out_p = [kv_p[g].astype(kvd) for g in range(N_GROUPS)]
    kv_out_s = [jnp.stack(kv_s[g]).astype(kvd) for g in range(N_GROUPS)]
    states_p = [jnp.stack([s[i] for s in st_p]).astype(sd) for i in range(3)]
    states_s = [jnp.stack([s[i] for s in st_s]).astype(sd) for i in range(3)]
    return (y_prompt, y_sample, *kv_out_p, *kv_out_s, *states_p, *states_s)
```

```python
import functools
import math

import jax
import jax.numpy as jnp
from jax import lax
from jax.experimental import pallas as pl
from jax.experimental.pallas import tpu as pltpu

F32 = jnp.float32
BF16 = jnp.bfloat16

EPS = 1e-6
NEG = -1e30

A_GROUPS = ((128, 1), (512, 4), (2048, 16))
N_GROUPS = len(A_GROUPS)
A_HEADS = 8
A_HEAD_DIM = 128
A_STEPS = 128
A_BLOCK = 128
NUM_BUCKETS = 32
MAX_DISTANCE = 2048
B_HEADS = 8
LANES = 128
VMEM_LIMIT = 56 * 1024 * 1024
PROMPT_CHUNK = 256
HEADS_AHEAD = 1
LOGITS_AHEAD = 2
SAMPLE_LOGITS_AHEAD = 3


def _cparams(sem):
    return pltpu.CompilerParams(dimension_semantics=sem, vmem_limit_bytes=VMEM_LIMIT)


def _pick(n, cands):
    for c in cands:
        if n % c == 0:
            return c
    return n


def _row_tile(m, fixed_bytes, block_bytes_per_row, temp_bytes_per_row):
    for tm in (1024, 512, 256, 128, 64):
        if m % tm == 0 and fixed_bytes + tm * (2 * block_bytes_per_row + temp_bytes_per_row) <= VMEM_LIMIT:
            return tm
    return m


def _rmsnorm_kernel(x_ref, xs_ref, g_ref, o_ref, os_ref):
    def norm(x):
        return x * lax.rsqrt(jnp.mean(x * x, axis=-1, keepdims=True) + EPS) * g_ref[...]

    @pl.when(pl.program_id(0) == 0)
    def _():
        os_ref[...] = norm(xs_ref[...]).astype(os_ref.dtype)

    o_ref[...] = norm(x_ref[...]).astype(o_ref.dtype)


def _round_kernel(x_ref, xs_ref, o_ref, os_ref):
    @pl.when(pl.program_id(0) == 0)
    def _():
        os_ref[...] = xs_ref[...].astype(os_ref.dtype)

    o_ref[...] = x_ref[...].astype(o_ref.dtype)


def round_bf16(x, xs):
    m, d = x.shape
    ms = xs.shape[0]
    tm = _pick(m, (512, 256, 128, 64))
    return pl.pallas_call(
        _round_kernel,
        grid=(m // tm,),
        in_specs=[pl.BlockSpec((tm, d), lambda i: (i, 0)), pl.BlockSpec((ms, d), lambda i: (0, 0))],
        out_specs=[pl.BlockSpec((tm, d), lambda i: (i, 0)), pl.BlockSpec((ms, d), lambda i: (0, 0))],
        out_shape=[jax.ShapeDtypeStruct((m, d), BF16), jax.ShapeDtypeStruct((ms, d), BF16)],
        compiler_params=_cparams(("arbitrary",)),
        name="round_bf16",
    )(x, xs)


def rmsnorm(x, xs, g, out_dtype):
    m, d = x.shape
    ms = xs.shape[0]
    tm = _pick(m, (512, 256, 128, 64))
    return pl.pallas_call(
        _rmsnorm_kernel,
        grid=(m // tm,),
        in_specs=[pl.BlockSpec((tm, d), lambda i: (i, 0)),
                  pl.BlockSpec((ms, d), lambda i: (0, 0)),
                  pl.BlockSpec((1, d), lambda i: (0, 0))],
        out_specs=[pl.BlockSpec((tm, d), lambda i: (i, 0)),
                   pl.BlockSpec((ms, d), lambda i: (0, 0))],
        out_shape=[jax.ShapeDtypeStruct((m, d), out_dtype), jax.ShapeDtypeStruct((ms, d), out_dtype)],
        compiler_params=_cparams(("arbitrary",)),
        name="rmsnorm",
    )(x, xs, g.reshape(1, d))


def _store_cols(o_ref, val):
    if len(o_ref.shape) == 2:
        o_ref[...] = val
    else:
        e = o_ref.shape[-1]
        for c in range(o_ref.shape[0]):
            o_ref[c] = val[:, c * e:(c + 1) * e]


def _row_rstd(x):
    xf = x.astype(F32)
    return lax.rsqrt(jnp.mean(xf * xf, axis=-1, keepdims=True) + EPS)


def _linear_kernel(*refs, n_w, has_res, swiglu, m_axis, w_t, norm, emit_bf16):
    x_ref, xs_ref = refs[:2]
    p = 2
    g_ref = refs[p] if norm else None
    p += norm
    w_refs = refs[p:p + n_w]
    p += n_w
    r_ref, rs_ref = (refs[p], refs[p + 1]) if has_res else (None, None)
    p += 2 * has_res
    o_ref, os_ref = refs[p], refs[p + 1]
    p += 2
    ob_ref, obs_ref = (refs[p], refs[p + 1]) if emit_bf16 else (None, None)
    p += 2 * emit_bf16
    wb_refs = refs[p:]
    contract = (((1,), (1 if w_t else 0,)), ((), ()))

    def apply(x_ref, r_ref, o_ref, ob_ref):
        if len(x_ref.shape) == 3:
            x = jnp.concatenate([x_ref[c] for c in range(x_ref.shape[0])], axis=1).astype(BF16)
        else:
            x = x_ref[...].astype(BF16)
        accs = [lax.dot_general(x, wb_ref[...], contract, preferred_element_type=F32) for wb_ref in wb_refs]
        if norm:
            rstd = _row_rstd(x)
            accs = [acc * rstd for acc in accs]
        out = accs[0] * jax.nn.sigmoid(accs[0]) * accs[1] if swiglu else accs[0]
        if has_res:
            out = r_ref[...] + out
        _store_cols(o_ref, out.astype(o_ref.dtype))
        if emit_bf16:
            ob_ref[...] = out.astype(BF16)

    @pl.when(pl.program_id(m_axis) == 0)
    def _():
        for w_ref, wb_ref in zip(w_refs, wb_refs):
            w = w_ref[...] * g_ref[...] if norm else w_ref[...]
            wb_ref[...] = w.astype(BF16)
        apply(xs_ref, rs_ref, os_ref, obs_ref)

    apply(x_ref, r_ref, o_ref, ob_ref)


def linear(x, xs, ws, w_index, outer, out_shape, out_block, out_index, tn, *, res=None, res_s=None,
           swiglu=False, out_dtype=F32, w_t=False, norm_gain=None, emit_bf16=False, name="linear"):
    if x.ndim == 3:
        m, k = x.shape[1], x.shape[0] * x.shape[2]
        x_spec = lambda tm: pl.BlockSpec((x.shape[0], tm, x.shape[2]), lambda *g: (0, g[nd], 0))
    else:
        m, k = x.shape
        x_spec = lambda tm: pl.BlockSpec((tm, k), lambda *g: (g[nd], 0))
    ms = xs.shape[0]
    nd = len(outer)
    has_res = res is not None
    norm = norm_gain is not None
    bf16_bytes = jnp.dtype(BF16).itemsize
    x_bytes, o_bytes = x.dtype.itemsize, jnp.dtype(out_dtype).itemsize
    row_bytes = (k * x_bytes + tn * (o_bytes + (res.dtype.itemsize if has_res else 0))
                 + (tn * bf16_bytes if emit_bf16 else 0))
    w_bytes = len(ws) * k * tn * (2 * ws[0].dtype.itemsize + bf16_bytes)
    acc_bytes = len(ws) * tn * jnp.dtype(F32).itemsize
    tm = _row_tile(m, w_bytes + 2 * ms * row_bytes, row_bytes, acc_bytes)
    w_block = (None, tn, k) if w_t else (None, k, tn)
    main_map = lambda *g: out_index(*g)
    sample_map = lambda *g: out_index(*g[:nd], 0)
    in_specs = [x_spec(tm), pl.BlockSpec((ms, k), lambda *g: (0, 0))]
    args = [x, xs]
    if norm:
        gain = norm_gain.astype(F32).reshape((1, k) if w_t else (k, 1))
        in_specs.append(pl.BlockSpec(gain.shape, lambda *g: (0, 0)))
        args.append(gain)
    in_specs += [pl.BlockSpec(w_block, lambda *g: w_index(*g[:nd])) for _ in ws]
    args += ws
    if has_res:
        in_specs += [pl.BlockSpec(out_block(tm), main_map), pl.BlockSpec(out_block(ms), sample_map)]
        args += [res, res_s]
    out_specs = [pl.BlockSpec(out_block(tm), main_map), pl.BlockSpec(out_block(ms), sample_map)]
    out_shapes = [jax.ShapeDtypeStruct(out_shape(m), out_dtype), jax.ShapeDtypeStruct(out_shape(ms), out_dtype)]
    if emit_bf16:
        out_specs += [pl.BlockSpec(out_block(tm), main_map), pl.BlockSpec(out_block(ms), sample_map)]
        out_shapes += [jax.ShapeDtypeStruct(out_shape(m), BF16), jax.ShapeDtypeStruct(out_shape(ms), BF16)]
    return pl.pallas_call(
        functools.partial(_linear_kernel, n_w=len(ws), has_res=has_res, swiglu=swiglu, m_axis=nd, w_t=w_t,
                          norm=norm, emit_bf16=emit_bf16),
        grid=(*outer, m // tm),
        in_specs=in_specs,
        out_specs=out_specs,
        out_shape=out_shapes,
        scratch_shapes=[pltpu.VMEM(w_block[1:], BF16) for _ in ws],
        compiler_params=_cparams(("parallel",) * nd + ("arbitrary",)),
        name=name,
    )(*args)


def dense(x, xs, w, layer, *, res=None, res_s=None, out_dtype=F32, n_cols=None, w_t=False,
          norm_gain=None, emit_bf16=False, name="dense"):
    k = x.shape[1] if x.ndim == 2 else x.shape[0] * x.shape[2]
    n = n_cols or w.shape[1 if w_t else 2]
    tn = _pick(n, (1024, 512, 256, 128)) if k <= 2048 else _pick(n, (512, 256, 128))
    w_index = (lambda j: (layer, j, 0)) if w_t else (lambda j: (layer, 0, j))
    return linear(x, xs, [w], w_index, (n // tn,), lambda m: (m, n),
                  lambda tm: (tm, tn), lambda j, i: (i, j), tn, res=res, res_s=res_s,
                  out_dtype=out_dtype, w_t=w_t, norm_gain=norm_gain, emit_bf16=emit_bf16, name=name)


def ffn_up(x, xs, gain, wg, wu, layer):
    n = wg.shape[2]
    tn = _pick(n, (512, 256, 128))
    return linear(x, xs, [wg, wu], lambda j: (layer, 0, j), (n // tn,), lambda m: (m, n),
                  lambda tm: (tm, tn), lambda j, i: (i, j), tn, swiglu=True, out_dtype=BF16,
                  norm_gain=gain, name="ffn_up")


def attn_q_proj(x, xs, gain, w, layer):
    he = A_HEADS * A_HEAD_DIM
    return linear(x, xs, [w], lambda g: (layer, 0, 3 * g), (N_GROUPS,),
                  lambda m: (N_GROUPS * A_HEADS, m, A_HEAD_DIM),
                  lambda tm: (A_HEADS, tm, A_HEAD_DIM), lambda g, i: (g, i, 0), he,
                  norm_gain=gain, name="attn_q_proj")


def _kv_proj_kernel(*refs, tiles_per_batch, first_kept, keep_blk, n_prev):
    x_ref, xs_ref, g_ref, w_ref = refs[:4]
    prev_ref = refs[4] if n_prev else None
    o_ref, os_ref, f_ref, fs_ref, wb_ref = refs[4 + bool(n_prev):]
    i = pl.program_id(1)
    nh, e = f_ref.shape[-2:]

    def keep_rows(acc):
        if n_prev:
            f_ref[pl.ds(0, n_prev)] = prev_ref[...]
        split_heads(f_ref.at[n_prev], acc[acc.shape[0] - keep_blk:])

    def split_heads(dst_ref, rows):
        flat = dst_ref.reshape(rows.shape[0] * nh, e)
        for h in range(nh):
            flat[pl.ds(h, rows.shape[0], stride=nh), :] = rows[:, h * e:(h + 1) * e]

    def project(x):
        return jnp.dot(x, wb_ref[...], preferred_element_type=F32) * _row_rstd(x)

    @pl.when(i == 0)
    def _():
        wb_ref[...] = (w_ref[...] * g_ref[...]).astype(BF16)
        acc = project(xs_ref[...])
        _store_cols(os_ref, acc)
        split_heads(fs_ref, acc)

    acc = project(x_ref[...])
    _store_cols(o_ref, acc)

    if first_kept == 0:
        keep_rows(acc)
    else:
        @pl.when(i % tiles_per_batch >= first_kept)
        def _():
            keep_rows(acc)


def attn_kv_proj(x, xs, gain, w, layer, group, batch, seq, keep, prev_kept=None):
    m, k = x.shape
    ms = xs.shape[0]
    he = A_HEADS * A_HEAD_DIM
    n_prev = 0 if prev_kept is None else prev_kept.shape[0]
    f32_bytes = jnp.dtype(F32).itemsize
    w_bytes = k * he * (2 * w.dtype.itemsize + jnp.dtype(BF16).itemsize)
    kept_bytes = lambda rows: 2 * min(keep, rows) * he * f32_bytes * (1 + 2 * n_prev)
    row_bytes = k * x.dtype.itemsize + he * f32_bytes
    small_bytes = 2 * k * LANES * f32_bytes + 2 * ms * (row_bytes + he * f32_bytes)
    tm = next(t for t in (1024, 512, 256, 128)
              if _row_tile(t, w_bytes + small_bytes + kept_bytes(t), row_bytes, 2 * he * f32_bytes) == t
              and seq % t == 0)
    keep_blk = min(keep, tm)
    assert keep % keep_blk == 0 and tm % keep_blk == 0
    tpb = seq // tm
    first_kept = tpb - keep // keep_blk if keep >= tm else tpb - 1

    def kept_map(t, i):
        return (0, i // tpb, jnp.maximum(i % tpb - first_kept, 0), t, 0, 0)

    kept_block = lambda layers: (layers, None, keep_blk, None, A_HEADS, A_HEAD_DIM)
    in_specs = [pl.BlockSpec((tm, k), lambda t, i: (i, 0)),
                pl.BlockSpec((ms, k), lambda t, i: (0, 0)),
                pl.BlockSpec((k, 1), lambda t, i: (0, 0)),
                pl.BlockSpec((None, k, he), lambda t, i: (layer, 0, 3 * group + 1 + t))]
    args = [x, xs, gain.astype(F32).reshape(k, 1), w]
    if n_prev:
        in_specs.append(pl.BlockSpec(kept_block(n_prev), kept_map))
        args.append(prev_kept)
    return pl.pallas_call(
        functools.partial(_kv_proj_kernel, tiles_per_batch=tpb, first_kept=first_kept, keep_blk=keep_blk,
                          n_prev=n_prev),
        grid=(2, m // tm),
        in_specs=in_specs,
        out_specs=[pl.BlockSpec((A_HEADS, tm, A_HEAD_DIM), lambda t, i: (t, i, 0)),
                   pl.BlockSpec((A_HEADS, ms, A_HEAD_DIM), lambda t, i: (t, 0, 0)),
                   pl.BlockSpec(kept_block(n_prev + 1), kept_map),
                   pl.BlockSpec((ms, None, A_HEADS, A_HEAD_DIM), lambda t, i: (0, t, 0, 0))],
        out_shape=[jax.ShapeDtypeStruct((2 * A_HEADS, m, A_HEAD_DIM), F32),
                   jax.ShapeDtypeStruct((2 * A_HEADS, ms, A_HEAD_DIM), F32),
                   jax.ShapeDtypeStruct((n_prev + 1, batch, keep, 2, A_HEADS, A_HEAD_DIM), F32),
                   jax.ShapeDtypeStruct((ms, 2, A_HEADS, A_HEAD_DIM), F32)],
        scratch_shapes=[pltpu.VMEM((k, he), BF16)],
        compiler_params=_cparams(("parallel", "arbitrary")),
        name="attn_kv_proj",
    )(*args)


def _gate_kernel(x_ref, xs_ref, g_ref, w_ref, o_ref, os_ref):
    w = (w_ref[...] * g_ref[...]).astype(BF16)
    contract = (((1,), (1,)), ((), ()))

    def project(x):
        return lax.dot_general(x, w, contract, preferred_element_type=F32) * _row_rstd(x)

    @pl.when(pl.program_id(0) == 0)
    def _():
        os_ref[...] = project(xs_ref[...])

    o_ref[...] = project(x_ref[...])


def gate_proj(x, xs, gain, w_t, layer, row0, n):
    m, k = x.shape
    ms = xs.shape[0]
    assert row0 % n == 0
    tm = _pick(m, (512, 256, 128, 64))
    return pl.pallas_call(
        _gate_kernel,
        grid=(m // tm,),
        in_specs=[pl.BlockSpec((tm, k), lambda i: (i, 0)),
                  pl.BlockSpec((ms, k), lambda i: (0, 0)),
                  pl.BlockSpec((1, k), lambda i: (0, 0)),
                  pl.BlockSpec((None, n, k), lambda i: (layer, row0 // n, 0))],
        out_specs=[pl.BlockSpec((tm, n), lambda i: (i, 0)), pl.BlockSpec((ms, n), lambda i: (0, 0))],
        out_shape=[jax.ShapeDtypeStruct((m, n), F32), jax.ShapeDtypeStruct((ms, n), F32)],
        compiler_params=_cparams(("arbitrary",)),
        name="gate_proj",
    )(x, xs, gain.astype(F32).reshape(1, k), w_t)


def _t5_bucket(dist):
    max_exact = NUM_BUCKETS // 2
    d = jnp.maximum(dist.astype(F32), 1.0)
    large = max_exact + jnp.log(d / max_exact) / math.log(MAX_DISTANCE / max_exact) * (NUM_BUCKETS - max_exact)
    large = jnp.minimum(large.astype(jnp.int32), NUM_BUCKETS - 1)
    return jnp.where(dist < max_exact, dist, large)


def _group_bias(rel_bias, g):
    dil = A_GROUPS[g][1]
    buckets = _t5_bucket(jnp.arange(A_STEPS + 1, dtype=jnp.int32) * dil)
    onehot = (buckets[:, None] == jnp.arange(NUM_BUCKETS)[None, :]).astype(F32)
    cols = rel_bias[:, g * A_HEADS:(g + 1) * A_HEADS].astype(F32)
    return jnp.einsum("kb,bh->hk", onehot, cols, precision=lax.Precision.HIGHEST)


def _toeplitz(f, lo, n_rows, n_cols, offset):
    first, last = offset - n_cols + 1, offset + n_rows - 1
    pad_lo, pad_hi = max(lo - first, 0), max(last - (lo + f.shape[-1] - 1), 0)
    fe = jnp.pad(f, [(0, 0)] * (f.ndim - 1) + [(pad_lo, pad_hi)], constant_values=NEG)
    base = lo - pad_lo
    ext = lambda a, b: fe[..., a - base:b - base]
    period = n_rows + n_cols - 1
    w = jnp.concatenate([ext(first, offset + 1)[..., ::-1], ext(offset + 1, last + 1)[..., ::-1]], axis=-1)
    flat = jnp.tile(w, n_rows)[..., :n_rows * (period - 1)]
    return flat.reshape(*f.shape[:-1], n_rows, period - 1)[..., :n_cols]


def _band_tables(biases):
    return jnp.stack([_toeplitz(b, 0, A_BLOCK, 2 * A_BLOCK, A_BLOCK) for b in biases], axis=1)


def _rows(start, size, stride):
    return pl.ds(start, size) if stride == 1 else pl.ds(start, size, stride=stride)


def _masked_logits(q, k, tab, scale):
    s = lax.dot_general(q.astype(BF16), k.astype(BF16), (((1,), (1,)), ((), ())),
                        preferred_element_type=F32) * scale
    return jnp.where(tab > 0.5 * NEG, s + tab, NEG)


def _attn_prompt_kernel(*refs, seq):
    qkv = refs[:3 * N_GROUPS]
    band_ref, y_ref, o_scr, lse_scr = refs[3 * N_GROUPS:]
    scale = A_HEAD_DIM ** -0.5
    q_blk = A_BLOCK

    def logits(g, dil, q_start, k_start, n_keys):
        q_ref, k_ref, v_ref = qkv[3 * g:3 * g + 3]
        q = q_ref[_rows(q_start, q_blk, dil), :]
        k = k_ref[_rows(k_start, n_keys, dil), :]
        v = v_ref[_rows(k_start, n_keys, dil), :]
        tab = band_ref[g] if n_keys == 2 * q_blk else band_ref[g, :, q_blk:]
        return _masked_logits(q, k, tab, scale), v.astype(BF16)

    def finish(g, dil, q_start, s, v):
        m = jnp.max(s, axis=-1, keepdims=True)
        p = jnp.exp(s - m)
        l = jnp.sum(p, axis=-1, keepdims=True)
        o = jnp.dot(p.astype(BF16), v, preferred_element_type=F32)
        rows = _rows(q_start, q_blk, dil)
        o_scr[g, rows, :] = o / l
        lse_scr[g, rows, :] = jnp.broadcast_to(m + jnp.log(l), (q_blk, LANES))

    blocks = []
    for g, (_, dil) in enumerate(A_GROUPS):
        for r in range(dil):
            blocks.append((g, dil, r, r, q_blk))
            for n in range(1, seq // dil // q_blk):
                start = r + dil * q_blk * (n - 1)
                blocks.append((g, dil, start + dil * q_blk, start, 2 * q_blk))
    pending = []
    for g, dil, q_start, k_start, n_keys in blocks:
        s, v = logits(g, dil, q_start, k_start, n_keys)
        pending.append((g, dil, q_start, s, v))
        if len(pending) > LOGITS_AHEAD:
            finish(*pending.pop(0))
    for item in pending:
        finish(*item)

    chunk = 256

    def merge(c, carry):
        rows = pl.ds(pl.multiple_of(c * chunk, chunk), chunk)
        lses = [lse_scr[g, rows, :] for g in range(N_GROUPS)]
        top = functools.reduce(jnp.maximum, lses)
        ws = [jnp.exp(lse - top) for lse in lses]
        num = functools.reduce(jnp.add, [w * o_scr[g, rows, :] for g, w in enumerate(ws)])
        y_ref[rows, :] = (num / functools.reduce(jnp.add, ws)).astype(y_ref.dtype)
        return carry

    lax.fori_loop(0, seq // chunk, merge, 0)


def attention_prompt(q, kv, band, batch, seq):
    e = A_HEAD_DIM
    assert all(seq % (dil * A_BLOCK) == 0 for _, dil in A_GROUPS)
    in_specs, args = [], []
    for g in range(N_GROUPS):
        in_specs.append(pl.BlockSpec((None, seq, e), lambda b, h, g=g: (g * A_HEADS + h, b, 0)))
        in_specs.append(pl.BlockSpec((None, seq, e), lambda b, h: (h, b, 0)))
        in_specs.append(pl.BlockSpec((None, seq, e), lambda b, h: (A_HEADS + h, b, 0)))
        args += [q, kv[g], kv[g]]
    in_specs.append(pl.BlockSpec((None, N_GROUPS, A_BLOCK, 2 * A_BLOCK), lambda b, h: (h, 0, 0, 0)))
    return pl.pallas_call(
        functools.partial(_attn_prompt_kernel, seq=seq),
        grid=(batch, A_HEADS),
        in_specs=in_specs,
        out_specs=pl.BlockSpec((None, seq, e), lambda b, h: (h, b, 0)),
        out_shape=jax.ShapeDtypeStruct((A_HEADS, batch * seq, e), BF16),
        scratch_shapes=[pltpu.VMEM((N_GROUPS, seq, e), F32),
                        pltpu.VMEM((N_GROUPS, seq, LANES), F32)],
        compiler_params=_cparams(("parallel", "parallel")),
        name="attention_prompt",
    )(*args, band)


def _attn_sample_kernel(*refs):
    q_ref = refs[0]
    kv_new = refs[1:1 + N_GROUPS]
    caches = refs[1 + N_GROUPS:1 + 3 * N_GROUPS]
    tables = refs[1 + 3 * N_GROUPS:1 + 5 * N_GROUPS]
    y_ref = refs[1 + 5 * N_GROUPS]
    scale = A_HEAD_DIM ** -0.5
    e = A_HEAD_DIM

    def head_rows(ref, h):
        rows = math.prod(ref.shape[:-2])
        return ref.reshape(rows * A_HEADS, e)[pl.ds(h, rows, stride=A_HEADS), :]

    def logits(h, g):
        q = q_ref[g * A_HEADS + h]
        s_old = _masked_logits(q, head_rows(caches[2 * g], h), tables[2 * g][h], scale)
        s_new = _masked_logits(q, kv_new[g][h], tables[2 * g + 1][h], scale)
        return s_old, s_new

    results = {}

    def finish(h, g, s_old, s_new):
        v_old = head_rows(caches[2 * g + 1], h)
        v_new = kv_new[g][A_HEADS + h]
        m = jnp.maximum(jnp.max(s_old, axis=-1, keepdims=True), jnp.max(s_new, axis=-1, keepdims=True))
        p_old = jnp.exp(s_old - m)
        p_new = jnp.exp(s_new - m)
        l = jnp.sum(p_old, axis=-1, keepdims=True) + jnp.sum(p_new, axis=-1, keepdims=True)
        o = (jnp.dot(p_old.astype(BF16), v_old.astype(BF16), preferred_element_type=F32)
             + jnp.dot(p_new.astype(BF16), v_new.astype(BF16), preferred_element_type=F32))
        results[h, g] = (o / l, m + jnp.log(l))
        if g == N_GROUPS - 1:
            outs, lses = zip(*[results.pop((h, gg)) for gg in range(N_GROUPS)])
            top = functools.reduce(jnp.maximum, lses)
            ws = [jnp.exp(lse - top) for lse in lses]
            num = functools.reduce(jnp.add, [w * o for w, o in zip(ws, outs)])
            y_ref[:, h * e:(h + 1) * e] = (num / functools.reduce(jnp.add, ws)).astype(y_ref.dtype)

    pending = []
    for h in range(A_HEADS):
        for g in range(N_GROUPS):
            pending.append((h, g, *logits(h, g)))
            if len(pending) > SAMPLE_LOGITS_AHEAD:
                finish(*pending.pop(0))
    for item in pending:
        finish(*item)


def attention_sample(q, kv_new, caches, layer, biases, batch, t_new):
    he = A_HEADS * A_HEAD_DIM
    in_specs = [pl.BlockSpec((N_GROUPS * A_HEADS, t_new, A_HEAD_DIM), lambda b: (0, b, 0))]
    args = [q]
    for g in range(N_GROUPS):
        in_specs.append(pl.BlockSpec((2 * A_HEADS, t_new, A_HEAD_DIM), lambda b: (0, b, 0)))
        args.append(kv_new[g])
    tables = []
    for g, (_, dil) in enumerate(A_GROUPS):
        c = caches[g]
        n_past = c.shape[2]
        bg = biases[g]
        by_dist = jnp.concatenate([bg[..., None], jnp.full(bg.shape + (dil - 1,), NEG, F32)], axis=-1)
        by_dist = by_dist.reshape(bg.shape[0], -1)[:, :A_STEPS * dil + 1]
        tab = _toeplitz(by_dist, 0, t_new, n_past + t_new, n_past)
        tab_old, tab_new = tab[:, :, :n_past], tab[:, :, n_past:]
        if dil % t_new == 0 and n_past % dil == 0:
            c = c.reshape(c.shape[0], batch, n_past // dil, dil, 2, A_HEADS, A_HEAD_DIM)
            for t in range(2):
                in_specs.append(pl.BlockSpec((None, None, n_past // dil, t_new, None, A_HEADS, A_HEAD_DIM),
                                             lambda b, t=t: (layer, b, 0, 0, t, 0, 0)))
                args.append(c)
            tab_old = tab_old.reshape(A_HEADS, t_new, n_past // dil, dil)[..., :t_new]
            tab_old = tab_old.reshape(A_HEADS, t_new, (n_past // dil) * t_new)
        else:
            for t in range(2):
                in_specs.append(pl.BlockSpec((None, None, n_past, None, A_HEADS, A_HEAD_DIM),
                                             lambda b, t=t: (layer, b, 0, t, 0, 0)))
                args.append(c)
        tables += [tab_old, tab_new]
    for tab in tables:
        in_specs.append(pl.BlockSpec(tab.shape, lambda b: (0, 0, 0)))
        args.append(tab)
    return pl.pallas_call(
        _attn_sample_kernel,
        grid=(batch,),
        in_specs=in_specs,
        out_specs=pl.BlockSpec((t_new, he), lambda b: (b, 0)),
        out_shape=jax.ShapeDtypeStruct((batch * t_new, he), F32),
        compiler_params=_cparams(("parallel",)),
        name="attention_sample",
    )(*args)


def _log_sigmoid(x):
    return jnp.minimum(x, 0.0) - jnp.log1p(jnp.exp(-jnp.abs(x)))


def _mlstm_kernel(q_ref, k_ref, v_ref, og_ref, gc_ref, gr_ref, bc_ref, br_ref, gn_ref,
                  c0_ref, n0_ref, m0_ref,
                  h_ref, c_out, n_out, m_out,
                  c_scr, n_scr, m_scr, *, chunk):
    c = pl.program_id(1)
    nh, dk, dv = c_scr.shape

    @pl.when(c == 0)
    def _():
        c_scr[...] = c0_ref[...].astype(F32)
        n_scr[...] = n0_ref[...].astype(F32)
        m_scr[...] = m0_ref[...].astype(F32)

    gc = gc_ref[...] + bc_ref[...]
    gr = gr_ref[...] + br_ref[...]
    logf_c = _log_sigmoid(gc[:, nh:])
    logf_r = _log_sigmoid(gr[nh:, :])
    qi = lax.broadcasted_iota(jnp.int32, (chunk, chunk), 0)
    si = lax.broadcasted_iota(jnp.int32, (chunk, chunk), 1)
    causal = si <= qi
    upper = qi <= si
    m_all, n_all = m_scr[...], n_scr[...]
    m_news, n_news = [], []

    def state_stage(h):
        i_col, i_row = gc[:, h:h + 1], gr[h:h + 1, :]
        logf_col, logf_row = logf_c[:, h:h + 1], logf_r[h:h + 1, :]
        b_col = jnp.sum(jnp.where(causal, logf_row, 0.0), axis=1, keepdims=True)
        b_row = jnp.sum(jnp.where(upper, logf_col, 0.0), axis=0, keepdims=True)
        b_last = jnp.sum(logf_row, axis=1, keepdims=True)

        m_prev = m_all[h:h + 1, :]
        a = b_col + m_prev
        dmat = jnp.where(causal, b_col - b_row + i_row, NEG)
        m_t = jnp.maximum(a, jnp.max(dmat, axis=1, keepdims=True))
        w_inter = jnp.exp(a - m_t)

        qf = q_ref[:, h * dk:(h + 1) * dk].astype(F32)
        q = qf.astype(BF16)
        kf = k_ref[:, h * dk:(h + 1) * dk].astype(F32) * (dk ** -0.5)
        k = kf.astype(BF16)
        v = v_ref[:, h * dv:(h + 1) * dv].astype(BF16)
        c_prev = c_scr[h]
        n_prev = n_all[h:h + 1, :]
        qk = lax.dot_general(q, k, (((1,), (1,)), ((), ())), preferred_element_type=F32)
        qc = jnp.dot(q, c_prev.astype(BF16), preferred_element_type=F32)
        qn = jnp.sum(qf * n_prev, axis=1, keepdims=True)

        m_new = jnp.maximum(b_last + m_prev, jnp.max(b_last - b_row + i_row, axis=1, keepdims=True))
        w_state = jnp.exp(b_last - b_col + i_col - m_new)
        decay = jnp.exp(b_last + m_prev - m_new)
        kw = (w_state * kf).astype(BF16)
        n_news.append(decay * n_prev + jnp.sum(w_state * kf, axis=0, keepdims=True))
        m_news.append(m_new)
        return h, qk, dmat, m_t, w_inter, qc, qn, v, kw, decay * c_prev

    def output_stage(h, qk, dmat, m_t, w_inter, qc, qn, v, kw, c_decayed):
        s = qk * jnp.exp(dmat - m_t)
        num = w_inter * qc + jnp.dot(s.astype(BF16), v, preferred_element_type=F32)
        c_scr[h] = c_decayed + lax.dot_general(kw, v, (((0,), (0,)), ((), ())), preferred_element_type=F32)
        den = w_inter * qn + jnp.sum(s, axis=1, keepdims=True)
        hh = num / jnp.maximum(jnp.abs(den), jnp.exp(-m_t))
        ht = hh * jax.nn.sigmoid(og_ref[:, h * dv:(h + 1) * dv].astype(F32))
        hn = ht * lax.rsqrt(jnp.mean(ht * ht, axis=-1, keepdims=True) + EPS)
        h_ref[:, h * dv:(h + 1) * dv] = (hn * gn_ref[:, h * dv:(h + 1) * dv]).astype(h_ref.dtype)

    pending = []
    for h in range(nh):
        pending.append(state_stage(h))
        if len(pending) > HEADS_AHEAD:
            output_stage(*pending.pop(0))
    for item in pending:
        output_stage(*item)

    n_scr[...] = jnp.concatenate(n_news, axis=0)
    m_scr[...] = jnp.concatenate(m_news, axis=0)

    @pl.when(c == pl.num_programs(1) - 1)
    def _():
        c_out[...] = c_scr[...].astype(c_out.dtype)
        n_out[...] = n_scr[...].astype(n_out.dtype)
        m_out[...] = m_scr[...].astype(m_out.dtype)


def mlstm(z, gates, b_gate, g_norm, state, batch, seq, chunk, out_dtype):
    c0, n0, m0 = state
    nh = B_HEADS
    dk, dv = c0.shape[-2], c0.shape[-1]
    nc = seq // chunk
    qk_w, v_w = nh * dk, nh * dv
    assert v_w % qk_w == 0
    gcol = gates.reshape(batch * nc, chunk, 2 * nh)
    grow = jnp.swapaxes(gcol, 1, 2)
    row = lambda b, c: b * nc + c
    state_specs = [
        pl.BlockSpec((None, nh, dk, dv), lambda b, c: (b, 0, 0, 0)),
        pl.BlockSpec((None, nh, dk), lambda b, c: (b, 0, 0)),
        pl.BlockSpec((None, nh, 1), lambda b, c: (b, 0, 0)),
    ]
    in_specs = [
        pl.BlockSpec((chunk, qk_w), lambda b, c: (row(b, c), 0)),
        pl.BlockSpec((chunk, qk_w), lambda b, c: (row(b, c), 1)),
        pl.BlockSpec((chunk, v_w), lambda b, c: (row(b, c), 2 * qk_w // v_w)),
        pl.BlockSpec((chunk, v_w), lambda b, c: (row(b, c), 2 * qk_w // v_w + 1)),
        pl.BlockSpec((None, chunk, 2 * nh), lambda b, c: (row(b, c), 0, 0)),
        pl.BlockSpec((None, 2 * nh, chunk), lambda b, c: (row(b, c), 0, 0)),
        pl.BlockSpec((1, 2 * nh), lambda b, c: (0, 0)),
        pl.BlockSpec((2 * nh, 1), lambda b, c: (0, 0)),
        pl.BlockSpec((1, v_w), lambda b, c: (0, 0)),
    ] + state_specs
    out_shape = [
        jax.ShapeDtypeStruct((batch * seq, v_w), out_dtype),
        jax.ShapeDtypeStruct((batch, nh, dk, dv), F32),
        jax.ShapeDtypeStruct((batch, nh, dk), F32),
        jax.ShapeDtypeStruct((batch, nh, 1), F32),
    ]
    bg = b_gate.astype(F32)
    hn, c_new, n_new, m_new = pl.pallas_call(
        functools.partial(_mlstm_kernel, chunk=chunk),
        grid=(batch, nc),
        in_specs=in_specs,
        out_specs=[pl.BlockSpec((chunk, v_w), lambda b, c: (row(b, c), 0))] + state_specs,
        out_shape=out_shape,
        scratch_shapes=[pltpu.VMEM((nh, dk, dv), F32), pltpu.VMEM((nh, dk), F32), pltpu.VMEM((nh, 1), F32)],
        compiler_params=_cparams(("parallel", "arbitrary")),
        name="mlstm",
    )(z, z, z, z, gcol, grow, bg.reshape(1, 2 * nh), bg.reshape(2 * nh, 1),
      g_norm.astype(F32).reshape(1, v_w),
      c0.reshape(batch, nh, dk, dv), n0.reshape(batch, nh, dk), m0.reshape(batch, nh, 1))
    return hn, (c_new, n_new, m_new.reshape(batch, nh))


def kernel(x_prompt, x_sample, cache_kv_w128, cache_kv_w512, cache_kv_w2048, state_C, state_n, state_m,
           norm_mix, norm_ffn, norm_final, rel_bias, w_a_in, w_a_out, w_b_in, b_b_gate, b_b_norm, w_b_out,
           w_ffn_gate, w_ffn_up, w_ffn_down):
    bp, seq, d_model = x_prompt.shape
    bs, t_new, _ = x_sample.shape
    depth = norm_mix.shape[0]
    nh, dk, dv = state_C.shape[2], state_C.shape[3], state_C.shape[4]
    main_w = 2 * nh * dk + 2 * nh * dv
    caches = (cache_kv_w128, cache_kv_w512, cache_kv_w2048)
    w_b_in_t = jnp.swapaxes(w_b_in, 1, 2)

    biases =[_group_bias(rel_bias, g) for g in range(N_GROUPS)]
    band = _band_tables(biases)

    xp = x_prompt.reshape(bp * seq, d_model)
    xs = x_sample.reshape(bs * t_new, d_model)
    xbp, xbs = round_bf16(xp, xs)
    kv_p = [None] * N_GROUPS
    kv_s = [[] for _ in range(N_GROUPS)]
    st_p, st_s = [], []
    for layer in range(depth):
        j = layer // 2
        gain = norm_mix[layer]
        if layer % 2 == 0:
            qp, qs = attn_q_proj(xbp, xbs, gain, w_a_in, j)
            kvp, kvs = [], []
            for g, (window, _) in enumerate(A_GROUPS):
                kp, ks, kv_p[g], fs = attn_kv_proj(xbp, xbs, gain, w_a_in, j, g, bp, seq, min(window, seq),
                                                   prev_kept=kv_p[g])
                kvp.append(kp)
                kvs.append(ks)
                kv_s[g].append(fs.reshape(bs, t_new, 2, A_HEADS, A_HEAD_DIM))
            yp = attention_prompt(qp, kvp, band, bp, seq)
            ys = attention_sample(qs, kvs, caches, j, biases, bs, t_new)
            xp, xs, xbp, xbs = dense(yp, ys, w_a_out, j, res=xp, res_s=xs, emit_bf16=True,
                                     name="attn_out_proj")
        else:
            zero = (jnp.zeros((bp, nh, dk, dv), F32), jnp.zeros((bp, nh, dk), F32), jnp.zeros((bp, nh), F32))
            past = (state_C[j], state_n[j], state_m[j])
            zp, zs = dense(xbp, xbs, w_b_in_t, j, n_cols=main_w, w_t=True, norm_gain=gain,
                           name="mlstm_in_proj")
            gp, gs = gate_proj(xbp, xbs, gain, w_b_in_t, j, main_w, 2 * nh)
            hnp, sp = mlstm(zp, gp, b_b_gate[j], b_b_norm[j], zero, bp, seq, min(PROMPT_CHUNK, seq), BF16)
            hns, ss = mlstm(zs, gs, b_b_gate[j], b_b_norm[j], past, bs, t_new, t_new, F32)
            xp, xs, xbp, xbs = dense(hnp, hns, w_b_out, j, res=xp, res_s=xs, emit_bf16=True,
                                     name="mlstm_out_proj")
            st_p.append(sp)
            st_s.append(ss)
        ap, as_ = ffn_up(xbp, xbs, norm_ffn[layer], w_ffn_gate, w_ffn_up, layer)
        if layer + 1 < depth:
            xp, xs, xbp, xbs = dense(ap, as_, w_ffn_down, layer, res=xp, res_s=xs, emit_bf16=True,
                                     name="ffn_down")
        else:
            xp, xs = dense(ap, as_, w_ffn_down, layer, res=xp, res_s=xs, name="ffn_down")

    y_prompt, y_sample = rmsnorm(xp, xs, norm_final, F32)
    y_prompt = y_prompt.reshape(bp, seq, d_model)
    y_sample = y_sample.reshape(bs, t_new, d_model)
    kvd = cache_kv_w128.dtype
    sd = state_C.dtype
    kv_out_p = [kv_p[g].astype(kvd) for g in range(N_GROUPS)]
    kv_out_s = [jnp.stack(kv_s[g]).astype(kvd) for g in range(N_GROUPS)]
    states_p = [jnp.stack([s[i] for s in st_p]).astype(sd) for i in range(3)]
    states_s = [jnp.stack([s[i] for s in st_s]).astype(sd) for i in range(3)]
    return (y_prompt, y_sample, *kv_out_p, *kv_out_s, *states_p, *states_s)
```
